```python
import numpy as np
import jax
import jax.numpy as jnp
from jax import lax

D_MODEL = 1024
BATCH = 8
SEQ = 4096
DEPTH = 2

HEAD_DIM = 64
ROT_DIM = HEAD_DIM // 4
ROPE_THETA = 500000.0
Q_BLOCK = 128
NEG = -1e30
EPS = 1e-6

NSA_HEADS = 8
NSA_GROUPS = 2
NSA_REP = NSA_HEADS // NSA_GROUPS
CMP_LEN = 32
CMP_STRIDE = 16
CMP_HIDDEN = 256
SEL_LEN = 64
SEL_BLOCKS = 16
WINDOW = 512

DSA_HEADS = 8
IDX_HEADS = 8
IDX_DIM = 64
DSA_TOPK = 256

PEER_HEADS = 8
PEER_QDIM = 256
N_KEYS = 128
N_EXPERTS = N_KEYS * N_KEYS
PEER_HALF_TOPK = 16
PEER_TOPK = 16
PEER_TOKEN_BLOCK = 128

PLE_DIM = 256

MIX_A = NSA_HEADS * HEAD_DIM
MIX_B = DSA_HEADS * HEAD_DIM
IN_SIZES = (
    MIX_A,
    6 * NSA_GROUPS * HEAD_DIM,
    3 * NSA_HEADS,
    MIX_B,
    2 * HEAD_DIM,
    IDX_HEADS * IDX_DIM,
    IDX_DIM,
    IDX_HEADS,
    2 * D_MODEL,
)
IN_COLS = sum(IN_SIZES)

kernel_name = 'hybrid_nsa_dsa_peer_ple'


def _rms(x):
    xf = x.astype(jnp.float32)
    return xf * lax.rsqrt(jnp.mean(xf * xf, axis=-1, keepdims=True) + EPS)


def rms_norm(x, g):
    return (_rms(x) * g.astype(jnp.float32)).astype(x.dtype)


def rope(x, pos):
    half = ROT_DIM // 2
    inv = ROPE_THETA ** (-jnp.arange(half, dtype=jnp.float32) / half)
    ang = pos.astype(jnp.float32)[..., None] * inv
    cos = jnp.cos(ang)[:, :, None, :]
    sin = jnp.sin(ang)[:, :, None, :]
    xf = x.astype(jnp.float32)
    x1, x2, rest = xf[..., :half], xf[..., half:ROT_DIM], xf[..., ROT_DIM:]
    return jnp.concatenate([x1 * cos - x2 * sin, x2 * cos + x1 * sin, rest], axis=-1).astype(x.dtype)


def masked_softmax(s, mask):
    p = jax.nn.softmax(jnp.where(mask, s, NEG), axis=-1)
    return jnp.where(mask, p, 0.0)


def compress(t, pos_emb, w1, w2):
    b, s, g, d = t.shape
    r = CMP_LEN // CMP_STRIDE
    n_chunk = s // CMP_STRIDE
    n_cmp = n_chunk - r + 1
    c = t.reshape(b, n_chunk, CMP_STRIDE, g, d)
    blocks = jnp.concatenate([c[:, j:j + n_cmp] for j in range(r)], axis=2)
    blocks = blocks + pos_emb[None, None, :, None, :].astype(t.dtype)
    flat = blocks.transpose(0, 1, 3, 2, 4).reshape(b, n_cmp, g, CMP_LEN * d)
    return jax.nn.gelu(flat @ w1) @ w2


def nsa_attention(q, kc, vc, ks, vs, kw, vw, gates):
    b, s = q.shape[0], q.shape[1]
    n_cmp = kc.shape[1]
    n_sel_blocks = s // SEL_LEN
    n_top = min(SEL_BLOCKS, n_sel_blocks)
    scale = HEAD_DIM ** -0.5
    dt = q.dtype
    cmp_end = jnp.asarray(np.arange(n_cmp) * CMP_STRIDE + CMP_LEN - 1)
    ci = np.arange(n_cmp)[:, None] * CMP_STRIDE
    sj = np.arange(n_sel_blocks)[None, :] * SEL_LEN
    overlap = jnp.asarray(((ci < sj + SEL_LEN) & (ci + CMP_LEN > sj)).astype(np.float32))
    ks_blk = ks.reshape(b, n_sel_blocks, SEL_LEN, NSA_GROUPS, HEAD_DIM).transpose(0, 3, 1, 2, 4)
    vs_blk = vs.reshape(b, n_sel_blocks, SEL_LEN, NSA_GROUPS, HEAD_DIM).transpose(0, 3, 1, 2, 4)
    kw_pad = jnp.pad(kw, ((0, 0), (WINDOW, 0), (0, 0), (0, 0)))
    vw_pad = jnp.pad(vw, ((0, 0), (WINDOW, 0), (0, 0), (0, 0)))
    bix = jnp.arange(b)[:, None, None, None]
    gix = jnp.arange(NSA_GROUPS)[None, :, None, None]
    blk_ids = jnp.arange(n_sel_blocks)
    in_blk = jnp.arange(SEL_LEN)
    win_off = jnp.arange(WINDOW + Q_BLOCK) - WINDOW

    def block(bi):
        q0 = bi * Q_BLOCK
        t = q0 + jnp.arange(Q_BLOCK)
        qg = lax.dynamic_slice_in_dim(q, q0, Q_BLOCK, 1).reshape(b, Q_BLOCK, NSA_GROUPS, NSA_REP, HEAD_DIM)
        s_c = jnp.einsum('bqgrd,bngd->bgrqn', qg, kc).astype(jnp.float32) * scale
        p_c = masked_softmax(s_c, cmp_end[None, :] <= t[:, None])
        o_c = jnp.einsum('bgrqn,bngd->bqgrd', p_c.astype(dt), vc)
        imp = jnp.einsum('bgrqn,nj->bgqj', p_c, overlap)
        cur = (t // SEL_LEN)[:, None]
        forced = (blk_ids[None] == 0) | (blk_ids[None] == cur) | (blk_ids[None] == cur - 1)
        imp = jnp.where(forced, jnp.inf, imp)
        imp = jnp.where(blk_ids[None] * SEL_LEN <= t[:, None], imp, -jnp.inf)
        _, sel = lax.top_k(imp, n_top)
        k_g = ks_blk[bix, gix, sel].reshape(b, NSA_GROUPS, Q_BLOCK, n_top * SEL_LEN, HEAD_DIM)
        v_g = vs_blk[bix, gix, sel].reshape(b, NSA_GROUPS, Q_BLOCK, n_top * SEL_LEN, HEAD_DIM)
        key_pos = (sel[..., None] * SEL_LEN + in_blk).reshape(b, NSA_GROUPS, 1, Q_BLOCK, n_top * SEL_LEN)
        s_s = jnp.einsum('bqgrd,bgqmd->bgrqm', qg, k_g).astype(jnp.float32) * scale
        p_s = masked_softmax(s_s, key_pos <= t[:, None])
        o_s = jnp.einsum('bgrqm,bgqmd->bqgrd', p_s.astype(dt), v_g)
        k_w = lax.dynamic_slice_in_dim(kw_pad, q0, WINDOW + Q_BLOCK, 1)
        v_w = lax.dynamic_slice_in_dim(vw_pad, q0, WINDOW + Q_BLOCK, 1)
        kpos = q0 + win_off
        diff = t[:, None] - kpos[None, :]
        mask_w = (kpos[None, :] >= 0) & (diff >= 0) & (diff < WINDOW)
        s_w = jnp.einsum('bqgrd,bkgd->bgrqk', qg, k_w).astype(jnp.float32) * scale
        p_w = masked_softmax(s_w, mask_w)
        o_w = jnp.einsum('bgrqk,bkgd->bqgrd', p_w.astype(dt), v_w)
        g = lax.dynamic_slice_in_dim(gates, q0, Q_BLOCK, 1)[..., None]
        o = g[:, :, 0] * o_c + g[:, :, 1] * o_s + g[:, :, 2] * o_w
        return o.reshape(b, Q_BLOCK, MIX_A)

    out = lax.map(block, jnp.arange(s // Q_BLOCK))
    return out.transpose(1, 0, 2, 3).reshape(b, s, MIX_A)


def dsa_attention(q, k, v, iq, ik, iw):
    b, s = q.shape[0], q.shape[1]
    n_keep = min(DSA_TOPK, s // 4)
    scale = HEAD_DIM ** -0.5
    dt = q.dtype
    key_pos = jnp.arange(s)
    bix = jnp.arange(b)[:, None, None]

    def block(bi):
        q0 = bi * Q_BLOCK
        t = q0 + jnp.arange(Q_BLOCK)
        qb = lax.dynamic_slice_in_dim(q, q0, Q_BLOCK, 1)
        iqb = lax.dynamic_slice_in_dim(iq, q0, Q_BLOCK, 1)
        iwb = lax.dynamic_slice_in_dim(iw, q0, Q_BLOCK, 1).astype(jnp.float32) * IDX_HEADS ** -0.5
        logits = jnp.einsum('bqhd,bsd->bqhs', iqb, ik).astype(jnp.float32) * IDX_DIM ** -0.5
        score = jnp.einsum('bqhs,bqh->bqs', jax.nn.relu(logits), iwb)
        score = jnp.where(key_pos[None, None, :] <= t[None, :, None], score, -jnp.inf)
        _, sel = lax.top_k(score, n_keep)
        k_g = k[bix, sel]
        v_g = v[bix, sel]
        att = jnp.einsum('bqhd,bqkd->bhqk', qb, k_g).astype(jnp.float32) * scale
        p = masked_softmax(att, (sel <= t[None, :, None])[:, None])
        o = jnp.einsum('bhqk,bqkd->bqhd', p.astype(dt), v_g)
        return o.reshape(b, Q_BLOCK, MIX_B)

    out = lax.map(block, jnp.arange(s // Q_BLOCK))
    return out.transpose(1, 0, 2, 3).reshape(b, s, MIX_B)


def peer(h, wq, sub_keys, u, v):
    b, s, d = h.shape
    dt = h.dtype
    tok = h.reshape(b * s // PEER_TOKEN_BLOCK, PEER_TOKEN_BLOCK, d)

    def block(xb):
        q = (xb @ wq).reshape(PEER_TOKEN_BLOCK, PEER_HEADS, 2, PEER_QDIM // 2)
        s1 = jnp.einsum('thd,kd->thk', q[:, :, 0], sub_keys[0]).astype(jnp.float32)
        s2 = jnp.einsum('thd,kd->thk', q[:, :, 1], sub_keys[1]).astype(jnp.float32)
        v1, i1 = lax.top_k(s1, PEER_HALF_TOPK)
        v2, i2 = lax.top_k(s2, PEER_HALF_TOPK)
        cand = (v1[..., :, None] + v2[..., None, :]).reshape(PEER_TOKEN_BLOCK, PEER_HEADS, -1)
        cidx = (i1[..., :, None] * N_KEYS + i2[..., None, :]).reshape(PEER_TOKEN_BLOCK, PEER_HEADS, -1)
        top, pos = lax.top_k(cand, PEER_TOPK)
        eidx = jnp.take_along_axis(cidx, pos, axis=-1)
        gate = jax.nn.softmax(top, axis=-1)
        act = jax.nn.gelu(jnp.einsum('td,thkd->thk', xb, u[eidx]).astype(jnp.float32))
        return jnp.einsum('thk,thkd->td', (gate * act).astype(dt), v[eidx])

    return lax.map(block, tok).reshape(b, s, d)


def hybrid_layer(x, p_i, positions, attn_norm, w_in, nsa_qk_gain, cmp_pos, cmp_w1, cmp_w2,
                 dsa_qk_gain, w_branch_a, w_branch_b, w_out, ffn_norm, peer_wq, peer_sub_keys,
                 peer_u, peer_v, ple_w, ple_gate_w, ple_norm):
    b, s, _ = x.shape
    h = rms_norm(x, attn_norm)
    proj = h @ w_in
    cuts = np.cumsum(IN_SIZES)[:-1].tolist()
    nq, nkv, ngate, dq, dkv, iq, ik, iw, mg = jnp.split(proj, cuts, axis=-1)

    qa = rope(rms_norm(nq.reshape(b, s, NSA_HEADS, HEAD_DIM), nsa_qk_gain[0]), positions)
    kv = nkv.reshape(b, s, 6, NSA_GROUPS, HEAD_DIM)
    kc = compress(kv[:, :, 0], cmp_pos[0], cmp_w1[0], cmp_w2[0])
    vc = compress(kv[:, :, 1], cmp_pos[1], cmp_w1[1], cmp_w2[1])
    n_cmp = kc.shape[1]
    cmp_pos_ids = positions[:, CMP_LEN - 1::CMP_STRIDE][:, :n_cmp]
    kc = rope(rms_norm(kc, nsa_qk_gain[1]), cmp_pos_ids)
    ks = rope(rms_norm(kv[:, :, 2], nsa_qk_gain[2]), positions)
    kw = rope(rms_norm(kv[:, :, 4], nsa_qk_gain[3]), positions)
    gates = jax.nn.sigmoid(ngate).reshape(b, s, 3, NSA_GROUPS, NSA_REP)
    ya = nsa_attention(qa, kc, vc, ks, kv[:, :, 3], kw, kv[:, :, 5], gates)

    qb = rope(rms_norm(dq.reshape(b, s, DSA_HEADS, HEAD_DIM), dsa_qk_gain[0]), positions)
    dkv = dkv.reshape(b, s, 2, HEAD_DIM)
    kb = rope(rms_norm(dkv[:, :, :1], dsa_qk_gain[1]), positions)[:, :, 0]
    vb = dkv[:, :, 1]
    iq = rope(iq.reshape(b, s, IDX_HEADS, IDX_DIM), positions)
    ik = rope(ik[:, :, None], positions)[:, :, 0]
    yb = dsa_attention(qb, kb, vb, iq, ik, iw)

    ga, gb = jnp.split(jax.nn.sigmoid(mg), 2, axis=-1)
    x = x + (ga * (ya @ w_branch_a) + gb * (yb @ w_branch_b)) @ w_out

    x = x + peer(rms_norm(x, ffn_norm), peer_wq, peer_sub_keys, peer_u, peer_v)

    gate = jax.nn.sigmoid(_rms(x).astype(x.dtype) @ ple_gate_w)
    return x + gate * rms_norm(p_i @ ple_w, ple_norm)


def setup_inputs(seed: int = 0) -> dict:
    key = jax.random.key(seed)
    ks = jax.random.split(key, 24)
    f32 = jnp.float32
    L = DEPTH

    def nrm(k, shape, scale):
        return jax.random.normal(k, shape, f32) * scale

    def gain(k, shape):
        return 1.0 + 0.05 * jax.random.normal(k, shape, f32)

    offset = jax.random.randint(ks[2], (BATCH, 1), 0, 1024, dtype=jnp.int32)
    positions = offset + jnp.arange(SEQ, dtype=jnp.int32)[None, :]
    return {
        'x': nrm(ks[0], (BATCH, SEQ, D_MODEL), 1.0),
        'p': nrm(ks[1], (DEPTH, BATCH, SEQ, PLE_DIM), 1.0),
        'positions': positions,
        'attn_norm': gain(ks[3], (L, D_MODEL)),
        'w_in': nrm(ks[4], (L, D_MODEL, IN_COLS), D_MODEL ** -0.5),
        'nsa_qk_gain': gain(ks[5], (L, 4, HEAD_DIM)),
        'cmp_pos': nrm(ks[6], (L, 2, CMP_LEN, HEAD_DIM), 0.1),
        'cmp_w1': nrm(ks[7], (L, 2, CMP_LEN * HEAD_DIM, CMP_HIDDEN), (CMP_LEN * HEAD_DIM) ** -0.5),
        'cmp_w2': nrm(ks[8], (L, 2, CMP_HIDDEN, HEAD_DIM), CMP_HIDDEN ** -0.5),
        'dsa_qk_gain': gain(ks[9], (L, 2, HEAD_DIM)),
        'w_branch_a': nrm(ks[10], (L, MIX_A, D_MODEL), MIX_A ** -0.5),
        'w_branch_b': nrm(ks[11], (L, MIX_B, D_MODEL), MIX_B ** -0.5),
        'w_out': nrm(ks[12], (L, D_MODEL, D_MODEL), 0.5 * D_MODEL ** -0.5),
        'ffn_norm': gain(ks[13], (L, D_MODEL)),
        'peer_wq': nrm(ks[14], (L, D_MODEL, PEER_HEADS * PEER_QDIM), D_MODEL ** -0.5),
        'peer_sub_keys': nrm(ks[15], (L, 2, N_KEYS, PEER_QDIM // 2), (PEER_QDIM // 2) ** -0.5),
        'peer_u': nrm(ks[16], (L, N_EXPERTS, D_MODEL), D_MODEL ** -0.5),
        'peer_v': nrm(ks[17], (L, N_EXPERTS, D_MODEL), PEER_HEADS ** -0.5),
        'ple_w': nrm(ks[18], (L, PLE_DIM, D_MODEL), PLE_DIM ** -0.5),
        'ple_gate_w': nrm(ks[19], (L, D_MODEL, D_MODEL), D_MODEL ** -0.5),
        'ple_norm': gain(ks[20], (L, D_MODEL)),
    }


def reference(x, p, positions, attn_norm, w_in, nsa_qk_gain, cmp_pos, cmp_w1, cmp_w2,
              dsa_qk_gain, w_branch_a, w_branch_b, w_out, ffn_norm, peer_wq, peer_sub_keys,
              peer_u, peer_v, ple_w, ple_gate_w, ple_norm):
    for i in range(DEPTH):
        x = hybrid_layer(x, p[i], positions, attn_norm[i], w_in[i], nsa_qk_gain[i], cmp_pos[i],
                         cmp_w1[i], cmp_w2[i], dsa_qk_gain[i], w_branch_a[i], w_branch_b[i],
                         w_out[i], ffn_norm[i], peer_wq[i], peer_sub_keys[i], peer_u[i],
                         peer_v[i], ple_w[i], ple_gate_w[i], ple_norm[i])
    return x
```

```python
import functools

import numpy as np
import jax
import jax.numpy as jnp
from jax import lax
from jax.experimental import pallas as pl
from jax.experimental.pallas import tpu as pltpu

D_MODEL = 1024
BATCH = 8
SEQ = 4096
DEPTH = 2

HEAD_DIM = 64
ROT_DIM = HEAD_DIM // 4
ROPE_THETA = 500000.0
Q_BLOCK = 128
NEG = -1e30
EPS = 1e-6

NSA_HEADS = 8
NSA_GROUPS = 2
NSA_REP = NSA_HEADS // NSA_GROUPS
CMP_LEN = 32
CMP_STRIDE = 16
CMP_HIDDEN = 256
SEL_LEN = 64
SEL_BLOCKS = 16
WINDOW = 512

DSA_HEADS = 8
IDX_HEADS = 8
IDX_DIM = 64
DSA_TOPK = 256

PEER_HEADS = 8
PEER_QDIM = 256
N_KEYS = 128
N_EXPERTS = N_KEYS * N_KEYS
PEER_HALF_TOPK = 16
PEER_TOPK = 16
PEER_TOKEN_BLOCK = 128

PLE_DIM = 256

MIX_A = NSA_HEADS * HEAD_DIM
MIX_B = DSA_HEADS * HEAD_DIM
IN_SIZES = (
    MIX_A,
    6 * NSA_GROUPS * HEAD_DIM,
    3 * NSA_HEADS,
    MIX_B,
    2 * HEAD_DIM,
    IDX_HEADS * IDX_DIM,
    IDX_DIM,
    IDX_HEADS,
    2 * D_MODEL,
)
IN_COLS = sum(IN_SIZES)

LANE = 128
VMEM_LIMIT = 48 * 1024 * 1024


def _norm_matmul_kernel(x_ref, g_ref, w_ref, o_ref):
    x = x_ref[...]
    h = x * lax.rsqrt(jnp.mean(x * x, axis=-1, keepdims=True) + EPS) * g_ref[...]
    o_ref[...] = jnp.dot(h.astype(jnp.bfloat16), w_ref[...],
                         preferred_element_type=jnp.float32)


def norm_matmul(x2d, g, w_bf16, *, tm=512, tn=512):
    m, k = x2d.shape
    n = w_bf16.shape[1]
    assert m % tm == 0 and n % tn == 0
    return pl.pallas_call(
        _norm_matmul_kernel,
        grid=(m // tm, n // tn),
        in_specs=[
            pl.BlockSpec((tm, k), lambda i, j: (i, 0)),
            pl.BlockSpec((1, k), lambda i, j: (0, 0)),
            pl.BlockSpec((k, tn), lambda i, j: (0, j)),
        ],
        out_specs=pl.BlockSpec((tm, tn), lambda i, j: (i, j)),
        out_shape=jax.ShapeDtypeStruct((m, n), jnp.float32),
        compiler_params=pltpu.CompilerParams(
            dimension_semantics=("parallel", "arbitrary"),
            vmem_limit_bytes=VMEM_LIMIT),
        name="norm_matmul",
    )(x2d, g.reshape(1, k), w_bf16)


def _rms(x):
    xf = x.astype(jnp.float32)
    return xf * lax.rsqrt(jnp.mean(xf * xf, axis=-1, keepdims=True) + EPS)


def rms_norm(x, g):
    return (_rms(x) * g.astype(jnp.float32)).astype(x.dtype)


def rope(x, pos):
    half = ROT_DIM // 2
    inv = ROPE_THETA ** (-jnp.arange(half, dtype=jnp.float32) / half)
    ang = pos.astype(jnp.float32)[..., None] * inv
    cos = jnp.cos(ang)[:, :, None, :]
    sin = jnp.sin(ang)[:, :, None, :]
    xf = x.astype(jnp.float32)
    x1, x2, rest = xf[..., :half], xf[..., half:ROT_DIM], xf[..., ROT_DIM:]
    return jnp.concatenate([x1 * cos - x2 * sin, x2 * cos + x1 * sin, rest], axis=-1).astype(x.dtype)


def masked_softmax(s, mask):
    p = jax.nn.softmax(jnp.where(mask, s, NEG), axis=-1)
    return jnp.where(mask, p, 0.0)


def compress(t, pos_emb, w1, w2):
    b, s, g, d = t.shape
    r = CMP_LEN // CMP_STRIDE
    n_chunk = s // CMP_STRIDE
    n_cmp = n_chunk - r + 1
    c = t.reshape(b, n_chunk, CMP_STRIDE, g, d)
    blocks = jnp.concatenate([c[:, j:j + n_cmp] for j in range(r)], axis=2)
    blocks = blocks + pos_emb[None, None, :, None, :].astype(t.dtype)
    flat = blocks.transpose(0, 1, 3, 2, 4).reshape(b, n_cmp, g, CMP_LEN * d)
    return jax.nn.gelu(flat @ w1) @ w2


def nsa_attention(q, kc, vc, ks, vs, kw, vw, gates):
    b, s = q.shape[0], q.shape[1]
    n_cmp = kc.shape[1]
    n_sel_blocks = s // SEL_LEN
    n_top = min(SEL_BLOCKS, n_sel_blocks)
    scale = HEAD_DIM ** -0.5
    dt = q.dtype
    cmp_end = jnp.asarray(np.arange(n_cmp) * CMP_STRIDE + CMP_LEN - 1)
    ci = np.arange(n_cmp)[:, None] * CMP_STRIDE
    sj = np.arange(n_sel_blocks)[None, :] * SEL_LEN
    overlap = jnp.asarray(((ci < sj + SEL_LEN) & (ci + CMP_LEN > sj)).astype(np.float32))
    ks_blk = ks.reshape(b, n_sel_blocks, SEL_LEN, NSA_GROUPS, HEAD_DIM).transpose(0, 3, 1, 2, 4)
    vs_blk = vs.reshape(b, n_sel_blocks, SEL_LEN, NSA_GROUPS, HEAD_DIM).transpose(0, 3, 1, 2, 4)
    kw_pad = jnp.pad(kw, ((0, 0), (WINDOW, 0), (0, 0), (0, 0)))
    vw_pad = jnp.pad(vw, ((0, 0), (WINDOW, 0), (0, 0), (0, 0)))
    bix = jnp.arange(b)[:, None, None, None]
    gix = jnp.arange(NSA_GROUPS)[None, :, None, None]
    blk_ids = jnp.arange(n_sel_blocks)
    in_blk = jnp.arange(SEL_LEN)
    win_off = jnp.arange(WINDOW + Q_BLOCK) - WINDOW

    def block(bi):
        q0 = bi * Q_BLOCK
        t = q0 + jnp.arange(Q_BLOCK)
        qg = lax.dynamic_slice_in_dim(q, q0, Q_BLOCK, 1).reshape(b, Q_BLOCK, NSA_GROUPS, NSA_REP, HEAD_DIM)
        s_c = jnp.einsum('bqgrd,bngd->bgrqn', qg, kc).astype(jnp.float32) * scale
        p_c = masked_softmax(s_c, cmp_end[None, :] <= t[:, None])
        o_c = jnp.einsum('bgrqn,bngd->bqgrd', p_c.astype(dt), vc)
        imp = jnp.einsum('bgrqn,nj->bgqj', p_c, overlap)
        cur = (t // SEL_LEN)[:, None]
        forced = (blk_ids[None] == 0) | (blk_ids[None] == cur) | (blk_ids[None] == cur - 1)
        imp = jnp.where(forced, jnp.inf, imp)
        imp = jnp.where(blk_ids[None] * SEL_LEN <= t[:, None], imp, -jnp.inf)
        _, sel = lax.top_k(imp, n_top)
        k_g = ks_blk[bix, gix, sel].reshape(b, NSA_GROUPS, Q_BLOCK, n_top * SEL_LEN, HEAD_DIM)
        v_g = vs_blk[bix, gix, sel].reshape(b, NSA_GROUPS, Q_BLOCK, n_top * SEL_LEN, HEAD_DIM)
        key_pos = (sel[..., None] * SEL_LEN + in_blk).reshape(b, NSA_GROUPS, 1, Q_BLOCK, n_top * SEL_LEN)
        s_s = jnp.einsum('bqgrd,bgqmd->bgrqm', qg, k_g).astype(jnp.float32) * scale
        p_s = masked_softmax(s_s, key_pos <= t[:, None])
        o_s = jnp.einsum('bgrqm,bgqmd->bqgrd', p_s.astype(dt), v_g)
        k_w = lax.dynamic_slice_in_dim(kw_pad, q0, WINDOW + Q_BLOCK, 1)
        v_w = lax.dynamic_slice_in_dim(vw_pad, q0, WINDOW + Q_BLOCK, 1)
        kpos = q0 + win_off
        diff = t[:, None] - kpos[None, :]
        mask_w = (kpos[None, :] >= 0) & (diff >= 0) & (diff < WINDOW)
        s_w = jnp.einsum('bqgrd,bkgd->bgrqk', qg, k_w).astype(jnp.float32) * scale
        p_w = masked_softmax(s_w, mask_w)
        o_w = jnp.einsum('bgrqk,bkgd->bqgrd', p_w.astype(dt), v_w)
        g = lax.dynamic_slice_in_dim(gates, q0, Q_BLOCK, 1)[..., None]
        o = g[:, :, 0] * o_c + g[:, :, 1] * o_s + g[:, :, 2] * o_w
        return o.reshape(b, Q_BLOCK, MIX_A)

    out = lax.map(block, jnp.arange(s // Q_BLOCK))
    return out.transpose(1, 0, 2, 3).reshape(b, s, MIX_A)


def dsa_attention(q, k, v, iq, ik, iw):
    b, s = q.shape[0], q.shape[1]
    n_keep = min(DSA_TOPK, s // 4)
    scale = HEAD_DIM ** -0.5
    dt = q.dtype
    key_pos = jnp.arange(s)
    bix = jnp.arange(b)[:, None, None]

    def block(bi):
        q0 = bi * Q_BLOCK
        t = q0 + jnp.arange(Q_BLOCK)
        qb = lax.dynamic_slice_in_dim(q, q0, Q_BLOCK, 1)
        iqb = lax.dynamic_slice_in_dim(iq, q0, Q_BLOCK, 1)
        iwb = lax.dynamic_slice_in_dim(iw, q0, Q_BLOCK, 1).astype(jnp.float32) * IDX_HEADS ** -0.5
        logits = jnp.einsum('bqhd,bsd->bqhs', iqb, ik).astype(jnp.float32) * IDX_DIM ** -0.5
        score = jnp.einsum('bqhs,bqh->bqs', jax.nn.relu(logits), iwb)
        score = jnp.where(key_pos[None, None, :] <= t[None, :, None], score, -jnp.inf)
        _, sel = lax.top_k(score, n_keep)
        k_g = k[bix, sel]
        v_g = v[bix, sel]
        att = jnp.einsum('bqhd,bqkd->bhqk', qb, k_g).astype(jnp.float32) * scale
        p = masked_softmax(att, (sel <= t[None, :, None])[:, None])
        o = jnp.einsum('bhqk,bqkd->bqhd', p.astype(dt), v_g)
        return o.reshape(b, Q_BLOCK, MIX_B)

    out = lax.map(block, jnp.arange(s // Q_BLOCK))
    return out.transpose(1, 0, 2, 3).reshape(b, s, MIX_B)


def peer(h, wq, sub_keys, u, v):
    b, s, d = h.shape
    dt = h.dtype
    tok = h.reshape(b * s // PEER_TOKEN_BLOCK, PEER_TOKEN_BLOCK, d)

    def block(xb):
        q = (xb @ wq).reshape(PEER_TOKEN_BLOCK, PEER_HEADS, 2, PEER_QDIM // 2)
        s1 = jnp.einsum('thd,kd->thk', q[:, :, 0], sub_keys[0]).astype(jnp.float32)
        s2 = jnp.einsum('thd,kd->thk', q[:, :, 1], sub_keys[1]).astype(jnp.float32)
        v1, i1 = lax.top_k(s1, PEER_HALF_TOPK)
        v2, i2 = lax.top_k(s2, PEER_HALF_TOPK)
        cand = (v1[..., :, None] + v2[..., None, :]).reshape(PEER_TOKEN_BLOCK, PEER_HEADS, -1)
        cidx = (i1[..., :, None] * N_KEYS + i2[..., None, :]).reshape(PEER_TOKEN_BLOCK, PEER_HEADS, -1)
        top, pos = lax.top_k(cand, PEER_TOPK)
        eidx = jnp.take_along_axis(cidx, pos, axis=-1)
        gate = jax.nn.softmax(top, axis=-1)
        act = jax.nn.gelu(jnp.einsum('td,thkd->thk', xb, u[eidx]).astype(jnp.float32))
        return jnp.einsum('thk,thkd->td', (gate * act).astype(dt), v[eidx])

    return lax.map(block, tok).reshape(b, s, d)


def hybrid_layer(x, p_i, positions, attn_norm, w_in, nsa_qk_gain, cmp_pos, cmp_w1, cmp_w2,
                 dsa_qk_gain, w_branch_a, w_branch_b, w_out, ffn_norm, peer_wq, peer_sub_keys,
                 peer_u, peer_v, ple_w, ple_gate_w, ple_norm):
    b, s, _ = x.shape
    n_pad = (-IN_COLS) % 512
    w_in_p = jnp.pad(w_in, ((0, 0), (0, n_pad))).astype(jnp.bfloat16)
    proj = norm_matmul(x.reshape(b * s, D_MODEL), attn_norm, w_in_p)[:, :IN_COLS]
    proj = proj.reshape(b, s, IN_COLS)
    cuts = np.cumsum(IN_SIZES)[:-1].tolist()
    nq, nkv, ngate, dq, dkv, iq, ik, iw, mg = jnp.split(proj, cuts, axis=-1)

    qa = rope(rms_norm(nq.reshape(b, s, NSA_HEADS, HEAD_DIM), nsa_qk_gain[0]), positions)
    kv = nkv.reshape(b, s, 6, NSA_GROUPS, HEAD_DIM)
    kc = compress(kv[:, :, 0], cmp_pos[0], cmp_w1[0], cmp_w2[0])
    vc = compress(kv[:, :, 1], cmp_pos[1], cmp_w1[1], cmp_w2[1])
    n_cmp = kc.shape[1]
    cmp_pos_ids = positions[:, CMP_LEN - 1::CMP_STRIDE][:, :n_cmp]
    kc = rope(rms_norm(kc, nsa_qk_gain[1]), cmp_pos_ids)
    ks = rope(rms_norm(kv[:, :, 2], nsa_qk_gain[2]), positions)
    kw = rope(rms_norm(kv[:, :, 4], nsa_qk_gain[3]), positions)
    gates = jax.nn.sigmoid(ngate).reshape(b, s, 3, NSA_GROUPS, NSA_REP)
    ya = nsa_attention(qa, kc, vc, ks, kv[:, :, 3], kw, kv[:, :, 5], gates)

    qb = rope(rms_norm(dq.reshape(b, s, DSA_HEADS, HEAD_DIM), dsa_qk_gain[0]), positions)
    dkv = dkv.reshape(b, s, 2, HEAD_DIM)
    kb = rope(rms_norm(dkv[:, :, :1], dsa_qk_gain[1]), positions)[:, :, 0]
    vb = dkv[:, :, 1]
    iq = rope(iq.reshape(b, s, IDX_HEADS, IDX_DIM), positions)
    ik = rope(ik[:, :, None], positions)[:, :, 0]
    yb = dsa_attention(qb, kb, vb, iq, ik, iw)

    ga, gb = jnp.split(jax.nn.sigmoid(mg), 2, axis=-1)
    x = x + (ga * (ya @ w_branch_a) + gb * (yb @ w_branch_b)) @ w_out

    x = x + peer(rms_norm(x, ffn_norm), peer_wq, peer_sub_keys, peer_u, peer_v)

    gate = jax.nn.sigmoid(_rms(x).astype(x.dtype) @ ple_gate_w)
    return x + gate * rms_norm(p_i @ ple_w, ple_norm)


def kernel(x, p, positions, attn_norm, w_in, nsa_qk_gain, cmp_pos, cmp_w1, cmp_w2,
           dsa_qk_gain, w_branch_a, w_branch_b, w_out, ffn_norm, peer_wq, peer_sub_keys,
           peer_u, peer_v, ple_w, ple_gate_w, ple_norm):
    for i in range(DEPTH):
        x = hybrid_layer(x, p[i], positions, attn_norm[i], w_in[i], nsa_qk_gain[i], cmp_pos[i],
                         cmp_w1[i], cmp_w2[i], dsa_qk_gain[i], w_branch_a[i], w_branch_b[i],
                         w_out[i], ffn_norm[i], peer_wq[i], peer_sub_keys[i], peer_u[i],
                         peer_v[i], ple_w[i], ple_gate_w[i], ple_norm[i])
    return x
```

```python
import functools

import numpy as np
import jax
import jax.numpy as jnp
from jax import lax
from jax.experimental import pallas as pl
from jax.experimental.pallas import tpu as pltpu

D_MODEL = 1024
BATCH = 8
SEQ = 4096
DEPTH = 2

HEAD_DIM = 64
ROT_DIM = HEAD_DIM // 4
ROPE_THETA = 500000.0
Q_BLOCK = 128
NEG = -1e30
EPS = 1e-6

NSA_HEADS = 8
NSA_GROUPS = 2
NSA_REP = NSA_HEADS // NSA_GROUPS
CMP_LEN = 32
CMP_STRIDE = 16
CMP_HIDDEN = 256
SEL_LEN = 64
SEL_BLOCKS = 16
WINDOW = 512

DSA_HEADS = 8
IDX_HEADS = 8
IDX_DIM = 64
DSA_TOPK = 256

PEER_HEADS = 8
PEER_QDIM = 256
N_KEYS = 128
N_EXPERTS = N_KEYS * N_KEYS
PEER_HALF_TOPK = 16
PEER_TOPK = 16
PEER_TOKEN_BLOCK = 128

PLE_DIM = 256

MIX_A = NSA_HEADS * HEAD_DIM
MIX_B = DSA_HEADS * HEAD_DIM
IN_SIZES = (
    MIX_A,
    6 * NSA_GROUPS * HEAD_DIM,
    3 * NSA_HEADS,
    MIX_B,
    2 * HEAD_DIM,
    IDX_HEADS * IDX_DIM,
    IDX_DIM,
    IDX_HEADS,
    2 * D_MODEL,
)
IN_COLS = sum(IN_SIZES)

LANE = 128
SUBLANE = 8
VMEM_LIMIT = 48 * 1024 * 1024


def _norm_matmul_kernel(x_ref, g_ref, w_ref, o_ref):
    x = x_ref[...]
    h = x * lax.rsqrt(jnp.mean(x * x, axis=-1, keepdims=True) + EPS) * g_ref[...]
    o_ref[...] = jnp.dot(h.astype(jnp.bfloat16), w_ref[...],
                         preferred_element_type=jnp.float32)


def norm_matmul(x2d, g, w_bf16, *, tm=512, tn=512):
    m, k = x2d.shape
    n = w_bf16.shape[1]
    assert m % tm == 0 and n % tn == 0
    return pl.pallas_call(
        _norm_matmul_kernel,
        grid=(m // tm, n // tn),
        in_specs=[
            pl.BlockSpec((tm, k), lambda i, j: (i, 0)),
            pl.BlockSpec((1, k), lambda i, j: (0, 0)),
            pl.BlockSpec((k, tn), lambda i, j: (0, j)),
        ],
        out_specs=pl.BlockSpec((tm, tn), lambda i, j: (i, j)),
        out_shape=jax.ShapeDtypeStruct((m, n), jnp.float32),
        compiler_params=pltpu.CompilerParams(
            dimension_semantics=("parallel", "arbitrary"),
            vmem_limit_bytes=VMEM_LIMIT),
        name="norm_matmul",
    )(x2d, g.reshape(1, k), w_bf16)


NSA_TQ = 128
NSA_TK = 512
NSA_TW = 128
NSA_WIN_TILES = (WINDOW + NSA_TQ) // NSA_TW
NSA_BLK_PAD = LANE


def _flash_update(state, s, v, mask=None):
    m, l, acc = state
    m_new = jnp.maximum(m, jnp.max(s, axis=-1, keepdims=True))
    alpha = jnp.exp(m - m_new)
    p = jnp.exp(s - m_new)
    if mask is not None:
        p = jnp.where(mask, p, 0.0)
    l = alpha * l + jnp.sum(p, axis=-1, keepdims=True)
    acc = alpha * acc + jnp.dot(p.astype(jnp.bfloat16), v, preferred_element_type=jnp.float32)
    return m_new, l, acc


def _nsa_kernel(q_ref, kct_ref, vc_ref, kst_ref, vs_ref, kwt_ref, vw_ref, g_ref, ovt_ref, exp_ref,
                o_ref, bias_ref, *, n_blk, n_top, n_ktiles):
    f32 = jnp.float32
    bf16 = jnp.bfloat16
    qi = pl.program_id(2)
    q0 = qi * NSA_TQ
    t_col = q0 + lax.broadcasted_iota(jnp.int32, (NSA_TQ, 1), 0)
    t_row = q0 + lax.broadcasted_iota(jnp.int32, (1, NSA_TQ), 1)
    qs = [q_ref[0, :, r * HEAD_DIM:(r + 1) * HEAD_DIM] for r in range(NSA_REP)]

    n_cmp_pad = kct_ref.shape[-1]
    cmp_end = lax.broadcasted_iota(jnp.int32, (1, n_cmp_pad), 1) * CMP_STRIDE + (CMP_LEN - 1)
    mask_c = cmp_end <= t_col
    kct = kct_ref[0, 0]
    vc = vc_ref[0, 0]
    p_sum = jnp.zeros((NSA_TQ, n_cmp_pad), f32)
    o_c = []
    for r in range(NSA_REP):
        s = jnp.dot(qs[r], kct, preferred_element_type=f32)
        s = jnp.where(mask_c, s, NEG)
        m = jnp.max(s, axis=-1, keepdims=True)
        p = jnp.where(mask_c, jnp.exp(s - m), 0.0)
        l = jnp.sum(p, axis=-1, keepdims=True)
        p = p / jnp.where(l > 0.0, l, 1.0)
        p_sum = p_sum + p
        o_c.append(jnp.dot(p.astype(bf16), vc, preferred_element_type=f32))

    nt_dims = (((1,), (1,)), ((), ()))
    p_hi = p_sum.astype(bf16)
    p_lo = (p_sum - p_hi.astype(f32)).astype(bf16)
    ovt = ovt_ref[...]
    imp_t = (lax.dot_general(ovt, p_hi, nt_dims, preferred_element_type=f32)
             + lax.dot_general(ovt, p_lo, nt_dims, preferred_element_type=f32))
    blk = lax.broadcasted_iota(jnp.int32, (NSA_BLK_PAD, 1), 0)
    cur = t_row // SEL_LEN
    forced = (blk == 0) | (blk == cur) | (blk == cur - 1)
    admissible = (blk * SEL_LEN <= t_row) & (blk < n_blk)
    imp_t = jnp.where(forced, jnp.inf, imp_t)
    imp_t = jnp.where(admissible, imp_t, -jnp.inf)
    n_chunks = n_blk // SUBLANE
    chunks = [imp_t[c * SUBLANE:(c + 1) * SUBLANE, :] for c in range(n_chunks)]
    ranks = [jnp.zeros((SUBLANE, NSA_TQ), f32) for _ in range(n_chunks)]
    sub = lax.broadcasted_iota(jnp.int32, (SUBLANE, NSA_TQ), 0)
    for i in range(n_blk):
        ci, si = divmod(i, SUBLANE)
        row = jnp.broadcast_to(chunks[ci][si:si + 1, :], (SUBLANE, NSA_TQ))
        for c in range(n_chunks):
            if c > ci:
                beats = jnp.where(row >= chunks[c], 1.0, 0.0)
            elif c < ci:
                beats = jnp.where(row > chunks[c], 1.0, 0.0)
            else:
                tie = jnp.where(sub > si, 1.0, 0.0)
                beats = jnp.where(row > chunks[c], 1.0, jnp.where(row == chunks[c], tie, 0.0))
            ranks[c] = ranks[c] + beats
    rank = jnp.concatenate(
        ranks + [jnp.full((NSA_BLK_PAD - n_blk, NSA_TQ), float(NSA_BLK_PAD), f32)], axis=0)
    sel_t = jnp.where((rank < n_top) & admissible, 1.0, 0.0)
    sel_q = sel_t.T.astype(bf16)

    n_live = qi // (NSA_TK // NSA_TQ) + 1
    for j in range(n_ktiles):
        @pl.when(j < n_live)
        def _():
            hit = jnp.dot(sel_q, exp_ref[:, j * NSA_TK:(j + 1) * NSA_TK],
                          preferred_element_type=f32)
            kpos = j * NSA_TK + lax.broadcasted_iota(jnp.int32, (1, NSA_TK), 1)
            bias_ref[j] = jnp.where((hit > 0.5) & (kpos <= t_col), 0.0, NEG)

    def init_state():
        return (jnp.full((NSA_TQ, 1), NEG, f32), jnp.zeros((NSA_TQ, 1), f32),
                jnp.zeros((NSA_TQ, HEAD_DIM), f32))

    def sel_body(j, states):
        kt = kst_ref[0, 0, j]
        v = vs_ref[0, 0, j]
        bias = bias_ref[j]
        out = []
        for r in range(NSA_REP):
            s = jnp.dot(qs[r], kt, preferred_element_type=f32) + bias
            out.append(_flash_update(states[r], s, v))
        return tuple(out)

    st_s = lax.fori_loop(0, n_live, sel_body, tuple(init_state() for _ in range(NSA_REP)))

    st_w = [init_state() for _ in range(NSA_REP)]
    for w in range(NSA_WIN_TILES):
        tile = qi - (NSA_WIN_TILES - 1) + w
        tix = jnp.maximum(tile, 0)
        kt = kwt_ref[0, 0, tix]
        v = vw_ref[0, 0, tix]
        kpos = tile * NSA_TW + lax.broadcasted_iota(jnp.int32, (1, NSA_TW), 1)
        diff = t_col - kpos
        mask = (kpos >= 0) & (diff >= 0) & (diff < WINDOW)
        for r in range(NSA_REP):
            s = jnp.where(mask, jnp.dot(qs[r], kt, preferred_element_type=f32), NEG)
            st_w[r] = _flash_update(st_w[r], s, v, mask)

    g = g_ref[0, 0]
    for r in range(NSA_REP):
        o_s = st_s[r][2] / st_s[r][1]
        o_w = st_w[r][2] / st_w[r][1]
        o = (g[:, r:r + 1] * o_c[r] + g[:, NSA_REP + r:NSA_REP + r + 1] * o_s
             + g[:, 2 * NSA_REP + r:2 * NSA_REP + r + 1] * o_w)
        o_ref[0, :, r * HEAD_DIM:(r + 1) * HEAD_DIM] = o


def nsa_attention_pallas(qa, kc, vc, ks, vs, kw, vw, gates):
    b, s = qa.shape[0], qa.shape[1]
    bf16 = jnp.bfloat16
    n_cmp = kc.shape[1]
    n_cmp_pad = -(-n_cmp // LANE) * LANE
    n_blk = s // SEL_LEN
    n_top = min(SEL_BLOCKS, n_blk)
    n_kt = s // NSA_TK
    n_wt = s // NSA_TW
    assert s % NSA_TK == 0 and n_blk <= NSA_BLK_PAD
    grp = NSA_REP * HEAD_DIM

    q = (qa.reshape(b, s, MIX_A) * HEAD_DIM ** -0.5).astype(bf16)
    pad_c = ((0, 0), (0, n_cmp_pad - n_cmp), (0, 0), (0, 0))
    kct = jnp.pad(kc, pad_c).transpose(0, 2, 3, 1).astype(bf16)
    vcp = jnp.pad(vc, pad_c).transpose(0, 2, 1, 3).astype(bf16)

    def key_tiles(k, tk):
        return k.reshape(b, s // tk, tk, NSA_GROUPS, HEAD_DIM).transpose(0, 3, 1, 4, 2).astype(bf16)

    def val_tiles(v, tk):
        return v.reshape(b, s // tk, tk, NSA_GROUPS, HEAD_DIM).transpose(0, 3, 1, 2, 4).astype(bf16)

    gates_g = gates.transpose(0, 3, 1, 2, 4).reshape(b, NSA_GROUPS, s, 3 * NSA_REP)

    ci = np.arange(n_cmp_pad)[None, :] * CMP_STRIDE
    sj = np.arange(NSA_BLK_PAD)[:, None] * SEL_LEN
    ovt = ((ci < sj + SEL_LEN) & (ci + CMP_LEN > sj) & (np.arange(n_cmp_pad)[None, :] < n_cmp)
           & (np.arange(NSA_BLK_PAD)[:, None] < n_blk))
    expand = (np.arange(s)[None, :] // SEL_LEN) == np.arange(NSA_BLK_PAD)[:, None]

    per_bg = lambda bi, gi, i: (bi, gi, 0, 0)
    per_bg5 = lambda bi, gi, i: (bi, gi, 0, 0, 0)
    return pl.pallas_call(
        functools.partial(_nsa_kernel, n_blk=n_blk, n_top=n_top, n_ktiles=n_kt),
        grid=(b, NSA_GROUPS, s // NSA_TQ),
        in_specs=[
            pl.BlockSpec((1, NSA_TQ, grp), lambda bi, gi, i: (bi, i, gi)),
            pl.BlockSpec((1, 1, HEAD_DIM, n_cmp_pad), per_bg),
            pl.BlockSpec((1, 1, n_cmp_pad, HEAD_DIM), per_bg),
            pl.BlockSpec((1, 1, n_kt, HEAD_DIM, NSA_TK), per_bg5),
            pl.BlockSpec((1, 1, n_kt, NSA_TK, HEAD_DIM), per_bg5),
            pl.BlockSpec((1, 1, n_wt, HEAD_DIM, NSA_TW), per_bg5),
            pl.BlockSpec((1, 1, n_wt, NSA_TW, HEAD_DIM), per_bg5),
            pl.BlockSpec((1, 1, NSA_TQ, 3 * NSA_REP), lambda bi, gi, i: (bi, gi, i, 0)),
            pl.BlockSpec((NSA_BLK_PAD, n_cmp_pad), lambda bi, gi, i: (0, 0)),
            pl.BlockSpec((NSA_BLK_PAD, s), lambda bi, gi, i: (0, 0)),
        ],
        out_specs=pl.BlockSpec((1, NSA_TQ, grp), lambda bi, gi, i: (bi, i, gi)),
        out_shape=jax.ShapeDtypeStruct((b, s, MIX_A), jnp.float32),
        scratch_shapes=[pltpu.VMEM((n_kt, NSA_TQ, NSA_TK), jnp.float32)],
        compiler_params=pltpu.CompilerParams(
            dimension_semantics=("parallel", "parallel", "arbitrary"),
            vmem_limit_bytes=VMEM_LIMIT),
        name="nsa_attention",
    )(q, kct, vcp, key_tiles(ks, NSA_TK), val_tiles(vs, NSA_TK), key_tiles(kw, NSA_TW),
      val_tiles(vw, NSA_TW), gates_g, jnp.asarray(ovt, bf16), jnp.asarray(expand, bf16))


def _rms(x):
    xf = x.astype(jnp.float32)
    return xf * lax.rsqrt(jnp.mean(xf * xf, axis=-1, keepdims=True) + EPS)


def rms_norm(x, g):
    return (_rms(x) * g.astype(jnp.float32)).astype(x.dtype)


def rope(x, pos):
    half = ROT_DIM // 2
    inv = ROPE_THETA ** (-jnp.arange(half, dtype=jnp.float32) / half)
    ang = pos.astype(jnp.float32)[..., None] * inv
    cos = jnp.cos(ang)[:, :, None, :]
    sin = jnp.sin(ang)[:, :, None, :]
    xf = x.astype(jnp.float32)
    x1, x2, rest = xf[..., :half], xf[..., half:ROT_DIM], xf[..., ROT_DIM:]
    return jnp.concatenate([x1 * cos - x2 * sin, x2 * cos + x1 * sin, rest], axis=-1).astype(x.dtype)


def masked_softmax(s, mask):
    p = jax.nn.softmax(jnp.where(mask, s, NEG), axis=-1)
    return jnp.where(mask, p, 0.0)


def compress(t, pos_emb, w1, w2):
    b, s, g, d = t.shape
    r = CMP_LEN // CMP_STRIDE
    n_chunk = s // CMP_STRIDE
    n_cmp = n_chunk - r + 1
    c = t.reshape(b, n_chunk, CMP_STRIDE, g, d)
    blocks = jnp.concatenate([c[:, j:j + n_cmp] for j in range(r)], axis=2)
    blocks = blocks + pos_emb[None, None, :, None, :].astype(t.dtype)
    flat = blocks.transpose(0, 1, 3, 2, 4).reshape(b, n_cmp, g, CMP_LEN * d)
    return jax.nn.gelu(flat @ w1) @ w2


def nsa_attention(q, kc, vc, ks, vs, kw, vw, gates):
    b, s = q.shape[0], q.shape[1]
    n_cmp = kc.shape[1]
    n_sel_blocks = s // SEL_LEN
    n_top = min(SEL_BLOCKS, n_sel_blocks)
    scale = HEAD_DIM ** -0.5
    dt = q.dtype
    cmp_end = jnp.asarray(np.arange(n_cmp) * CMP_STRIDE + CMP_LEN - 1)
    ci = np.arange(n_cmp)[:, None] * CMP_STRIDE
    sj = np.arange(n_sel_blocks)[None, :] * SEL_LEN
    overlap = jnp.asarray(((ci < sj + SEL_LEN) & (ci + CMP_LEN > sj)).astype(np.float32))
    ks_blk = ks.reshape(b, n_sel_blocks, SEL_LEN, NSA_GROUPS, HEAD_DIM).transpose(0, 3, 1, 2, 4)
    vs_blk = vs.reshape(b, n_sel_blocks, SEL_LEN, NSA_GROUPS, HEAD_DIM).transpose(0, 3, 1, 2, 4)
    kw_pad = jnp.pad(kw, ((0, 0), (WINDOW, 0), (0, 0), (0, 0)))
    vw_pad = jnp.pad(vw, ((0, 0), (WINDOW, 0), (0, 0), (0, 0)))
    bix = jnp.arange(b)[:, None, None, None]
    gix = jnp.arange(NSA_GROUPS)[None, :, None, None]
    blk_ids = jnp.arange(n_sel_blocks)
    in_blk = jnp.arange(SEL_LEN)
    win_off = jnp.arange(WINDOW + Q_BLOCK) - WINDOW

    def block(bi):
        q0 = bi * Q_BLOCK
        t = q0 + jnp.arange(Q_BLOCK)
        qg = lax.dynamic_slice_in_dim(q, q0, Q_BLOCK, 1).reshape(b, Q_BLOCK, NSA_GROUPS, NSA_REP, HEAD_DIM)
        s_c = jnp.einsum('bqgrd,bngd->bgrqn', qg, kc).astype(jnp.float32) * scale
        p_c = masked_softmax(s_c, cmp_end[None, :] <= t[:, None])
        o_c = jnp.einsum('bgrqn,bngd->bqgrd', p_c.astype(dt), vc)
        imp = jnp.einsum('bgrqn,nj->bgqj', p_c, overlap)
        cur = (t // SEL_LEN)[:, None]
        forced = (blk_ids[None] == 0) | (blk_ids[None] == cur) | (blk_ids[None] == cur - 1)
        imp = jnp.where(forced, jnp.inf, imp)
        imp = jnp.where(blk_ids[None] * SEL_LEN <= t[:, None], imp, -jnp.inf)
        _, sel = lax.top_k(imp, n_top)
        k_g = ks_blk[bix, gix, sel].reshape(b, NSA_GROUPS, Q_BLOCK, n_top * SEL_LEN, HEAD_DIM)
        v_g = vs_blk[bix, gix, sel].reshape(b, NSA_GROUPS, Q_BLOCK, n_top * SEL_LEN, HEAD_DIM)
        key_pos = (sel[..., None] * SEL_LEN + in_blk).reshape(b, NSA_GROUPS, 1, Q_BLOCK, n_top * SEL_LEN)
        s_s = jnp.einsum('bqgrd,bgqmd->bgrqm', qg, k_g).astype(jnp.float32) * scale
        p_s = masked_softmax(s_s, key_pos <= t[:, None])
        o_s = jnp.einsum('bgrqm,bgqmd->bqgrd', p_s.astype(dt), v_g)
        k_w = lax.dynamic_slice_in_dim(kw_pad, q0, WINDOW + Q_BLOCK, 1)
        v_w = lax.dynamic_slice_in_dim(vw_pad, q0, WINDOW + Q_BLOCK, 1)
        kpos = q0 + win_off
        diff = t[:, None] - kpos[None, :]
        mask_w = (kpos[None, :] >= 0) & (diff >= 0) & (diff < WINDOW)
        s_w = jnp.einsum('bqgrd,bkgd->bgrqk', qg, k_w).astype(jnp.float32) * scale
        p_w = masked_softmax(s_w, mask_w)
        o_w = jnp.einsum('bgrqk,bkgd->bqgrd', p_w.astype(dt), v_w)
        g = lax.dynamic_slice_in_dim(gates, q0, Q_BLOCK, 1)[..., None]
        o = g[:, :, 0] * o_c + g[:, :, 1] * o_s + g[:, :, 2] * o_w
        return o.reshape(b, Q_BLOCK, MIX_A)

    out = lax.map(block, jnp.arange(s // Q_BLOCK))
    return out.transpose(1, 0, 2, 3).reshape(b, s, MIX_A)


def dsa_attention(q, k, v, iq, ik, iw):
    b, s = q.shape[0], q.shape[1]
    n_keep = min(DSA_TOPK, s // 4)
    scale = HEAD_DIM ** -0.5
    dt = q.dtype
    key_pos = jnp.arange(s)
    bix = jnp.arange(b)[:, None, None]

    def block(bi):
        q0 = bi * Q_BLOCK
        t = q0 + jnp.arange(Q_BLOCK)
        qb = lax.dynamic_slice_in_dim(q, q0, Q_BLOCK, 1)
        iqb = lax.dynamic_slice_in_dim(iq, q0, Q_BLOCK, 1)
        iwb = lax.dynamic_slice_in_dim(iw, q0, Q_BLOCK, 1).astype(jnp.float32) * IDX_HEADS ** -0.5
        logits = jnp.einsum('bqhd,bsd->bqhs', iqb, ik).astype(jnp.float32) * IDX_DIM ** -0.5
        score = jnp.einsum('bqhs,bqh->bqs', jax.nn.relu(logits), iwb)
        score = jnp.where(key_pos[None, None, :] <= t[None, :, None], score, -jnp.inf)
        _, sel = lax.top_k(score, n_keep)
        k_g = k[bix, sel]
        v_g = v[bix, sel]
        att = jnp.einsum('bqhd,bqkd->bhqk', qb, k_g).astype(jnp.float32) * scale
        p = masked_softmax(att, (sel <= t[None, :, None])[:, None])
        o = jnp.einsum('bhqk,bqkd->bqhd', p.astype(dt), v_g)
        return o.reshape(b, Q_BLOCK, MIX_B)

    out = lax.map(block, jnp.arange(s // Q_BLOCK))
    return out.transpose(1, 0, 2, 3).reshape(b, s, MIX_B)


def peer(h, wq, sub_keys, u, v):
    b, s, d = h.shape
    dt = h.dtype
    tok = h.reshape(b * s // PEER_TOKEN_BLOCK, PEER_TOKEN_BLOCK, d)

    def block(xb):
        q = (xb @ wq).reshape(PEER_TOKEN_BLOCK, PEER_HEADS, 2, PEER_QDIM // 2)
        s1 = jnp.einsum('thd,kd->thk', q[:, :, 0], sub_keys[0]).astype(jnp.float32)
        s2 = jnp.einsum('thd,kd->thk', q[:, :, 1], sub_keys[1]).astype(jnp.float32)
        v1, i1 = lax.top_k(s1, PEER_HALF_TOPK)
        v2, i2 = lax.top_k(s2, PEER_HALF_TOPK)
        cand = (v1[..., :, None] + v2[..., None, :]).reshape(PEER_TOKEN_BLOCK, PEER_HEADS, -1)
        cidx = (i1[..., :, None] * N_KEYS + i2[..., None, :]).reshape(PEER_TOKEN_BLOCK, PEER_HEADS, -1)
        top, pos = lax.top_k(cand, PEER_TOPK)
        eidx = jnp.take_along_axis(cidx, pos, axis=-1)
        gate = jax.nn.softmax(top, axis=-1)
        act = jax.nn.gelu(jnp.einsum('td,thkd->thk', xb, u[eidx]).astype(jnp.float32))
        return jnp.einsum('thk,thkd->td', (gate * act).astype(dt), v[eidx])

    return lax.map(block, tok).reshape(b, s, d)


def hybrid_layer(x, p_i, positions, attn_norm, w_in, nsa_qk_gain, cmp_pos, cmp_w1, cmp_w2,
                 dsa_qk_gain, w_branch_a, w_branch_b, w_out, ffn_norm, peer_wq, peer_sub_keys,
                 peer_u, peer_v, ple_w, ple_gate_w, ple_norm):
    b, s, _ = x.shape
    n_pad = (-IN_COLS) % 512
    w_in_p = jnp.pad(w_in, ((0, 0), (0, n_pad))).astype(jnp.bfloat16)
    proj = norm_matmul(x.reshape(b * s, D_MODEL), attn_norm, w_in_p)[:, :IN_COLS]
    proj = proj.reshape(b, s, IN_COLS)
    cuts = np.cumsum(IN_SIZES)[:-1].tolist()
    nq, nkv, ngate, dq, dkv, iq, ik, iw, mg = jnp.split(proj, cuts, axis=-1)

    qa = rope(rms_norm(nq.reshape(b, s, NSA_HEADS, HEAD_DIM), nsa_qk_gain[0]), positions)
    kv = nkv.reshape(b, s, 6, NSA_GROUPS, HEAD_DIM)
    kc = compress(kv[:, :, 0], cmp_pos[0], cmp_w1[0], cmp_w2[0])
    vc = compress(kv[:, :, 1], cmp_pos[1], cmp_w1[1], cmp_w2[1])
    n_cmp = kc.shape[1]
    cmp_pos_ids = positions[:, CMP_LEN - 1::CMP_STRIDE][:, :n_cmp]
    kc = rope(rms_norm(kc, nsa_qk_gain[1]), cmp_pos_ids)
    ks = rope(rms_norm(kv[:, :, 2], nsa_qk_gain[2]), positions)
    kw = rope(rms_norm(kv[:, :, 4], nsa_qk_gain[3]), positions)
    gates = jax.nn.sigmoid(ngate).reshape(b, s, 3, NSA_GROUPS, NSA_REP)
    ya = nsa_attention_pallas(qa, kc, vc, ks, kv[:, :, 3], kw, kv[:, :, 5], gates)

    qb = rope(rms_norm(dq.reshape(b, s, DSA_HEADS, HEAD_DIM), dsa_qk_gain[0]), positions)
    dkv = dkv.reshape(b, s, 2, HEAD_DIM)
    kb = rope(rms_norm(dkv[:, :, :1], dsa_qk_gain[1]), positions)[:, :, 0]
    vb = dkv[:, :, 1]
    iq = rope(iq.reshape(b, s, IDX_HEADS, IDX_DIM), positions)
    ik = rope(ik[:, :, None], positions)[:, :, 0]
    yb = dsa_attention(qb, kb, vb, iq, ik, iw)

    ga, gb = jnp.split(jax.nn.sigmoid(mg), 2, axis=-1)
    x = x + (ga * (ya @ w_branch_a) + gb * (yb @ w_branch_b)) @ w_out

    x = x + peer(rms_norm(x, ffn_norm), peer_wq, peer_sub_keys, peer_u, peer_v)

    gate = jax.nn.sigmoid(_rms(x).astype(x.dtype) @ ple_gate_w)
    return x + gate * rms_norm(p_i @ ple_w, ple_norm)


def kernel(x, p, positions, attn_norm, w_in, nsa_qk_gain, cmp_pos, cmp_w1, cmp_w2,
           dsa_qk_gain, w_branch_a, w_branch_b, w_out, ffn_norm, peer_wq, peer_sub_keys,
           peer_u, peer_v, ple_w, ple_gate_w, ple_norm):
    for i in range(DEPTH):
        x = hybrid_layer(x, p[i], positions, attn_norm[i], w_in[i], nsa_qk_gain[i], cmp_pos[i],
                         cmp_w1[i], cmp_w2[i], dsa_qk_gain[i], w_branch_a[i], w_branch_b[i],
                         w_out[i], ffn_norm[i], peer_wq[i], peer_sub_keys[i], peer_u[i],
                         peer_v[i], ple_w[i], ple_gate_w[i], ple_norm[i])
    return x
```

```python
import functools

import numpy as np
import jax
import jax.numpy as jnp
from jax import lax
from jax.experimental import pallas as pl
from jax.experimental.pallas import tpu as pltpu

D_MODEL = 1024
BATCH = 8
SEQ = 4096
DEPTH = 2

HEAD_DIM = 64
ROT_DIM = HEAD_DIM // 4
ROPE_THETA = 500000.0
Q_BLOCK = 128
NEG = -1e30
EPS = 1e-6

NSA_HEADS = 8
NSA_GROUPS = 2
NSA_REP = NSA_HEADS // NSA_GROUPS
CMP_LEN = 32
CMP_STRIDE = 16
CMP_HIDDEN = 256
SEL_LEN = 64
SEL_BLOCKS = 16
WINDOW = 512

DSA_HEADS = 8
IDX_HEADS = 8
IDX_DIM = 64
DSA_TOPK = 256

PEER_HEADS = 8
PEER_QDIM = 256
N_KEYS = 128
N_EXPERTS = N_KEYS * N_KEYS
PEER_HALF_TOPK = 16
PEER_TOPK = 16
PEER_TOKEN_BLOCK = 128

PLE_DIM = 256

MIX_A = NSA_HEADS * HEAD_DIM
MIX_B = DSA_HEADS * HEAD_DIM
IN_SIZES = (
    MIX_A,
    6 * NSA_GROUPS * HEAD_DIM,
    3 * NSA_HEADS,
    MIX_B,
    2 * HEAD_DIM,
    IDX_HEADS * IDX_DIM,
    IDX_DIM,
    IDX_HEADS,
    2 * D_MODEL,
)
IN_COLS = sum(IN_SIZES)

LANE = 128
SUBLANE = 8
VMEM_LIMIT = 48 * 1024 * 1024


def _norm_matmul_kernel(x_ref, g_ref, w_ref, o_ref):
    x = x_ref[...]
    h = x * lax.rsqrt(jnp.mean(x * x, axis=-1, keepdims=True) + EPS) * g_ref[...]
    o_ref[...] = jnp.dot(h.astype(jnp.bfloat16), w_ref[...],
                         preferred_element_type=jnp.float32)


def norm_matmul(x2d, g, w_bf16, *, tm=512, tn=512):
    m, k = x2d.shape
    n = w_bf16.shape[1]
    assert m % tm == 0 and n % tn == 0
    return pl.pallas_call(
        _norm_matmul_kernel,
        grid=(m // tm, n // tn),
        in_specs=[
            pl.BlockSpec((tm, k), lambda i, j: (i, 0)),
            pl.BlockSpec((1, k), lambda i, j: (0, 0)),
            pl.BlockSpec((k, tn), lambda i, j: (0, j)),
        ],
        out_specs=pl.BlockSpec((tm, tn), lambda i, j: (i, j)),
        out_shape=jax.ShapeDtypeStruct((m, n), jnp.float32),
        compiler_params=pltpu.CompilerParams(
            dimension_semantics=("parallel", "arbitrary"),
            vmem_limit_bytes=VMEM_LIMIT),
        name="norm_matmul",
    )(x2d, g.reshape(1, k), w_bf16)


NSA_TQ = 128
NSA_TK = 512
NSA_TW = 128
NSA_WIN_TILES = (WINDOW + NSA_TQ) // NSA_TW
NSA_BLK_PAD = LANE


def _flash_update(state, s, v, mask=None):
    m, l, acc = state
    m_new = jnp.maximum(m, jnp.max(s, axis=-1, keepdims=True))
    alpha = jnp.exp(m - m_new)
    p = jnp.exp(s - m_new)
    if mask is not None:
        p = jnp.where(mask, p, 0.0)
    l = alpha * l + jnp.sum(p, axis=-1, keepdims=True)
    acc = alpha * acc + jnp.dot(p.astype(jnp.bfloat16), v, preferred_element_type=jnp.float32)
    return m_new, l, acc


def _nsa_kernel(q_ref, kct_ref, vc_ref, kst_ref, vs_ref, kwt_ref, vw_ref, g_ref, ovt_ref, exp_ref,
                o_ref, bias_ref, *, n_blk, n_top, n_ktiles):
    f32 = jnp.float32
    bf16 = jnp.bfloat16
    qi = pl.program_id(2)
    q0 = qi * NSA_TQ
    t_col = q0 + lax.broadcasted_iota(jnp.int32, (NSA_TQ, 1), 0)
    t_row = q0 + lax.broadcasted_iota(jnp.int32, (1, NSA_TQ), 1)
    qs = [q_ref[0, :, r * HEAD_DIM:(r + 1) * HEAD_DIM] for r in range(NSA_REP)]

    n_cmp_pad = kct_ref.shape[-1]
    cmp_end = lax.broadcasted_iota(jnp.int32, (1, n_cmp_pad), 1) * CMP_STRIDE + (CMP_LEN - 1)
    mask_c = cmp_end <= t_col
    kct = kct_ref[0, 0]
    vc = vc_ref[0, 0]
    p_sum = jnp.zeros((NSA_TQ, n_cmp_pad), f32)
    o_c = []
    for r in range(NSA_REP):
        s = jnp.dot(qs[r], kct, preferred_element_type=f32)
        s = jnp.where(mask_c, s, NEG)
        m = jnp.max(s, axis=-1, keepdims=True)
        p = jnp.where(mask_c, jnp.exp(s - m), 0.0)
        l = jnp.sum(p, axis=-1, keepdims=True)
        p = p / jnp.where(l > 0.0, l, 1.0)
        p_sum = p_sum + p
        o_c.append(jnp.dot(p.astype(bf16), vc, preferred_element_type=f32))

    nt_dims = (((1,), (1,)), ((), ()))
    p_hi = p_sum.astype(bf16)
    p_lo = (p_sum - p_hi.astype(f32)).astype(bf16)
    ovt = ovt_ref[...]
    imp_t = (lax.dot_general(ovt, p_hi, nt_dims, preferred_element_type=f32)
             + lax.dot_general(ovt, p_lo, nt_dims, preferred_element_type=f32))
    blk = lax.broadcasted_iota(jnp.int32, (NSA_BLK_PAD, 1), 0)
    cur = t_row // SEL_LEN
    forced = (blk == 0) | (blk == cur) | (blk == cur - 1)
    admissible = (blk * SEL_LEN <= t_row) & (blk < n_blk)
    imp_t = jnp.where(forced, jnp.inf, imp_t)
    imp_t = jnp.where(admissible, imp_t, -jnp.inf)
    n_chunks = n_blk // SUBLANE
    chunks = [imp_t[c * SUBLANE:(c + 1) * SUBLANE, :] for c in range(n_chunks)]
    ranks = [jnp.zeros((SUBLANE, NSA_TQ), f32) for _ in range(n_chunks)]
    sub = lax.broadcasted_iota(jnp.int32, (SUBLANE, NSA_TQ), 0)
    for i in range(n_blk):
        ci, si = divmod(i, SUBLANE)
        row = jnp.broadcast_to(chunks[ci][si:si + 1, :], (SUBLANE, NSA_TQ))
        for c in range(n_chunks):
            if c > ci:
                beats = jnp.where(row >= chunks[c], 1.0, 0.0)
            elif c < ci:
                beats = jnp.where(row > chunks[c], 1.0, 0.0)
            else:
                tie = jnp.where(sub > si, 1.0, 0.0)
                beats = jnp.where(row > chunks[c], 1.0, jnp.where(row == chunks[c], tie, 0.0))
            ranks[c] = ranks[c] + beats
    rank = jnp.concatenate(
        ranks + [jnp.full((NSA_BLK_PAD - n_blk, NSA_TQ), float(NSA_BLK_PAD), f32)], axis=0)
    sel_t = jnp.where((rank < n_top) & admissible, 1.0, 0.0)
    sel_q = sel_t.T.astype(bf16)

    n_live = qi // (NSA_TK // NSA_TQ) + 1
    for j in range(n_ktiles):
        @pl.when(j < n_live)
        def _():
            hit = jnp.dot(sel_q, exp_ref[:, j * NSA_TK:(j + 1) * NSA_TK],
                          preferred_element_type=f32)
            kpos = j * NSA_TK + lax.broadcasted_iota(jnp.int32, (1, NSA_TK), 1)
            bias_ref[j] = jnp.where((hit > 0.5) & (kpos <= t_col), 0.0, NEG)

    def init_state():
        return (jnp.full((NSA_TQ, 1), NEG, f32), jnp.zeros((NSA_TQ, 1), f32),
                jnp.zeros((NSA_TQ, HEAD_DIM), f32))

    def sel_body(j, states):
        kt = kst_ref[0, 0, j]
        v = vs_ref[0, 0, j]
        bias = bias_ref[j]
        out = []
        for r in range(NSA_REP):
            s = jnp.dot(qs[r], kt, preferred_element_type=f32) + bias
            out.append(_flash_update(states[r], s, v))
        return tuple(out)

    st_s = lax.fori_loop(0, n_live, sel_body, tuple(init_state() for _ in range(NSA_REP)))

    st_w = [init_state() for _ in range(NSA_REP)]
    for w in range(NSA_WIN_TILES):
        tile = qi - (NSA_WIN_TILES - 1) + w
        tix = jnp.maximum(tile, 0)
        kt = kwt_ref[0, 0, tix]
        v = vw_ref[0, 0, tix]
        kpos = tile * NSA_TW + lax.broadcasted_iota(jnp.int32, (1, NSA_TW), 1)
        diff = t_col - kpos
        mask = (kpos >= 0) & (diff >= 0) & (diff < WINDOW)
        for r in range(NSA_REP):
            s = jnp.where(mask, jnp.dot(qs[r], kt, preferred_element_type=f32), NEG)
            st_w[r] = _flash_update(st_w[r], s, v, mask)

    g = g_ref[0, 0]
    for r in range(NSA_REP):
        o_s = st_s[r][2] / st_s[r][1]
        o_w = st_w[r][2] / st_w[r][1]
        o = (g[:, r:r + 1] * o_c[r] + g[:, NSA_REP + r:NSA_REP + r + 1] * o_s
             + g[:, 2 * NSA_REP + r:2 * NSA_REP + r + 1] * o_w)
        o_ref[0, :, r * HEAD_DIM:(r + 1) * HEAD_DIM] = o


def nsa_attention_pallas(qa, kc, vc, ks, vs, kw, vw, gates):
    b, s = qa.shape[0], qa.shape[1]
    bf16 = jnp.bfloat16
    n_cmp = kc.shape[1]
    n_cmp_pad = -(-n_cmp // LANE) * LANE
    n_blk = s // SEL_LEN
    n_top = min(SEL_BLOCKS, n_blk)
    n_kt = s // NSA_TK
    n_wt = s // NSA_TW
    assert s % NSA_TK == 0 and n_blk <= NSA_BLK_PAD
    grp = NSA_REP * HEAD_DIM

    q = (qa.reshape(b, s, MIX_A) * HEAD_DIM ** -0.5).astype(bf16)
    pad_c = ((0, 0), (0, n_cmp_pad - n_cmp), (0, 0), (0, 0))
    kct = jnp.pad(kc, pad_c).transpose(0, 2, 3, 1).astype(bf16)
    vcp = jnp.pad(vc, pad_c).transpose(0, 2, 1, 3).astype(bf16)

    def key_tiles(k, tk):
        return k.reshape(b, s // tk, tk, NSA_GROUPS, HEAD_DIM).transpose(0, 3, 1, 4, 2).astype(bf16)

    def val_tiles(v, tk):
        return v.reshape(b, s // tk, tk, NSA_GROUPS, HEAD_DIM).transpose(0, 3, 1, 2, 4).astype(bf16)

    gates_g = gates.transpose(0, 3, 1, 2, 4).reshape(b, NSA_GROUPS, s, 3 * NSA_REP)

    ci = np.arange(n_cmp_pad)[None, :] * CMP_STRIDE
    sj = np.arange(NSA_BLK_PAD)[:, None] * SEL_LEN
    ovt = ((ci < sj + SEL_LEN) & (ci + CMP_LEN > sj) & (np.arange(n_cmp_pad)[None, :] < n_cmp)
           & (np.arange(NSA_BLK_PAD)[:, None] < n_blk))
    expand = (np.arange(s)[None, :] // SEL_LEN) == np.arange(NSA_BLK_PAD)[:, None]

    per_bg = lambda bi, gi, i: (bi, gi, 0, 0)
    per_bg5 = lambda bi, gi, i: (bi, gi, 0, 0, 0)
    return pl.pallas_call(
        functools.partial(_nsa_kernel, n_blk=n_blk, n_top=n_top, n_ktiles=n_kt),
        grid=(b, NSA_GROUPS, s // NSA_TQ),
        in_specs=[
            pl.BlockSpec((1, NSA_TQ, grp), lambda bi, gi, i: (bi, i, gi)),
            pl.BlockSpec((1, 1, HEAD_DIM, n_cmp_pad), per_bg),
            pl.BlockSpec((1, 1, n_cmp_pad, HEAD_DIM), per_bg),
            pl.BlockSpec((1, 1, n_kt, HEAD_DIM, NSA_TK), per_bg5),
            pl.BlockSpec((1, 1, n_kt, NSA_TK, HEAD_DIM), per_bg5),
            pl.BlockSpec((1, 1, n_wt, HEAD_DIM, NSA_TW), per_bg5),
            pl.BlockSpec((1, 1, n_wt, NSA_TW, HEAD_DIM), per_bg5),
            pl.BlockSpec((1, 1, NSA_TQ, 3 * NSA_REP), lambda bi, gi, i: (bi, gi, i, 0)),
            pl.BlockSpec((NSA_BLK_PAD, n_cmp_pad), lambda bi, gi, i: (0, 0)),
            pl.BlockSpec((NSA_BLK_PAD, s), lambda bi, gi, i: (0, 0)),
        ],
        out_specs=pl.BlockSpec((1, NSA_TQ, grp), lambda bi, gi, i: (bi, i, gi)),
        out_shape=jax.ShapeDtypeStruct((b, s, MIX_A), jnp.float32),
        scratch_shapes=[pltpu.VMEM((n_kt, NSA_TQ, NSA_TK), jnp.float32)],
        compiler_params=pltpu.CompilerParams(
            dimension_semantics=("parallel", "parallel", "arbitrary"),
            vmem_limit_bytes=VMEM_LIMIT),
        name="nsa_attention",
    )(q, kct, vcp, key_tiles(ks, NSA_TK), val_tiles(vs, NSA_TK), key_tiles(kw, NSA_TW),
      val_tiles(vw, NSA_TW), gates_g, jnp.asarray(ovt, bf16), jnp.asarray(expand, bf16))


DSA_TQ = 128
DSA_TK = 512
F32_ORDER_MASK = 0x7FFFFFFF
F32_NEG_INF_BITS = -8388608
F32_POS_INF_BITS = 0x7F800000


def _ordered_from_bits(bits):
    return jnp.where(bits >= 0, bits, bits ^ F32_ORDER_MASK)


def _lane_tile_sum(x):
    acc = x[:, :LANE]
    for c in range(1, x.shape[1] // LANE):
        acc = acc + x[:, c * LANE:(c + 1) * LANE]
    return acc


def _dsa_kernel(iq_ref, w_ref, ikt_ref, q_ref, kt_ref, v_ref, tri_ref, o_ref, sc_ref, *, n_keep):
    f32 = jnp.float32
    bf16 = jnp.bfloat16
    qi = pl.program_id(1)
    q0 = qi * DSA_TQ
    t_col = q0 + lax.broadcasted_iota(jnp.int32, (DSA_TQ, 1), 0)
    n_live = qi // (DSA_TK // DSA_TQ) + 1
    lane_pos = lax.broadcasted_iota(jnp.int32, (1, DSA_TK), 1)

    iqs = [iq_ref[0, :, h * IDX_DIM:(h + 1) * IDX_DIM] for h in range(IDX_HEADS)]
    w = w_ref[0]

    def score_body(j, carry):
        ikt = ikt_ref[0, j]
        acc = jnp.zeros((DSA_TQ, DSA_TK), f32)
        for h in range(IDX_HEADS):
            logit = jnp.dot(iqs[h], ikt, preferred_element_type=f32)
            acc = acc + jnp.maximum(logit, 0.0) * w[:, h:h + 1]
        causal = (j * DSA_TK + lane_pos) <= t_col
        sc_ref[j] = jnp.where(causal, acc + 0.0, -jnp.inf)
        return carry

    lax.fori_loop(0, n_live, score_body, 0)

    def count_ge(thr):
        def body(j, acc):
            return acc + _lane_tile_sum(jnp.where(sc_ref[j] >= thr, 1.0, 0.0))
        part = lax.fori_loop(0, n_live, body, jnp.zeros((DSA_TQ, LANE), f32))
        return jnp.sum(part, axis=-1, keepdims=True)

    def bisect_body(_, lohi):
        lo, hi = lohi
        mid = (lo >> 1) + (hi >> 1) + (lo & hi & 1)
        thr = lax.bitcast_convert_type(_ordered_from_bits(mid), f32)
        ok = count_ge(thr) >= n_keep
        return jnp.where(ok, mid, lo), jnp.where(ok, hi, mid)

    lo0 = jnp.full((DSA_TQ, 1), F32_NEG_INF_BITS ^ F32_ORDER_MASK, jnp.int32)
    hi0 = jnp.full((DSA_TQ, 1), F32_POS_INF_BITS + 1, jnp.int32)
    lo, _ = lax.fori_loop(0, 32, bisect_body, (lo0, hi0))
    thr = lax.bitcast_convert_type(_ordered_from_bits(lo), f32)

    def count_gt_body(j, acc):
        return acc + _lane_tile_sum(jnp.where(sc_ref[j] > thr, 1.0, 0.0))
    n_gt = jnp.sum(lax.fori_loop(0, n_live, count_gt_body, jnp.zeros((DSA_TQ, LANE), f32)),
                   axis=-1, keepdims=True)
    need = n_keep - n_gt

    tri = tri_ref[...]

    def mask_body(j, ties_before):
        sc = sc_ref[j]
        eq = sc == thr
        pref = ties_before + jnp.dot(jnp.where(eq, 1.0, 0.0).astype(bf16), tri,
                                     preferred_element_type=f32)
        keep_tie = jnp.where(pref <= need, 0.0, NEG)
        bias = jnp.where(sc > thr, 0.0, jnp.where(eq, keep_tie, NEG))
        causal = (j * DSA_TK + lane_pos) <= t_col
        sc_ref[j] = jnp.where(causal, bias, NEG)
        return pref[:, DSA_TK - 1:DSA_TK]

    lax.fori_loop(0, n_live, mask_body, jnp.zeros((DSA_TQ, 1), f32))

    qs = [q_ref[0, :, h * HEAD_DIM:(h + 1) * HEAD_DIM] for h in range(DSA_HEADS)]

    def att_body(j, states):
        kt = kt_ref[0, j]
        v = v_ref[0, j]
        bias = sc_ref[j]
        out = []
        for h in range(DSA_HEADS):
            s = jnp.dot(qs[h], kt, preferred_element_type=f32) + bias
            out.append(_flash_update(states[h], s, v))
        return tuple(out)

    init = tuple((jnp.full((DSA_TQ, 1), NEG, f32), jnp.zeros((DSA_TQ, 1), f32),
                  jnp.zeros((DSA_TQ, HEAD_DIM), f32)) for _ in range(DSA_HEADS))
    st = lax.fori_loop(0, n_live, att_body, init)
    for h in range(DSA_HEADS):
        o_ref[0, :, h * HEAD_DIM:(h + 1) * HEAD_DIM] = st[h][2] / st[h][1]


def dsa_attention_pallas(q, k, v, iq, ik, iw):
    b, s = q.shape[0], q.shape[1]
    bf16 = jnp.bfloat16
    n_keep = min(DSA_TOPK, s // 4)
    n_kt = s // DSA_TK
    assert s % DSA_TK == 0

    def key_tiles(x):
        return x.reshape(b, n_kt, DSA_TK, x.shape[-1]).transpose(0, 1, 3, 2).astype(bf16)

    iq2 = (iq.reshape(b, s, IDX_HEADS * IDX_DIM) * IDX_DIM ** -0.5).astype(bf16)
    q2 = (q.reshape(b, s, MIX_B) * HEAD_DIM ** -0.5).astype(bf16)
    w = iw.astype(jnp.float32) * IDX_HEADS ** -0.5
    v_t = v.reshape(b, n_kt, DSA_TK, HEAD_DIM).astype(bf16)
    tri = np.arange(DSA_TK)[:, None] <= np.arange(DSA_TK)[None, :]

    per_b = lambda bi, i: (bi, 0, 0, 0)
    return pl.pallas_call(
        functools.partial(_dsa_kernel, n_keep=n_keep),
        grid=(b, s // DSA_TQ),
        in_specs=[
            pl.BlockSpec((1, DSA_TQ, IDX_HEADS * IDX_DIM), lambda bi, i: (bi, i, 0)),
            pl.BlockSpec((1, DSA_TQ, IDX_HEADS), lambda bi, i: (bi, i, 0)),
            pl.BlockSpec((1, n_kt, IDX_DIM, DSA_TK), per_b),
            pl.BlockSpec((1, DSA_TQ, MIX_B), lambda bi, i: (bi, i, 0)),
            pl.BlockSpec((1, n_kt, HEAD_DIM, DSA_TK), per_b),
            pl.BlockSpec((1, n_kt, DSA_TK, HEAD_DIM), per_b),
            pl.BlockSpec((DSA_TK, DSA_TK), lambda bi, i: (0, 0)),
        ],
        out_specs=pl.BlockSpec((1, DSA_TQ, MIX_B), lambda bi, i: (bi, i, 0)),
        out_shape=jax.ShapeDtypeStruct((b, s, MIX_B), jnp.float32),
        scratch_shapes=[pltpu.VMEM((n_kt, DSA_TQ, DSA_TK), jnp.float32)],
        compiler_params=pltpu.CompilerParams(
            dimension_semantics=("parallel", "arbitrary"),
            vmem_limit_bytes=VMEM_LIMIT),
        name="dsa_attention",
    )(iq2, w, key_tiles(ik), q2, key_tiles(k), v_t, jnp.asarray(tri, bf16))


def _rms(x):
    xf = x.astype(jnp.float32)
    return xf * lax.rsqrt(jnp.mean(xf * xf, axis=-1, keepdims=True) + EPS)


def rms_norm(x, g):
    return (_rms(x) * g.astype(jnp.float32)).astype(x.dtype)


def rope(x, pos):
    half = ROT_DIM // 2
    inv = ROPE_THETA ** (-jnp.arange(half, dtype=jnp.float32) / half)
    ang = pos.astype(jnp.float32)[..., None] * inv
    cos = jnp.cos(ang)[:, :, None, :]
    sin = jnp.sin(ang)[:, :, None, :]
    xf = x.astype(jnp.float32)
    x1, x2, rest = xf[..., :half], xf[..., half:ROT_DIM], xf[..., ROT_DIM:]
    return jnp.concatenate([x1 * cos - x2 * sin, x2 * cos + x1 * sin, rest], axis=-1).astype(x.dtype)


def masked_softmax(s, mask):
    p = jax.nn.softmax(jnp.where(mask, s, NEG), axis=-1)
    return jnp.where(mask, p, 0.0)


def compress(t, pos_emb, w1, w2):
    b, s, g, d = t.shape
    r = CMP_LEN // CMP_STRIDE
    n_chunk = s // CMP_STRIDE
    n_cmp = n_chunk - r + 1
    c = t.reshape(b, n_chunk, CMP_STRIDE, g, d)
    blocks = jnp.concatenate([c[:, j:j + n_cmp] for j in range(r)], axis=2)
    blocks = blocks + pos_emb[None, None, :, None, :].astype(t.dtype)
    flat = blocks.transpose(0, 1, 3, 2, 4).reshape(b, n_cmp, g, CMP_LEN * d)
    return jax.nn.gelu(flat @ w1) @ w2


def nsa_attention(q, kc, vc, ks, vs, kw, vw, gates):
    b, s = q.shape[0], q.shape[1]
    n_cmp = kc.shape[1]
    n_sel_blocks = s // SEL_LEN
    n_top = min(SEL_BLOCKS, n_sel_blocks)
    scale = HEAD_DIM ** -0.5
    dt = q.dtype
    cmp_end = jnp.asarray(np.arange(n_cmp) * CMP_STRIDE + CMP_LEN - 1)
    ci = np.arange(n_cmp)[:, None] * CMP_STRIDE
    sj = np.arange(n_sel_blocks)[None, :] * SEL_LEN
    overlap = jnp.asarray(((ci < sj + SEL_LEN) & (ci + CMP_LEN > sj)).astype(np.float32))
    ks_blk = ks.reshape(b, n_sel_blocks, SEL_LEN, NSA_GROUPS, HEAD_DIM).transpose(0, 3, 1, 2, 4)
    vs_blk = vs.reshape(b, n_sel_blocks, SEL_LEN, NSA_GROUPS, HEAD_DIM).transpose(0, 3, 1, 2, 4)
    kw_pad = jnp.pad(kw, ((0, 0), (WINDOW, 0), (0, 0), (0, 0)))
    vw_pad = jnp.pad(vw, ((0, 0), (WINDOW, 0), (0, 0), (0, 0)))
    bix = jnp.arange(b)[:, None, None, None]
    gix = jnp.arange(NSA_GROUPS)[None, :, None, None]
    blk_ids = jnp.arange(n_sel_blocks)
    in_blk = jnp.arange(SEL_LEN)
    win_off = jnp.arange(WINDOW + Q_BLOCK) - WINDOW

    def block(bi):
        q0 = bi * Q_BLOCK
        t = q0 + jnp.arange(Q_BLOCK)
        qg = lax.dynamic_slice_in_dim(q, q0, Q_BLOCK, 1).reshape(b, Q_BLOCK, NSA_GROUPS, NSA_REP, HEAD_DIM)
        s_c = jnp.einsum('bqgrd,bngd->bgrqn', qg, kc).astype(jnp.float32) * scale
        p_c = masked_softmax(s_c, cmp_end[None, :] <= t[:, None])
        o_c = jnp.einsum('bgrqn,bngd->bqgrd', p_c.astype(dt), vc)
        imp = jnp.einsum('bgrqn,nj->bgqj', p_c, overlap)
        cur = (t // SEL_LEN)[:, None]
        forced = (blk_ids[None] == 0) | (blk_ids[None] == cur) | (blk_ids[None] == cur - 1)
        imp = jnp.where(forced, jnp.inf, imp)
        imp = jnp.where(blk_ids[None] * SEL_LEN <= t[:, None], imp, -jnp.inf)
        _, sel = lax.top_k(imp, n_top)
        k_g = ks_blk[bix, gix, sel].reshape(b, NSA_GROUPS, Q_BLOCK, n_top * SEL_LEN, HEAD_DIM)
        v_g = vs_blk[bix, gix, sel].reshape(b, NSA_GROUPS, Q_BLOCK, n_top * SEL_LEN, HEAD_DIM)
        key_pos = (sel[..., None] * SEL_LEN + in_blk).reshape(b, NSA_GROUPS, 1, Q_BLOCK, n_top * SEL_LEN)
        s_s = jnp.einsum('bqgrd,bgqmd->bgrqm', qg, k_g).astype(jnp.float32) * scale
        p_s = masked_softmax(s_s, key_pos <= t[:, None])
        o_s = jnp.einsum('bgrqm,bgqmd->bqgrd', p_s.astype(dt), v_g)
        k_w = lax.dynamic_slice_in_dim(kw_pad, q0, WINDOW + Q_BLOCK, 1)
        v_w = lax.dynamic_slice_in_dim(vw_pad, q0, WINDOW + Q_BLOCK, 1)
        kpos = q0 + win_off
        diff = t[:, None] - kpos[None, :]
        mask_w = (kpos[None, :] >= 0) & (diff >= 0) & (diff < WINDOW)
        s_w = jnp.einsum('bqgrd,bkgd->bgrqk', qg, k_w).astype(jnp.float32) * scale
        p_w = masked_softmax(s_w, mask_w)
        o_w = jnp.einsum('bgrqk,bkgd->bqgrd', p_w.astype(dt), v_w)
        g = lax.dynamic_slice_in_dim(gates, q0, Q_BLOCK, 1)[..., None]
        o = g[:, :, 0] * o_c + g[:, :, 1] * o_s + g[:, :, 2] * o_w
        return o.reshape(b, Q_BLOCK, MIX_A)

    out = lax.map(block, jnp.arange(s // Q_BLOCK))
    return out.transpose(1, 0, 2, 3).reshape(b, s, MIX_A)


def dsa_attention(q, k, v, iq, ik, iw):
    b, s = q.shape[0], q.shape[1]
    n_keep = min(DSA_TOPK, s // 4)
    scale = HEAD_DIM ** -0.5
    dt = q.dtype
    key_pos = jnp.arange(s)
    bix = jnp.arange(b)[:, None, None]

    def block(bi):
        q0 = bi * Q_BLOCK
        t = q0 + jnp.arange(Q_BLOCK)
        qb = lax.dynamic_slice_in_dim(q, q0, Q_BLOCK, 1)
        iqb = lax.dynamic_slice_in_dim(iq, q0, Q_BLOCK, 1)
        iwb = lax.dynamic_slice_in_dim(iw, q0, Q_BLOCK, 1).astype(jnp.float32) * IDX_HEADS ** -0.5
        logits = jnp.einsum('bqhd,bsd->bqhs', iqb, ik).astype(jnp.float32) * IDX_DIM ** -0.5
        score = jnp.einsum('bqhs,bqh->bqs', jax.nn.relu(logits), iwb)
        score = jnp.where(key_pos[None, None, :] <= t[None, :, None], score, -jnp.inf)
        _, sel = lax.top_k(score, n_keep)
        k_g = k[bix, sel]
        v_g = v[bix, sel]
        att = jnp.einsum('bqhd,bqkd->bhqk', qb, k_g).astype(jnp.float32) * scale
        p = masked_softmax(att, (sel <= t[None, :, None])[:, None])
        o = jnp.einsum('bhqk,bqkd->bqhd', p.astype(dt), v_g)
        return o.reshape(b, Q_BLOCK, MIX_B)

    out = lax.map(block, jnp.arange(s // Q_BLOCK))
    return out.transpose(1, 0, 2, 3).reshape(b, s, MIX_B)


def peer(h, wq, sub_keys, u, v):
    b, s, d = h.shape
    dt = h.dtype
    tok = h.reshape(b * s // PEER_TOKEN_BLOCK, PEER_TOKEN_BLOCK, d)

    def block(xb):
        q = (xb @ wq).reshape(PEER_TOKEN_BLOCK, PEER_HEADS, 2, PEER_QDIM // 2)
        s1 = jnp.einsum('thd,kd->thk', q[:, :, 0], sub_keys[0]).astype(jnp.float32)
        s2 = jnp.einsum('thd,kd->thk', q[:, :, 1], sub_keys[1]).astype(jnp.float32)
        v1, i1 = lax.top_k(s1, PEER_HALF_TOPK)
        v2, i2 = lax.top_k(s2, PEER_HALF_TOPK)
        cand = (v1[..., :, None] + v2[..., None, :]).reshape(PEER_TOKEN_BLOCK, PEER_HEADS, -1)
        cidx = (i1[..., :, None] * N_KEYS + i2[..., None, :]).reshape(PEER_TOKEN_BLOCK, PEER_HEADS, -1)
        top, pos = lax.top_k(cand, PEER_TOPK)
        eidx = jnp.take_along_axis(cidx, pos, axis=-1)
        gate = jax.nn.softmax(top, axis=-1)
        act = jax.nn.gelu(jnp.einsum('td,thkd->thk', xb, u[eidx]).astype(jnp.float32))
        return jnp.einsum('thk,thkd->td', (gate * act).astype(dt), v[eidx])

    return lax.map(block, tok).reshape(b, s, d)


def hybrid_layer(x, p_i, positions, attn_norm, w_in, nsa_qk_gain, cmp_pos, cmp_w1, cmp_w2,
                 dsa_qk_gain, w_branch_a, w_branch_b, w_out, ffn_norm, peer_wq, peer_sub_keys,
                 peer_u, peer_v, ple_w, ple_gate_w, ple_norm):
    b, s, _ = x.shape
    n_pad = (-IN_COLS) % 512
    w_in_p = jnp.pad(w_in, ((0, 0), (0, n_pad))).astype(jnp.bfloat16)
    proj = norm_matmul(x.reshape(b * s, D_MODEL), attn_norm, w_in_p)[:, :IN_COLS]
    proj = proj.reshape(b, s, IN_COLS)
    cuts = np.cumsum(IN_SIZES)[:-1].tolist()
    nq, nkv, ngate, dq, dkv, iq, ik, iw, mg = jnp.split(proj, cuts, axis=-1)

    qa = rope(rms_norm(nq.reshape(b, s, NSA_HEADS, HEAD_DIM), nsa_qk_gain[0]), positions)
    kv = nkv.reshape(b, s, 6, NSA_GROUPS, HEAD_DIM)
    kc = compress(kv[:, :, 0], cmp_pos[0], cmp_w1[0], cmp_w2[0])
    vc = compress(kv[:, :, 1], cmp_pos[1], cmp_w1[1], cmp_w2[1])
    n_cmp = kc.shape[1]
    cmp_pos_ids = positions[:, CMP_LEN - 1::CMP_STRIDE][:, :n_cmp]
    kc = rope(rms_norm(kc, nsa_qk_gain[1]), cmp_pos_ids)
    ks = rope(rms_norm(kv[:, :, 2], nsa_qk_gain[2]), positions)
    kw = rope(rms_norm(kv[:, :, 4], nsa_qk_gain[3]), positions)
    gates = jax.nn.sigmoid(ngate).reshape(b, s, 3, NSA_GROUPS, NSA_REP)
    ya = nsa_attention_pallas(qa, kc, vc, ks, kv[:, :, 3], kw, kv[:, :, 5], gates)

    qb = rope(rms_norm(dq.reshape(b, s, DSA_HEADS, HEAD_DIM), dsa_qk_gain[0]), positions)
    dkv = dkv.reshape(b, s, 2, HEAD_DIM)
    kb = rope(rms_norm(dkv[:, :, :1], dsa_qk_gain[1]), positions)[:, :, 0]
    vb = dkv[:, :, 1]
    iq = rope(iq.reshape(b, s, IDX_HEADS, IDX_DIM), positions)
    ik = rope(ik[:, :, None], positions)[:, :, 0]
    yb = dsa_attention_pallas(qb, kb, vb, iq, ik, iw)

    ga, gb = jnp.split(jax.nn.sigmoid(mg), 2, axis=-1)
    x = x + (ga * (ya @ w_branch_a) + gb * (yb @ w_branch_b)) @ w_out

    x = x + peer(rms_norm(x, ffn_norm), peer_wq, peer_sub_keys, peer_u, peer_v)

    gate = jax.nn.sigmoid(_rms(x).astype(x.dtype) @ ple_gate_w)
    return x + gate * rms_norm(p_i @ ple_w, ple_norm)


def kernel(x, p, positions, attn_norm, w_in, nsa_qk_gain, cmp_pos, cmp_w1, cmp_w2,
           dsa_qk_gain, w_branch_a, w_branch_b, w_out, ffn_norm, peer_wq, peer_sub_keys,
           peer_u, peer_v, ple_w, ple_gate_w, ple_norm):
    for i in range(DEPTH):
        x = hybrid_layer(x, p[i], positions, attn_norm[i], w_in[i], nsa_qk_gain[i], cmp_pos[i],
                         cmp_w1[i], cmp_w2[i], dsa_qk_gain[i], w_branch_a[i], w_branch_b[i],
                         w_out[i], ffn_norm[i], peer_wq[i], peer_sub_keys[i], peer_u[i],
                         peer_v[i], ple_w[i], ple_gate_w[i], ple_norm[i])
    return x
```

```python
import functools

import numpy as np
import jax
import jax.numpy as jnp
from jax import lax
from jax.experimental import pallas as pl
from jax.experimental.pallas import tpu as pltpu

D_MODEL = 1024
BATCH = 8
SEQ = 4096
DEPTH = 2

HEAD_DIM = 64
ROT_DIM = HEAD_DIM // 4
ROPE_THETA = 500000.0
Q_BLOCK = 128
NEG = -1e30
EPS = 1e-6

NSA_HEADS = 8
NSA_GROUPS = 2
NSA_REP = NSA_HEADS // NSA_GROUPS
CMP_LEN = 32
CMP_STRIDE = 16
CMP_HIDDEN = 256
SEL_LEN = 64
SEL_BLOCKS = 16
WINDOW = 512

DSA_HEADS = 8
IDX_HEADS = 8
IDX_DIM = 64
DSA_TOPK = 256

PEER_HEADS = 8
PEER_QDIM = 256
N_KEYS = 128
N_EXPERTS = N_KEYS * N_KEYS
PEER_HALF_TOPK = 16
PEER_TOPK = 16
PEER_TOKEN_BLOCK = 128

PLE_DIM = 256

MIX_A = NSA_HEADS * HEAD_DIM
MIX_B = DSA_HEADS * HEAD_DIM
IN_SIZES = (
    MIX_A,
    6 * NSA_GROUPS * HEAD_DIM,
    3 * NSA_HEADS,
    MIX_B,
    2 * HEAD_DIM,
    IDX_HEADS * IDX_DIM,
    IDX_DIM,
    IDX_HEADS,
    2 * D_MODEL,
)
IN_COLS = sum(IN_SIZES)

LANE = 128
SUBLANE = 8
VMEM_LIMIT = 48 * 1024 * 1024


def _norm_matmul_kernel(x_ref, g_ref, w_ref, o_ref):
    x = x_ref[...]
    h = x * lax.rsqrt(jnp.mean(x * x, axis=-1, keepdims=True) + EPS) * g_ref[...]
    o_ref[...] = jnp.dot(h.astype(jnp.bfloat16), w_ref[...],
                         preferred_element_type=jnp.float32)


def norm_matmul(x2d, g, w_bf16, *, tm=512, tn=512):
    m, k = x2d.shape
    n = w_bf16.shape[1]
    assert m % tm == 0 and n % tn == 0
    return pl.pallas_call(
        _norm_matmul_kernel,
        grid=(m // tm, n // tn),
        in_specs=[
            pl.BlockSpec((tm, k), lambda i, j: (i, 0)),
            pl.BlockSpec((1, k), lambda i, j: (0, 0)),
            pl.BlockSpec((k, tn), lambda i, j: (0, j)),
        ],
        out_specs=pl.BlockSpec((tm, tn), lambda i, j: (i, j)),
        out_shape=jax.ShapeDtypeStruct((m, n), jnp.float32),
        compiler_params=pltpu.CompilerParams(
            dimension_semantics=("parallel", "arbitrary"),
            vmem_limit_bytes=VMEM_LIMIT),
        name="norm_matmul",
    )(x2d, g.reshape(1, k), w_bf16)


NSA_TQ = 128
NSA_TK = 512
NSA_TW = 128
NSA_WIN_TILES = (WINDOW + NSA_TQ) // NSA_TW
NSA_BLK_PAD = LANE


def _flash_update(state, s, v, mask=None):
    m, l, acc = state
    m_new = jnp.maximum(m, jnp.max(s, axis=-1, keepdims=True))
    alpha = jnp.exp(m - m_new)
    p = jnp.exp(s - m_new)
    if mask is not None:
        p = jnp.where(mask, p, 0.0)
    l = alpha * l + jnp.sum(p, axis=-1, keepdims=True)
    acc = alpha * acc + jnp.dot(p.astype(jnp.bfloat16), v, preferred_element_type=jnp.float32)
    return m_new, l, acc


def _nsa_kernel(q_ref, kct_ref, vc_ref, kst_ref, vs_ref, kwt_ref, vw_ref, g_ref, ovt_ref, exp_ref,
                o_ref, bias_ref, *, n_blk, n_top, n_ktiles):
    f32 = jnp.float32
    bf16 = jnp.bfloat16
    qi = pl.program_id(2)
    q0 = qi * NSA_TQ
    t_col = q0 + lax.broadcasted_iota(jnp.int32, (NSA_TQ, 1), 0)
    t_row = q0 + lax.broadcasted_iota(jnp.int32, (1, NSA_TQ), 1)
    qs = [q_ref[0, :, r * HEAD_DIM:(r + 1) * HEAD_DIM] for r in range(NSA_REP)]

    n_cmp_pad = kct_ref.shape[-1]
    cmp_end = lax.broadcasted_iota(jnp.int32, (1, n_cmp_pad), 1) * CMP_STRIDE + (CMP_LEN - 1)
    mask_c = cmp_end <= t_col
    kct = kct_ref[0, 0]
    vc = vc_ref[0, 0]
    p_sum = jnp.zeros((NSA_TQ, n_cmp_pad), f32)
    o_c = []
    for r in range(NSA_REP):
        s = jnp.dot(qs[r], kct, preferred_element_type=f32)
        s = jnp.where(mask_c, s, NEG)
        m = jnp.max(s, axis=-1, keepdims=True)
        p = jnp.where(mask_c, jnp.exp(s - m), 0.0)
        l = jnp.sum(p, axis=-1, keepdims=True)
        p = p / jnp.where(l > 0.0, l, 1.0)
        p_sum = p_sum + p
        o_c.append(jnp.dot(p.astype(bf16), vc, preferred_element_type=f32))

    nt_dims = (((1,), (1,)), ((), ()))
    p_hi = p_sum.astype(bf16)
    p_lo = (p_sum - p_hi.astype(f32)).astype(bf16)
    ovt = ovt_ref[...]
    imp_t = (lax.dot_general(ovt, p_hi, nt_dims, preferred_element_type=f32)
             + lax.dot_general(ovt, p_lo, nt_dims, preferred_element_type=f32))
    blk = lax.broadcasted_iota(jnp.int32, (NSA_BLK_PAD, 1), 0)
    cur = t_row // SEL_LEN
    forced = (blk == 0) | (blk == cur) | (blk == cur - 1)
    admissible = (blk * SEL_LEN <= t_row) & (blk < n_blk)
    imp_t = jnp.where(forced, jnp.inf, imp_t)
    imp_t = jnp.where(admissible, imp_t, -jnp.inf)
    n_chunks = n_blk // SUBLANE
    chunks = [imp_t[c * SUBLANE:(c + 1) * SUBLANE, :] for c in range(n_chunks)]
    ranks = [jnp.zeros((SUBLANE, NSA_TQ), f32) for _ in range(n_chunks)]
    sub = lax.broadcasted_iota(jnp.int32, (SUBLANE, NSA_TQ), 0)
    for i in range(n_blk):
        ci, si = divmod(i, SUBLANE)
        row = jnp.broadcast_to(chunks[ci][si:si + 1, :], (SUBLANE, NSA_TQ))
        for c in range(n_chunks):
            if c > ci:
                beats = jnp.where(row >= chunks[c], 1.0, 0.0)
            elif c < ci:
                beats = jnp.where(row > chunks[c], 1.0, 0.0)
            else:
                tie = jnp.where(sub > si, 1.0, 0.0)
                beats = jnp.where(row > chunks[c], 1.0, jnp.where(row == chunks[c], tie, 0.0))
            ranks[c] = ranks[c] + beats
    rank = jnp.concatenate(
        ranks + [jnp.full((NSA_BLK_PAD - n_blk, NSA_TQ), float(NSA_BLK_PAD), f32)], axis=0)
    sel_t = jnp.where((rank < n_top) & admissible, 1.0, 0.0)
    sel_q = sel_t.T.astype(bf16)

    n_live = qi // (NSA_TK // NSA_TQ) + 1
    for j in range(n_ktiles):
        @pl.when(j < n_live)
        def _():
            hit = jnp.dot(sel_q, exp_ref[:, j * NSA_TK:(j + 1) * NSA_TK],
                          preferred_element_type=f32)
            kpos = j * NSA_TK + lax.broadcasted_iota(jnp.int32, (1, NSA_TK), 1)
            bias_ref[j] = jnp.where((hit > 0.5) & (kpos <= t_col), 0.0, NEG)

    def init_state():
        return (jnp.full((NSA_TQ, 1), NEG, f32), jnp.zeros((NSA_TQ, 1), f32),
                jnp.zeros((NSA_TQ, HEAD_DIM), f32))

    def sel_body(j, states):
        kt = kst_ref[0, 0, j]
        v = vs_ref[0, 0, j]
        bias = bias_ref[j]
        out = []
        for r in range(NSA_REP):
            s = jnp.dot(qs[r], kt, preferred_element_type=f32) + bias
            out.append(_flash_update(states[r], s, v))
        return tuple(out)

    st_s = lax.fori_loop(0, n_live, sel_body, tuple(init_state() for _ in range(NSA_REP)))

    st_w = [init_state() for _ in range(NSA_REP)]
    for w in range(NSA_WIN_TILES):
        tile = qi - (NSA_WIN_TILES - 1) + w
        tix = jnp.maximum(tile, 0)
        kt = kwt_ref[0, 0, tix]
        v = vw_ref[0, 0, tix]
        kpos = tile * NSA_TW + lax.broadcasted_iota(jnp.int32, (1, NSA_TW), 1)
        diff = t_col - kpos
        mask = (kpos >= 0) & (diff >= 0) & (diff < WINDOW)
        for r in range(NSA_REP):
            s = jnp.where(mask, jnp.dot(qs[r], kt, preferred_element_type=f32), NEG)
            st_w[r] = _flash_update(st_w[r], s, v, mask)

    g = g_ref[0, 0]
    for r in range(NSA_REP):
        o_s = st_s[r][2] / st_s[r][1]
        o_w = st_w[r][2] / st_w[r][1]
        o = (g[:, r:r + 1] * o_c[r] + g[:, NSA_REP + r:NSA_REP + r + 1] * o_s
             + g[:, 2 * NSA_REP + r:2 * NSA_REP + r + 1] * o_w)
        o_ref[0, :, r * HEAD_DIM:(r + 1) * HEAD_DIM] = o


def nsa_attention_pallas(qa, kc, vc, ks, vs, kw, vw, gates):
    b, s = qa.shape[0], qa.shape[1]
    bf16 = jnp.bfloat16
    n_cmp = kc.shape[1]
    n_cmp_pad = -(-n_cmp // LANE) * LANE
    n_blk = s // SEL_LEN
    n_top = min(SEL_BLOCKS, n_blk)
    n_kt = s // NSA_TK
    n_wt = s // NSA_TW
    assert s % NSA_TK == 0 and n_blk <= NSA_BLK_PAD
    grp = NSA_REP * HEAD_DIM

    q = (qa.reshape(b, s, MIX_A) * HEAD_DIM ** -0.5).astype(bf16)
    pad_c = ((0, 0), (0, n_cmp_pad - n_cmp), (0, 0), (0, 0))
    kct = jnp.pad(kc, pad_c).transpose(0, 2, 3, 1).astype(bf16)
    vcp = jnp.pad(vc, pad_c).transpose(0, 2, 1, 3).astype(bf16)

    def key_tiles(k, tk):
        return k.reshape(b, s // tk, tk, NSA_GROUPS, HEAD_DIM).transpose(0, 3, 1, 4, 2).astype(bf16)

    def val_tiles(v, tk):
        return v.reshape(b, s // tk, tk, NSA_GROUPS, HEAD_DIM).transpose(0, 3, 1, 2, 4).astype(bf16)

    gates_g = gates.transpose(0, 3, 1, 2, 4).reshape(b, NSA_GROUPS, s, 3 * NSA_REP)

    ci = np.arange(n_cmp_pad)[None, :] * CMP_STRIDE
    sj = np.arange(NSA_BLK_PAD)[:, None] * SEL_LEN
    ovt = ((ci < sj + SEL_LEN) & (ci + CMP_LEN > sj) & (np.arange(n_cmp_pad)[None, :] < n_cmp)
           & (np.arange(NSA_BLK_PAD)[:, None] < n_blk))
    expand = (np.arange(s)[None, :] // SEL_LEN) == np.arange(NSA_BLK_PAD)[:, None]

    per_bg = lambda bi, gi, i: (bi, gi, 0, 0)
    per_bg5 = lambda bi, gi, i: (bi, gi, 0, 0, 0)
    return pl.pallas_call(
        functools.partial(_nsa_kernel, n_blk=n_blk, n_top=n_top, n_ktiles=n_kt),
        grid=(b, NSA_GROUPS, s // NSA_TQ),
        in_specs=[
            pl.BlockSpec((1, NSA_TQ, grp), lambda bi, gi, i: (bi, i, gi)),
            pl.BlockSpec((1, 1, HEAD_DIM, n_cmp_pad), per_bg),
            pl.BlockSpec((1, 1, n_cmp_pad, HEAD_DIM), per_bg),
            pl.BlockSpec((1, 1, n_kt, HEAD_DIM, NSA_TK), per_bg5),
            pl.BlockSpec((1, 1, n_kt, NSA_TK, HEAD_DIM), per_bg5),
            pl.BlockSpec((1, 1, n_wt, HEAD_DIM, NSA_TW), per_bg5),
            pl.BlockSpec((1, 1, n_wt, NSA_TW, HEAD_DIM), per_bg5),
            pl.BlockSpec((1, 1, NSA_TQ, 3 * NSA_REP), lambda bi, gi, i: (bi, gi, i, 0)),
            pl.BlockSpec((NSA_BLK_PAD, n_cmp_pad), lambda bi, gi, i: (0, 0)),
            pl.BlockSpec((NSA_BLK_PAD, s), lambda bi, gi, i: (0, 0)),
        ],
        out_specs=pl.BlockSpec((1, NSA_TQ, grp), lambda bi, gi, i: (bi, i, gi)),
        out_shape=jax.ShapeDtypeStruct((b, s, MIX_A), jnp.float32),
        scratch_shapes=[pltpu.VMEM((n_kt, NSA_TQ, NSA_TK), jnp.float32)],
        compiler_params=pltpu.CompilerParams(
            dimension_semantics=("parallel", "parallel", "arbitrary"),
            vmem_limit_bytes=VMEM_LIMIT),
        name="nsa_attention",
    )(q, kct, vcp, key_tiles(ks, NSA_TK), val_tiles(vs, NSA_TK), key_tiles(kw, NSA_TW),
      val_tiles(vw, NSA_TW), gates_g, jnp.asarray(ovt, bf16), jnp.asarray(expand, bf16))


DSA_TQ = 128
DSA_TK = 512
F32_ORDER_MASK = 0x7FFFFFFF
F32_NEG_INF_BITS = -8388608
F32_POS_INF_BITS = 0x7F800000


def _ordered_from_bits(bits):
    return jnp.where(bits >= 0, bits, bits ^ F32_ORDER_MASK)


def _lane_tile_sum(x):
    acc = x[:, :LANE]
    for c in range(1, x.shape[1] // LANE):
        acc = acc + x[:, c * LANE:(c + 1) * LANE]
    return acc


def _dsa_kernel(iq_ref, w_ref, ikt_ref, q_ref, kt_ref, v_ref, tri_ref, o_ref, sc_ref, *, n_keep):
    f32 = jnp.float32
    bf16 = jnp.bfloat16
    qi = pl.program_id(1)
    q0 = qi * DSA_TQ
    t_col = q0 + lax.broadcasted_iota(jnp.int32, (DSA_TQ, 1), 0)
    n_live = qi // (DSA_TK // DSA_TQ) + 1
    lane_pos = lax.broadcasted_iota(jnp.int32, (1, DSA_TK), 1)

    iqs = [iq_ref[0, :, h * IDX_DIM:(h + 1) * IDX_DIM] for h in range(IDX_HEADS)]
    w = w_ref[0]

    def score_body(j, carry):
        ikt = ikt_ref[0, j]
        acc = jnp.zeros((DSA_TQ, DSA_TK), f32)
        for h in range(IDX_HEADS):
            logit = jnp.dot(iqs[h], ikt, preferred_element_type=f32)
            acc = acc + jnp.maximum(logit, 0.0) * w[:, h:h + 1]
        causal = (j * DSA_TK + lane_pos) <= t_col
        sc_ref[j] = jnp.where(causal, acc + 0.0, -jnp.inf)
        return carry

    lax.fori_loop(0, n_live, score_body, 0)

    def count_ge(thr):
        def body(j, acc):
            return acc + _lane_tile_sum(jnp.where(sc_ref[j] >= thr, 1.0, 0.0))
        part = lax.fori_loop(0, n_live, body, jnp.zeros((DSA_TQ, LANE), f32))
        return jnp.sum(part, axis=-1, keepdims=True)

    def bisect_body(_, lohi):
        lo, hi = lohi
        mid = (lo >> 1) + (hi >> 1) + (lo & hi & 1)
        thr = lax.bitcast_convert_type(_ordered_from_bits(mid), f32)
        ok = count_ge(thr) >= n_keep
        return jnp.where(ok, mid, lo), jnp.where(ok, hi, mid)

    lo0 = jnp.full((DSA_TQ, 1), F32_NEG_INF_BITS ^ F32_ORDER_MASK, jnp.int32)
    hi0 = jnp.full((DSA_TQ, 1), F32_POS_INF_BITS + 1, jnp.int32)
    lo, _ = lax.fori_loop(0, 32, bisect_body, (lo0, hi0))
    thr = lax.bitcast_convert_type(_ordered_from_bits(lo), f32)

    def count_gt_body(j, acc):
        return acc + _lane_tile_sum(jnp.where(sc_ref[j] > thr, 1.0, 0.0))
    n_gt = jnp.sum(lax.fori_loop(0, n_live, count_gt_body, jnp.zeros((DSA_TQ, LANE), f32)),
                   axis=-1, keepdims=True)
    need = n_keep - n_gt

    tri = tri_ref[...]

    def mask_body(j, ties_before):
        sc = sc_ref[j]
        eq = sc == thr
        pref = ties_before + jnp.dot(jnp.where(eq, 1.0, 0.0).astype(bf16), tri,
                                     preferred_element_type=f32)
        keep_tie = jnp.where(pref <= need, 0.0, NEG)
        bias = jnp.where(sc > thr, 0.0, jnp.where(eq, keep_tie, NEG))
        causal = (j * DSA_TK + lane_pos) <= t_col
        sc_ref[j] = jnp.where(causal, bias, NEG)
        return pref[:, DSA_TK - 1:DSA_TK]

    lax.fori_loop(0, n_live, mask_body, jnp.zeros((DSA_TQ, 1), f32))

    qs = [q_ref[0, :, h * HEAD_DIM:(h + 1) * HEAD_DIM] for h in range(DSA_HEADS)]

    def att_body(j, states):
        kt = kt_ref[0, j]
        v = v_ref[0, j]
        bias = sc_ref[j]
        out = []
        for h in range(DSA_HEADS):
            s = jnp.dot(qs[h], kt, preferred_element_type=f32) + bias
            out.append(_flash_update(states[h], s, v))
        return tuple(out)

    init = tuple((jnp.full((DSA_TQ, 1), NEG, f32), jnp.zeros((DSA_TQ, 1), f32),
                  jnp.zeros((DSA_TQ, HEAD_DIM), f32)) for _ in range(DSA_HEADS))
    st = lax.fori_loop(0, n_live, att_body, init)
    for h in range(DSA_HEADS):
        o_ref[0, :, h * HEAD_DIM:(h + 1) * HEAD_DIM] = st[h][2] / st[h][1]


def dsa_attention_pallas(q, k, v, iq, ik, iw):
    b, s = q.shape[0], q.shape[1]
    bf16 = jnp.bfloat16
    n_keep = min(DSA_TOPK, s // 4)
    n_kt = s // DSA_TK
    assert s % DSA_TK == 0

    def key_tiles(x):
        return x.reshape(b, n_kt, DSA_TK, x.shape[-1]).transpose(0, 1, 3, 2).astype(bf16)

    iq2 = (iq.reshape(b, s, IDX_HEADS * IDX_DIM) * IDX_DIM ** -0.5).astype(bf16)
    q2 = (q.reshape(b, s, MIX_B) * HEAD_DIM ** -0.5).astype(bf16)
    w = iw.astype(jnp.float32) * IDX_HEADS ** -0.5
    v_t = v.reshape(b, n_kt, DSA_TK, HEAD_DIM).astype(bf16)
    tri = np.arange(DSA_TK)[:, None] <= np.arange(DSA_TK)[None, :]

    per_b = lambda bi, i: (bi, 0, 0, 0)
    return pl.pallas_call(
        functools.partial(_dsa_kernel, n_keep=n_keep),
        grid=(b, s // DSA_TQ),
        in_specs=[
            pl.BlockSpec((1, DSA_TQ, IDX_HEADS * IDX_DIM), lambda bi, i: (bi, i, 0)),
            pl.BlockSpec((1, DSA_TQ, IDX_HEADS), lambda bi, i: (bi, i, 0)),
            pl.BlockSpec((1, n_kt, IDX_DIM, DSA_TK), per_b),
            pl.BlockSpec((1, DSA_TQ, MIX_B), lambda bi, i: (bi, i, 0)),
            pl.BlockSpec((1, n_kt, HEAD_DIM, DSA_TK), per_b),
            pl.BlockSpec((1, n_kt, DSA_TK, HEAD_DIM), per_b),
            pl.BlockSpec((DSA_TK, DSA_TK), lambda bi, i: (0, 0)),
        ],
        out_specs=pl.BlockSpec((1, DSA_TQ, MIX_B), lambda bi, i: (bi, i, 0)),
        out_shape=jax.ShapeDtypeStruct((b, s, MIX_B), jnp.float32),
        scratch_shapes=[pltpu.VMEM((n_kt, DSA_TQ, DSA_TK), jnp.float32)],
        compiler_params=pltpu.CompilerParams(
            dimension_semantics=("parallel", "arbitrary"),
            vmem_limit_bytes=VMEM_LIMIT),
        name="dsa_attention",
    )(iq2, w, key_tiles(ik), q2, key_tiles(k), v_t, jnp.asarray(tri, bf16))


PEER_TT = 128
PEER_SLOTS = PEER_HEADS * PEER_TOPK
PEER_CAND_A0 = PEER_HALF_TOPK
PEER_CAND_SQ = SUBLANE


def _peer_cand_flat_ids():
    ids = [0 * PEER_HALF_TOPK + bb for bb in range(PEER_CAND_A0)]
    for a in range(1, PEER_CAND_SQ):
        ids += [a * PEER_HALF_TOPK + bb for bb in range(PEER_CAND_SQ)]
    ids += [a * PEER_HALF_TOPK for a in range(PEER_CAND_SQ, PEER_HALF_TOPK)]
    return np.asarray(ids, np.int32)


def _extract_top(cur, ids, n):
    vals, picks = [], []
    for _ in range(n):
        m = jnp.max(cur, axis=0, keepdims=True)
        pick = jnp.min(jnp.where(cur == m, ids, jnp.int32(2 ** 30)), axis=0, keepdims=True)
        vals.append(m)
        picks.append(pick)
        cur = jnp.where(ids == pick, -jnp.inf, cur)
    return vals, picks


def _pair_grid(first, second, op):
    pieces = [op(first[0:1], second)]
    for a in range(1, PEER_CAND_SQ):
        pieces.append(op(first[a:a + 1], second[0:PEER_CAND_SQ]))
    pieces.append(op(first[PEER_CAND_SQ:], second[0:1]))
    return jnp.concatenate(pieces, axis=0)


def _peer_topk_kernel(x_ref, g_ref, wq_ref, sk_ref, fid_ref, h_ref, eidx_ref, gate_ref):
    f32 = jnp.float32
    bf16 = jnp.bfloat16
    x = x_ref[...]
    h = x * lax.rsqrt(jnp.mean(x * x, axis=-1, keepdims=True) + EPS) * g_ref[...]
    h_ref[...] = h
    q = jnp.dot(h.astype(bf16), wq_ref[...], preferred_element_type=f32).astype(bf16)
    key_ids = lax.broadcasted_iota(jnp.int32, (N_KEYS, PEER_TT), 0)
    fid = fid_ref[...]
    nt_dims = (((1,), (1,)), ((), ()))
    half = PEER_QDIM // 2
    for hd in range(PEER_HEADS):
        tops = []
        for c in range(2):
            col = (hd * 2 + c) * half
            s_t = lax.dot_general(sk_ref[c], q[:, col:col + half], nt_dims,
                                  preferred_element_type=f32)
            vals, picks = _extract_top(s_t, key_ids, PEER_HALF_TOPK)
            tops.append((jnp.concatenate(vals, axis=0), jnp.concatenate(picks, axis=0)))
        (v1, i1), (v2, i2) = tops
        cand = _pair_grid(v1, v2, lambda a, b: a + b)
        cexp = _pair_grid(i1, i2, lambda a, b: a * N_KEYS + b)
        vals, picks = _extract_top(cand, fid, PEER_TOPK)
        top = jnp.concatenate(vals, axis=0)
        eids = [jnp.max(jnp.where(fid == p, cexp, -1), axis=0, keepdims=True) for p in picks]
        ex = jnp.exp(top - top[0:1])
        gate = ex / jnp.sum(ex, axis=0, keepdims=True)
        eidx_ref[0, hd * PEER_TOPK:(hd + 1) * PEER_TOPK, :] = jnp.concatenate(eids, axis=0)
        gate_ref[0, hd * PEER_TOPK:(hd + 1) * PEER_TOPK, :] = gate


def peer_topk_pallas(x2d, ffn_norm, wq, sub_keys):
    t, d = x2d.shape
    n_tiles = t // PEER_TT
    fid = np.broadcast_to(_peer_cand_flat_ids()[:, None], (_peer_cand_flat_ids().shape[0], PEER_TT))
    n_cand = fid.shape[0]
    return pl.pallas_call(
        _peer_topk_kernel,
        grid=(n_tiles,),
        in_specs=[
            pl.BlockSpec((PEER_TT, d), lambda i: (i, 0)),
            pl.BlockSpec((1, d), lambda i: (0, 0)),
            pl.BlockSpec((d, PEER_HEADS * PEER_QDIM), lambda i: (0, 0)),
            pl.BlockSpec((2, N_KEYS, PEER_QDIM // 2), lambda i: (0, 0, 0)),
            pl.BlockSpec((n_cand, PEER_TT), lambda i: (0, 0)),
        ],
        out_specs=[
            pl.BlockSpec((PEER_TT, d), lambda i: (i, 0)),
            pl.BlockSpec((1, PEER_SLOTS, PEER_TT), lambda i: (i, 0, 0)),
            pl.BlockSpec((1, PEER_SLOTS, PEER_TT), lambda i: (i, 0, 0)),
        ],
        out_shape=[
            jax.ShapeDtypeStruct((t, d), jnp.float32),
            jax.ShapeDtypeStruct((n_tiles, PEER_SLOTS, PEER_TT), jnp.int32),
            jax.ShapeDtypeStruct((n_tiles, PEER_SLOTS, PEER_TT), jnp.float32),
        ],
        compiler_params=pltpu.CompilerParams(
            dimension_semantics=("parallel",), vmem_limit_bytes=VMEM_LIMIT),
        name="peer_topk",
    )(x2d, ffn_norm.reshape(1, d), wq.astype(jnp.bfloat16), sub_keys.astype(jnp.bfloat16),
      jnp.asarray(fid))


PEER_GT = 8
PEER_ROWS = PEER_GT * PEER_SLOTS
PEER_ISSUE_UNROLL = 16


def _peer_row_copy(uv_hbm, buf, sem, slot, e, r):
    return pltpu.make_async_copy(uv_hbm.at[pl.ds(e, 1)], buf.at[slot, pl.ds(r, 1)], sem.at[slot])


def _peer_issue(idx_ref, uv_hbm, buf, sem, slot):
    def body(c, carry):
        for k in range(PEER_ISSUE_UNROLL):
            r = c * PEER_ISSUE_UNROLL + k
            _peer_row_copy(uv_hbm, buf, sem, slot, idx_ref[0, 0, r], r).start()
        return carry
    lax.fori_loop(0, PEER_ROWS // PEER_ISSUE_UNROLL, body, 0)


def _peer_eval_kernel(idx_cur_ref, idx_nxt_ref, h_ref, gate_ref, x_ref, uv_hbm, o_ref, buf, sem):
    f32 = jnp.float32
    i = pl.program_id(0)
    n = pl.num_programs(0)
    slot = i % 2

    @pl.when(i == 0)
    def _():
        _peer_issue(idx_cur_ref, uv_hbm, buf, sem, 0)

    @pl.when(i + 1 < n)
    def _():
        _peer_issue(idx_nxt_ref, uv_hbm, buf, sem, 1 - slot)

    pltpu.make_async_copy(uv_hbm.at[pl.ds(0, PEER_ROWS)], buf.at[slot], sem.at[slot]).wait()

    gate_t = gate_ref[0]
    lane = lax.broadcasted_iota(jnp.int32, (PEER_SLOTS, PEER_TT), 1)
    lane0 = (i % (PEER_TT // PEER_GT)) * PEER_GT
    for tok in range(PEER_GT):
        rows = buf[slot, tok * PEER_SLOTS:(tok + 1) * PEER_SLOTS, :]
        act = jnp.sum(rows[:, :D_MODEL] * h_ref[tok:tok + 1, :], axis=-1, keepdims=True)
        gate = jnp.sum(jnp.where(lane == lane0 + tok, gate_t, 0.0), axis=-1, keepdims=True)
        wgt = gate * jax.nn.gelu(act)
        out = jnp.sum(wgt * rows[:, D_MODEL:], axis=0, keepdims=True)
        o_ref[tok:tok + 1, :] = x_ref[tok:tok + 1, :] + out


def peer_eval_pallas(x2d, h2d, eidx_t, gate_t, u, v):
    t, d = x2d.shape
    n_steps = t // PEER_GT
    uv = jnp.concatenate([u, v], axis=1)
    idx = eidx_t.transpose(0, 2, 1).reshape(n_steps, 1, PEER_ROWS)
    per_tile = PEER_TT // PEER_GT
    smem_spec = lambda f: pl.BlockSpec((1, 1, PEER_ROWS), f, memory_space=pltpu.SMEM)
    return pl.pallas_call(
        _peer_eval_kernel,
        grid=(n_steps,),
        in_specs=[
            smem_spec(lambda i: (i, 0, 0)),
            smem_spec(lambda i: (jnp.minimum(i + 1, n_steps - 1), 0, 0)),
            pl.BlockSpec((PEER_GT, d), lambda i: (i, 0)),
            pl.BlockSpec((1, PEER_SLOTS, PEER_TT), lambda i: (i // per_tile, 0, 0)),
            pl.BlockSpec((PEER_GT, d), lambda i: (i, 0)),
            pl.BlockSpec(memory_space=pl.ANY),
        ],
        out_specs=pl.BlockSpec((PEER_GT, d), lambda i: (i, 0)),
        out_shape=jax.ShapeDtypeStruct((t, d), jnp.float32),
        scratch_shapes=[pltpu.VMEM((2, PEER_ROWS, 2 * d), jnp.float32),
                        pltpu.SemaphoreType.DMA((2,))],
        compiler_params=pltpu.CompilerParams(
            dimension_semantics=("arbitrary",), vmem_limit_bytes=VMEM_LIMIT),
        name="peer_eval",
    )(idx, idx, h2d, gate_t, x2d, uv)


def peer_pallas(x, ffn_norm, wq, sub_keys, u, v):
    b, s, d = x.shape
    x2d = x.reshape(b * s, d)
    h2d, eidx_t, gate_t = peer_topk_pallas(x2d, ffn_norm, wq, sub_keys)
    return peer_eval_pallas(x2d, h2d, eidx_t, gate_t, u, v).reshape(b, s, d)


def _rms(x):
    xf = x.astype(jnp.float32)
    return xf * lax.rsqrt(jnp.mean(xf * xf, axis=-1, keepdims=True) + EPS)


def rms_norm(x, g):
    return (_rms(x) * g.astype(jnp.float32)).astype(x.dtype)


def rope(x, pos):
    half = ROT_DIM // 2
    inv = ROPE_THETA ** (-jnp.arange(half, dtype=jnp.float32) / half)
    ang = pos.astype(jnp.float32)[..., None] * inv
    cos = jnp.cos(ang)[:, :, None, :]
    sin = jnp.sin(ang)[:, :, None, :]
    xf = x.astype(jnp.float32)
    x1, x2, rest = xf[..., :half], xf[..., half:ROT_DIM], xf[..., ROT_DIM:]
    return jnp.concatenate([x1 * cos - x2 * sin, x2 * cos + x1 * sin, rest], axis=-1).astype(x.dtype)


def masked_softmax(s, mask):
    p = jax.nn.softmax(jnp.where(mask, s, NEG), axis=-1)
    return jnp.where(mask, p, 0.0)


def compress(t, pos_emb, w1, w2):
    b, s, g, d = t.shape
    r = CMP_LEN // CMP_STRIDE
    n_chunk = s // CMP_STRIDE
    n_cmp = n_chunk - r + 1
    c = t.reshape(b, n_chunk, CMP_STRIDE, g, d)
    blocks = jnp.concatenate([c[:, j:j + n_cmp] for j in range(r)], axis=2)
    blocks = blocks + pos_emb[None, None, :, None, :].astype(t.dtype)
    flat = blocks.transpose(0, 1, 3, 2, 4).reshape(b, n_cmp, g, CMP_LEN * d)
    return jax.nn.gelu(flat @ w1) @ w2


def nsa_attention(q, kc, vc, ks, vs, kw, vw, gates):
    b, s = q.shape[0], q.shape[1]
    n_cmp = kc.shape[1]
    n_sel_blocks = s // SEL_LEN
    n_top = min(SEL_BLOCKS, n_sel_blocks)
    scale = HEAD_DIM ** -0.5
    dt = q.dtype
    cmp_end = jnp.asarray(np.arange(n_cmp) * CMP_STRIDE + CMP_LEN - 1)
    ci = np.arange(n_cmp)[:, None] * CMP_STRIDE
    sj = np.arange(n_sel_blocks)[None, :] * SEL_LEN
    overlap = jnp.asarray(((ci < sj + SEL_LEN) & (ci + CMP_LEN > sj)).astype(np.float32))
    ks_blk = ks.reshape(b, n_sel_blocks, SEL_LEN, NSA_GROUPS, HEAD_DIM).transpose(0, 3, 1, 2, 4)
    vs_blk = vs.reshape(b, n_sel_blocks, SEL_LEN, NSA_GROUPS, HEAD_DIM).transpose(0, 3, 1, 2, 4)
    kw_pad = jnp.pad(kw, ((0, 0), (WINDOW, 0), (0, 0), (0, 0)))
    vw_pad = jnp.pad(vw, ((0, 0), (WINDOW, 0), (0, 0), (0, 0)))
    bix = jnp.arange(b)[:, None, None, None]
    gix = jnp.arange(NSA_GROUPS)[None, :, None, None]
    blk_ids = jnp.arange(n_sel_blocks)
    in_blk = jnp.arange(SEL_LEN)
    win_off = jnp.arange(WINDOW + Q_BLOCK) - WINDOW

    def block(bi):
        q0 = bi * Q_BLOCK
        t = q0 + jnp.arange(Q_BLOCK)
        qg = lax.dynamic_slice_in_dim(q, q0, Q_BLOCK, 1).reshape(b, Q_BLOCK, NSA_GROUPS, NSA_REP, HEAD_DIM)
        s_c = jnp.einsum('bqgrd,bngd->bgrqn', qg, kc).astype(jnp.float32) * scale
        p_c = masked_softmax(s_c, cmp_end[None, :] <= t[:, None])
        o_c = jnp.einsum('bgrqn,bngd->bqgrd', p_c.astype(dt), vc)
        imp = jnp.einsum('bgrqn,nj->bgqj', p_c, overlap)
        cur = (t // SEL_LEN)[:, None]
        forced = (blk_ids[None] == 0) | (blk_ids[None] == cur) | (blk_ids[None] == cur - 1)
        imp = jnp.where(forced, jnp.inf, imp)
        imp = jnp.where(blk_ids[None] * SEL_LEN <= t[:, None], imp, -jnp.inf)
        _, sel = lax.top_k(imp, n_top)
        k_g = ks_blk[bix, gix, sel].reshape(b, NSA_GROUPS, Q_BLOCK, n_top * SEL_LEN, HEAD_DIM)
        v_g = vs_blk[bix, gix, sel].reshape(b, NSA_GROUPS, Q_BLOCK, n_top * SEL_LEN, HEAD_DIM)
        key_pos = (sel[..., None] * SEL_LEN + in_blk).reshape(b, NSA_GROUPS, 1, Q_BLOCK, n_top * SEL_LEN)
        s_s = jnp.einsum('bqgrd,bgqmd->bgrqm', qg, k_g).astype(jnp.float32) * scale
        p_s = masked_softmax(s_s, key_pos <= t[:, None])
        o_s = jnp.einsum('bgrqm,bgqmd->bqgrd', p_s.astype(dt), v_g)
        k_w = lax.dynamic_slice_in_dim(kw_pad, q0, WINDOW + Q_BLOCK, 1)
        v_w = lax.dynamic_slice_in_dim(vw_pad, q0, WINDOW + Q_BLOCK, 1)
        kpos = q0 + win_off
        diff = t[:, None] - kpos[None, :]
        mask_w = (kpos[None, :] >= 0) & (diff >= 0) & (diff < WINDOW)
        s_w = jnp.einsum('bqgrd,bkgd->bgrqk', qg, k_w).astype(jnp.float32) * scale
        p_w = masked_softmax(s_w, mask_w)
        o_w = jnp.einsum('bgrqk,bkgd->bqgrd', p_w.astype(dt), v_w)
        g = lax.dynamic_slice_in_dim(gates, q0, Q_BLOCK, 1)[..., None]
        o = g[:, :, 0] * o_c + g[:, :, 1] * o_s + g[:, :, 2] * o_w
        return o.reshape(b, Q_BLOCK, MIX_A)

    out = lax.map(block, jnp.arange(s // Q_BLOCK))
    return out.transpose(1, 0, 2, 3).reshape(b, s, MIX_A)


def dsa_attention(q, k, v, iq, ik, iw):
    b, s = q.shape[0], q.shape[1]
    n_keep = min(DSA_TOPK, s // 4)
    scale = HEAD_DIM ** -0.5
    dt = q.dtype
    key_pos = jnp.arange(s)
    bix = jnp.arange(b)[:, None, None]

    def block(bi):
        q0 = bi * Q_BLOCK
        t = q0 + jnp.arange(Q_BLOCK)
        qb = lax.dynamic_slice_in_dim(q, q0, Q_BLOCK, 1)
        iqb = lax.dynamic_slice_in_dim(iq, q0, Q_BLOCK, 1)
        iwb = lax.dynamic_slice_in_dim(iw, q0, Q_BLOCK, 1).astype(jnp.float32) * IDX_HEADS ** -0.5
        logits = jnp.einsum('bqhd,bsd->bqhs', iqb, ik).astype(jnp.float32) * IDX_DIM ** -0.5
        score = jnp.einsum('bqhs,bqh->bqs', jax.nn.relu(logits), iwb)
        score = jnp.where(key_pos[None, None, :] <= t[None, :, None], score, -jnp.inf)
        _, sel = lax.top_k(score, n_keep)
        k_g = k[bix, sel]
        v_g = v[bix, sel]
        att = jnp.einsum('bqhd,bqkd->bhqk', qb, k_g).astype(jnp.float32) * scale
        p = masked_softmax(att, (sel <= t[None, :, None])[:, None])
        o = jnp.einsum('bhqk,bqkd->bqhd', p.astype(dt), v_g)
        return o.reshape(b, Q_BLOCK, MIX_B)

    out = lax.map(block, jnp.arange(s // Q_BLOCK))
    return out.transpose(1, 0, 2, 3).reshape(b, s, MIX_B)


def peer(h, wq, sub_keys, u, v):
    b, s, d = h.shape
    dt = h.dtype
    tok = h.reshape(b * s // PEER_TOKEN_BLOCK, PEER_TOKEN_BLOCK, d)

    def block(xb):
        q = (xb @ wq).reshape(PEER_TOKEN_BLOCK, PEER_HEADS, 2, PEER_QDIM // 2)
        s1 = jnp.einsum('thd,kd->thk', q[:, :, 0], sub_keys[0]).astype(jnp.float32)
        s2 = jnp.einsum('thd,kd->thk', q[:, :, 1], sub_keys[1]).astype(jnp.float32)
        v1, i1 = lax.top_k(s1, PEER_HALF_TOPK)
        v2, i2 = lax.top_k(s2, PEER_HALF_TOPK)
        cand = (v1[..., :, None] + v2[..., None, :]).reshape(PEER_TOKEN_BLOCK, PEER_HEADS, -1)
        cidx = (i1[..., :, None] * N_KEYS + i2[..., None, :]).reshape(PEER_TOKEN_BLOCK, PEER_HEADS, -1)
        top, pos = lax.top_k(cand, PEER_TOPK)
        eidx = jnp.take_along_axis(cidx, pos, axis=-1)
        gate = jax.nn.softmax(top, axis=-1)
        act = jax.nn.gelu(jnp.einsum('td,thkd->thk', xb, u[eidx]).astype(jnp.float32))
        return jnp.einsum('thk,thkd->td', (gate * act).astype(dt), v[eidx])

    return lax.map(block, tok).reshape(b, s, d)


def hybrid_layer(x, p_i, positions, attn_norm, w_in, nsa_qk_gain, cmp_pos, cmp_w1, cmp_w2,
                 dsa_qk_gain, w_branch_a, w_branch_b, w_out, ffn_norm, peer_wq, peer_sub_keys,
                 peer_u, peer_v, ple_w, ple_gate_w, ple_norm):
    b, s, _ = x.shape
    n_pad = (-IN_COLS) % 512
    w_in_p = jnp.pad(w_in, ((0, 0), (0, n_pad))).astype(jnp.bfloat16)
    proj = norm_matmul(x.reshape(b * s, D_MODEL), attn_norm, w_in_p)[:, :IN_COLS]
    proj = proj.reshape(b, s, IN_COLS)
    cuts = np.cumsum(IN_SIZES)[:-1].tolist()
    nq, nkv, ngate, dq, dkv, iq, ik, iw, mg = jnp.split(proj, cuts, axis=-1)

    qa = rope(rms_norm(nq.reshape(b, s, NSA_HEADS, HEAD_DIM), nsa_qk_gain[0]), positions)
    kv = nkv.reshape(b, s, 6, NSA_GROUPS, HEAD_DIM)
    kc = compress(kv[:, :, 0], cmp_pos[0], cmp_w1[0], cmp_w2[0])
    vc = compress(kv[:, :, 1], cmp_pos[1], cmp_w1[1], cmp_w2[1])
    n_cmp = kc.shape[1]
    cmp_pos_ids = positions[:, CMP_LEN - 1::CMP_STRIDE][:, :n_cmp]
    kc = rope(rms_norm(kc, nsa_qk_gain[1]), cmp_pos_ids)
    ks = rope(rms_norm(kv[:, :, 2], nsa_qk_gain[2]), positions)
    kw = rope(rms_norm(kv[:, :, 4], nsa_qk_gain[3]), positions)
    gates = jax.nn.sigmoid(ngate).reshape(b, s, 3, NSA_GROUPS, NSA_REP)
    ya = nsa_attention_pallas(qa, kc, vc, ks, kv[:, :, 3], kw, kv[:, :, 5], gates)

    qb = rope(rms_norm(dq.reshape(b, s, DSA_HEADS, HEAD_DIM), dsa_qk_gain[0]), positions)
    dkv = dkv.reshape(b, s, 2, HEAD_DIM)
    kb = rope(rms_norm(dkv[:, :, :1], dsa_qk_gain[1]), positions)[:, :, 0]
    vb = dkv[:, :, 1]
    iq = rope(iq.reshape(b, s, IDX_HEADS, IDX_DIM), positions)
    ik = rope(ik[:, :, None], positions)[:, :, 0]
    yb = dsa_attention_pallas(qb, kb, vb, iq, ik, iw)

    ga, gb = jnp.split(jax.nn.sigmoid(mg), 2, axis=-1)
    x = x + (ga * (ya @ w_branch_a) + gb * (yb @ w_branch_b)) @ w_out

    x = peer_pallas(x, ffn_norm, peer_wq, peer_sub_keys, peer_u, peer_v)

    gate = jax.nn.sigmoid(_rms(x).astype(x.dtype) @ ple_gate_w)
    return x + gate * rms_norm(p_i @ ple_w, ple_norm)


def kernel(x, p, positions, attn_norm, w_in, nsa_qk_gain, cmp_pos, cmp_w1, cmp_w2,
           dsa_qk_gain, w_branch_a, w_branch_b, w_out, ffn_norm, peer_wq, peer_sub_keys,
           peer_u, peer_v, ple_w, ple_gate_w, ple_norm):
    for i in range(DEPTH):
        x = hybrid_layer(x, p[i], positions, attn_norm[i], w_in[i], nsa_qk_gain[i], cmp_pos[i],
                         cmp_w1[i], cmp_w2[i], dsa_qk_gain[i], w_branch_a[i], w_branch_b[i],
                         w_out[i], ffn_norm[i], peer_wq[i], peer_sub_keys[i], peer_u[i],
                         peer_v[i], ple_w[i], ple_gate_w[i], ple_norm[i])
    return x
```

```python
import functools

import numpy as np
import jax
import jax.numpy as jnp
from jax import lax
from jax.experimental import pallas as pl
from jax.experimental.pallas import tpu as pltpu

D_MODEL = 1024
BATCH = 8
SEQ = 4096
DEPTH = 2

HEAD_DIM = 64
ROT_DIM = HEAD_DIM // 4
ROPE_THETA = 500000.0
Q_BLOCK = 128
NEG = -1e30
EPS = 1e-6

NSA_HEADS = 8
NSA_GROUPS = 2
NSA_REP = NSA_HEADS // NSA_GROUPS
CMP_LEN = 32
CMP_STRIDE = 16
CMP_HIDDEN = 256
SEL_LEN = 64
SEL_BLOCKS = 16
WINDOW = 512

DSA_HEADS = 8
IDX_HEADS = 8
IDX_DIM = 64
DSA_TOPK = 256

PEER_HEADS = 8
PEER_QDIM = 256
N_KEYS = 128
N_EXPERTS = N_KEYS * N_KEYS
PEER_HALF_TOPK = 16
PEER_TOPK = 16
PEER_TOKEN_BLOCK = 128

PLE_DIM = 256

MIX_A = NSA_HEADS * HEAD_DIM
MIX_B = DSA_HEADS * HEAD_DIM
IN_SIZES = (
    MIX_A,
    6 * NSA_GROUPS * HEAD_DIM,
    3 * NSA_HEADS,
    MIX_B,
    2 * HEAD_DIM,
    IDX_HEADS * IDX_DIM,
    IDX_DIM,
    IDX_HEADS,
    2 * D_MODEL,
)
IN_COLS = sum(IN_SIZES)

LANE = 128
SUBLANE = 8
VMEM_LIMIT = 48 * 1024 * 1024


def _norm_matmul_kernel(x_ref, g_ref, w_ref, o_ref):
    x = x_ref[...]
    h = x * lax.rsqrt(jnp.mean(x * x, axis=-1, keepdims=True) + EPS) * g_ref[...]
    o_ref[...] = jnp.dot(h.astype(jnp.bfloat16), w_ref[...],
                         preferred_element_type=jnp.float32)


def norm_matmul(x2d, g, w_bf16, *, tm=512, tn=512):
    m, k = x2d.shape
    n = w_bf16.shape[1]
    assert m % tm == 0 and n % tn == 0
    return pl.pallas_call(
        _norm_matmul_kernel,
        grid=(m // tm, n // tn),
        in_specs=[
            pl.BlockSpec((tm, k), lambda i, j: (i, 0)),
            pl.BlockSpec((1, k), lambda i, j: (0, 0)),
            pl.BlockSpec((k, tn), lambda i, j: (0, j)),
        ],
        out_specs=pl.BlockSpec((tm, tn), lambda i, j: (i, j)),
        out_shape=jax.ShapeDtypeStruct((m, n), jnp.float32),
        compiler_params=pltpu.CompilerParams(
            dimension_semantics=("parallel", "arbitrary"),
            vmem_limit_bytes=VMEM_LIMIT),
        name="norm_matmul",
    )(x2d, g.reshape(1, k), w_bf16)


SEG_NQ, SEG_DQ, SEG_IQ = 0, MIX_A, MIX_A + MIX_B
SEG_NKV = SEG_IQ + IDX_HEADS * IDX_DIM
SEG_DKV = SEG_NKV + 6 * NSA_GROUPS * HEAD_DIM
SEG_MISC = SEG_DKV + 2 * HEAD_DIM
PROJ_SMALL = SEG_MISC + LANE
PREP_TM = 256


def _in_proj_column_order():
    offs = np.concatenate([[0], np.cumsum(IN_SIZES)])
    seg = lambda i: np.arange(offs[i], offs[i + 1])
    nq, nkv, ngate, dq, dkv, iq, ik, iw, mg = (seg(i) for i in range(len(IN_SIZES)))
    small = np.concatenate([nq, dq, iq, nkv, dkv, ik, iw, ngate])
    return small, mg


def _rope_lane_constants():
    half = ROT_DIM // 2
    d = np.arange(LANE) % HEAD_DIM
    inv = np.where(d < ROT_DIM, ROPE_THETA ** (-(d % half) / half), 0.0)
    sign = np.where(d < half, -1.0, 1.0)
    return jnp.asarray(np.stack([inv, sign]), jnp.float32)


def _rope_tables(pos_col, rope_const):
    ang = pos_col.astype(jnp.float32) * rope_const[0:1, :]
    return jnp.cos(ang), jnp.sin(ang) * rope_const[1:2, :]


def _rope_apply(x, cos, sin):
    half = ROT_DIM // 2
    w = x.shape[-1]
    d = lax.broadcasted_iota(jnp.int32, (1, w), 1) % HEAD_DIM
    partner = jnp.where(d < half, pltpu.roll(x, w - half, 1), pltpu.roll(x, half, 1))
    return x * cos + partner * sin


def _head_sumsq(x, ones_bd):
    sq = x * x
    hi = sq.astype(jnp.bfloat16)
    lo = (sq - hi.astype(jnp.float32)).astype(jnp.bfloat16)
    w = x.shape[-1]
    bd = ones_bd[:w, :w]
    return (jnp.dot(hi, bd, preferred_element_type=jnp.float32)
            + jnp.dot(lo, bd, preferred_element_type=jnp.float32))


def _head_norm(x, gain, ones_bd):
    return x * lax.rsqrt(_head_sumsq(x, ones_bd) * (1.0 / HEAD_DIM) + EPS) * gain


def _prep_kernel(proj_ref, pos_ref, rc_ref, gq_ref, gdq_ref, gkv_ref, gkb_ref, bd_ref,
                 qa_ref, qb_ref, iq_ref, kvc_ref, ks_ref, vs_ref, kw_ref, vw_ref, kbvb_ref, misc_ref):
    bf16 = jnp.bfloat16
    bd = bd_ref[...]
    cos1, sin1 = _rope_tables(pos_ref[...], rc_ref[...])
    cos4, sin4 = jnp.tile(cos1, (1, 4)), jnp.tile(sin1, (1, 4))
    q_scale = HEAD_DIM ** -0.5

    nq = proj_ref[:, SEG_NQ:SEG_NQ + MIX_A]
    qa_ref[...] = (_rope_apply(_head_norm(nq, gq_ref[...], bd), cos4, sin4) * q_scale).astype(bf16)
    dq = proj_ref[:, SEG_DQ:SEG_DQ + MIX_B]
    qb_ref[...] = (_rope_apply(_head_norm(dq, gdq_ref[...], bd), cos4, sin4) * q_scale).astype(bf16)
    iq = proj_ref[:, SEG_IQ:SEG_IQ + IDX_HEADS * IDX_DIM]
    iq_ref[...] = (_rope_apply(iq, cos4, sin4) * IDX_DIM ** -0.5).astype(bf16)

    grp = NSA_GROUPS * HEAD_DIM
    kvc_ref[...] = proj_ref[:, SEG_NKV:SEG_NKV + 2 * grp]
    ks = proj_ref[:, SEG_NKV + 2 * grp:SEG_NKV + 3 * grp]
    ks_ref[...] = _rope_apply(_head_norm(ks, gkv_ref[0:1, :], bd), cos1, sin1).astype(bf16)
    vs_ref[...] = proj_ref[:, SEG_NKV + 3 * grp:SEG_NKV + 4 * grp].astype(bf16)
    kw = proj_ref[:, SEG_NKV + 4 * grp:SEG_NKV + 5 * grp]
    kw_ref[...] = _rope_apply(_head_norm(kw, gkv_ref[1:2, :], bd), cos1, sin1).astype(bf16)
    vw_ref[...] = proj_ref[:, SEG_NKV + 5 * grp:SEG_NKV + 6 * grp].astype(bf16)

    lane = lax.broadcasted_iota(jnp.int32, (1, LANE), 1)
    dkv = proj_ref[:, SEG_DKV:SEG_DKV + LANE]
    kb = _rope_apply(_head_norm(dkv, gkb_ref[...], bd), cos1, sin1)
    kbvb_ref[...] = jnp.where(lane < HEAD_DIM, kb, dkv).astype(bf16)

    misc = proj_ref[:, SEG_MISC:SEG_MISC + LANE]
    ik = _rope_apply(misc, cos1, sin1)
    misc_ref[...] = jnp.where(lane < IDX_DIM, ik,
                              jnp.where(lane < IDX_DIM + IDX_HEADS, misc * IDX_HEADS ** -0.5,
                                        jax.nn.sigmoid(misc)))


def prep_pallas(proj_small, pos_col, nsa_qk_gain, dsa_qk_gain):
    t = proj_small.shape[0]
    bf16 = jnp.bfloat16
    f32 = jnp.float32
    gq = jnp.tile(nsa_qk_gain[0], NSA_HEADS).reshape(1, MIX_A)
    gdq = jnp.tile(dsa_qk_gain[0], DSA_HEADS).reshape(1, MIX_B)
    gkv = jnp.stack([jnp.tile(nsa_qk_gain[2], NSA_GROUPS), jnp.tile(nsa_qk_gain[3], NSA_GROUPS)])
    gkb = jnp.tile(dsa_qk_gain[1], 2).reshape(1, LANE)
    head_of = np.arange(MIX_A) // HEAD_DIM
    bd = jnp.asarray(head_of[:, None] == head_of[None, :], bf16)
    rc = _rope_lane_constants()
    row = lambda w: pl.BlockSpec((PREP_TM, w), lambda i: (i, 0))
    full = lambda a: pl.BlockSpec(a.shape, lambda i: (0,) * a.ndim)
    widths = [MIX_A, MIX_B, IDX_HEADS * IDX_DIM, 2 * LANE, LANE, LANE, LANE, LANE, LANE, LANE]
    dtypes = [bf16, bf16, bf16, f32, bf16, bf16, bf16, bf16, bf16, f32]
    return pl.pallas_call(
        _prep_kernel,
        grid=(t // PREP_TM,),
        in_specs=[row(PROJ_SMALL), row(1), full(rc), full(gq), full(gdq), full(gkv), full(gkb),
                  full(bd)],
        out_specs=[row(w) for w in widths],
        out_shape=[jax.ShapeDtypeStruct((t, w), dt) for w, dt in zip(widths, dtypes)],
        compiler_params=pltpu.CompilerParams(
            dimension_semantics=("parallel",), vmem_limit_bytes=VMEM_LIMIT),
        name="proj_prep",
    )(proj_small, pos_col, rc, gq, gdq, gkv, gkb, bd)


def _compress_kernel(flat_ref, pe_ref, w1_ref, w2_ref, pos_ref, rc_ref, gain_ref, bd_ref, o_ref):
    bf16 = jnp.bfloat16
    f32 = jnp.float32
    outs = []
    for g in range(NSA_GROUPS):
        xin = (flat_ref[0, 0, g] + pe_ref[0]).astype(bf16)
        hid = jax.nn.gelu(jnp.dot(xin, w1_ref[0], preferred_element_type=f32))
        outs.append(jnp.dot(hid.astype(bf16), w2_ref[0], preferred_element_type=f32))
    out = jnp.concatenate(outs, axis=-1)

    @pl.when(pl.program_id(0) == 0)
    def _():
        cos, sin = _rope_tables(pos_ref[0], rc_ref[...])
        o_ref[0, 0] = _rope_apply(_head_norm(out, gain_ref[...], bd_ref[...]), cos, sin)

    @pl.when(pl.program_id(0) != 0)
    def _():
        o_ref[0, 0] = out


def compress_pallas(kvc, positions, cmp_pos, cmp_w1, cmp_w2, k_gain):
    b, s, _ = kvc.shape
    bf16 = jnp.bfloat16
    n_chunk = s // CMP_STRIDE
    n_cmp = n_chunk - CMP_LEN // CMP_STRIDE + 1
    n_pad = -(-n_cmp // LANE) * LANE
    c = kvc.reshape(b, n_chunk, CMP_STRIDE, 2, NSA_GROUPS, HEAD_DIM).transpose(3, 0, 4, 1, 2, 5)
    c = c.reshape(2, b, NSA_GROUPS, n_chunk, CMP_STRIDE * HEAD_DIM)
    flat = jnp.concatenate([c[..., j:j + n_cmp, :] for j in range(CMP_LEN // CMP_STRIDE)], axis=-1)
    flat = jnp.pad(flat, ((0, 0),) * 3 + ((0, n_pad - n_cmp), (0, 0)))
    pe = cmp_pos.reshape(2, 1, CMP_LEN * HEAD_DIM)
    pos_c = positions[:, CMP_LEN - 1::CMP_STRIDE][:, :n_cmp]
    pos_c = jnp.pad(pos_c, ((0, 0), (0, n_pad - n_cmp))).reshape(b, n_pad, 1)
    gain = jnp.tile(k_gain, NSA_GROUPS).reshape(1, LANE)
    head_of = np.arange(LANE) // HEAD_DIM
    bd = jnp.asarray(head_of[:, None] == head_of[None, :], bf16)
    kdim = CMP_LEN * HEAD_DIM
    return pl.pallas_call(
        _compress_kernel,
        grid=(2, b),
        in_specs=[
            pl.BlockSpec((1, 1, NSA_GROUPS, n_pad, kdim), lambda w, bi: (w, bi, 0, 0, 0)),
            pl.BlockSpec((1, 1, kdim), lambda w, bi: (w, 0, 0)),
            pl.BlockSpec((1, kdim, CMP_HIDDEN), lambda w, bi: (w, 0, 0)),
            pl.BlockSpec((1, CMP_HIDDEN, HEAD_DIM), lambda w, bi: (w, 0, 0)),
            pl.BlockSpec((1, n_pad, 1), lambda w, bi: (bi, 0, 0)),
            pl.BlockSpec((2, LANE), lambda w, bi: (0, 0)),
            pl.BlockSpec((1, LANE), lambda w, bi: (0, 0)),
            pl.BlockSpec((LANE, LANE), lambda w, bi: (0, 0)),
        ],
        out_specs=pl.BlockSpec((1, 1, n_pad, LANE), lambda w, bi: (w, bi, 0, 0)),
        out_shape=jax.ShapeDtypeStruct((2, b, n_pad, LANE), jnp.float32),
        compiler_params=pltpu.CompilerParams(
            dimension_semantics=("arbitrary", "arbitrary"), vmem_limit_bytes=VMEM_LIMIT),
        name="compress",
    )(flat, pe, cmp_w1.astype(bf16), cmp_w2.astype(bf16), pos_c, _rope_lane_constants(), gain, bd)


ROW_TM = 512


def _merge_kernel(x_ref, ya_ref, yb_ref, mg_ref, wa_ref, wb_ref, wo_ref, o_ref):
    f32 = jnp.float32
    a = jnp.dot(ya_ref[...], wa_ref[...], preferred_element_type=f32)
    b = jnp.dot(yb_ref[...], wb_ref[...], preferred_element_type=f32)
    g = jax.nn.sigmoid(mg_ref[...])
    mix = g[:, :D_MODEL] * a + g[:, D_MODEL:] * b
    o_ref[...] = x_ref[...] + jnp.dot(mix.astype(jnp.bfloat16), wo_ref[...],
                                      preferred_element_type=f32)


def merge_pallas(x2d, ya, yb, mg, wa, wb, wo):
    t, d = x2d.shape
    bf16 = jnp.bfloat16
    row = lambda w: pl.BlockSpec((ROW_TM, w), lambda i: (i, 0))
    full = lambda r, c: pl.BlockSpec((r, c), lambda i: (0, 0))
    return pl.pallas_call(
        _merge_kernel,
        grid=(t // ROW_TM,),
        in_specs=[row(d), row(MIX_A), row(MIX_B), row(2 * d),
                  full(MIX_A, d), full(MIX_B, d), full(d, d)],
        out_specs=row(d),
        out_shape=jax.ShapeDtypeStruct((t, d), jnp.float32),
        compiler_params=pltpu.CompilerParams(
            dimension_semantics=("parallel",), vmem_limit_bytes=VMEM_LIMIT),
        name="merge",
    )(x2d, ya, yb, mg, wa.astype(bf16), wb.astype(bf16), wo.astype(bf16))


def _ple_kernel(x_ref, p_ref, wg_ref, wp_ref, gn_ref, o_ref):
    f32 = jnp.float32
    bf16 = jnp.bfloat16
    x = x_ref[...]
    r = x * lax.rsqrt(jnp.mean(x * x, axis=-1, keepdims=True) + EPS)
    gate = jax.nn.sigmoid(jnp.dot(r.astype(bf16), wg_ref[...], preferred_element_type=f32))
    e = jnp.dot(p_ref[...].astype(bf16), wp_ref[...], preferred_element_type=f32)
    e = e * lax.rsqrt(jnp.mean(e * e, axis=-1, keepdims=True) + EPS) * gn_ref[...]
    o_ref[...] = x + gate * e


def ple_pallas(x2d, p2d, wg, wp, gn):
    t, d = x2d.shape
    bf16 = jnp.bfloat16
    row = lambda w: pl.BlockSpec((ROW_TM, w), lambda i: (i, 0))
    full = lambda r, c: pl.BlockSpec((r, c), lambda i: (0, 0))
    return pl.pallas_call(
        _ple_kernel,
        grid=(t // ROW_TM,),
        in_specs=[row(d), row(PLE_DIM), full(d, d), full(PLE_DIM, d), full(1, d)],
        out_specs=row(d),
        out_shape=jax.ShapeDtypeStruct((t, d), jnp.float32),
        compiler_params=pltpu.CompilerParams(
            dimension_semantics=("parallel",), vmem_limit_bytes=VMEM_LIMIT),
        name="ple",
    )(x2d, p2d, wg.astype(bf16), wp.astype(bf16), gn.reshape(1, d))


NSA_TQ = 128
NSA_TK = 512
NSA_TW = 128
NSA_WIN_TILES = (WINDOW + NSA_TQ) // NSA_TW
NSA_SUB = 128


def _flash_update_t(state, s, v_t, mask=None):
    m, l, acc = state
    m_new = jnp.maximum(m, jnp.max(s, axis=0, keepdims=True))
    alpha = jnp.exp(m - m_new)
    p = jnp.exp(s - m_new)
    if mask is not None:
        p = jnp.where(mask, p, 0.0)
    l = alpha * l + jnp.sum(p, axis=0, keepdims=True)
    acc = alpha * acc + jnp.dot(v_t, p.astype(jnp.bfloat16), preferred_element_type=jnp.float32)
    return m_new, l, acc


def _nsa_kernel(q_ref, kc_ref, vct_ref, ks_ref, vst_ref, kw_ref, vwt_ref, g_ref, ov_ref, exp_ref,
                o_ref, bias_ref, *, n_blk, n_top, n_ktiles):
    f32 = jnp.float32
    bf16 = jnp.bfloat16
    qi = pl.program_id(2)
    q0 = qi * NSA_TQ
    t_row = q0 + lax.broadcasted_iota(jnp.int32, (1, NSA_TQ), 1)

    n_cmp_pad = kc_ref.shape[-2]
    cmp_end = lax.broadcasted_iota(jnp.int32, (n_cmp_pad, 1), 0) * CMP_STRIDE + (CMP_LEN - 1)
    mask_c = cmp_end <= t_row
    kc = kc_ref[0, 0]
    vc_t = vct_ref[0, 0]
    p_sum = jnp.zeros((n_cmp_pad, NSA_TQ), f32)
    o_c = []
    for r in range(NSA_REP):
        s = jnp.dot(kc, q_ref[0, r], preferred_element_type=f32)
        s = jnp.where(mask_c, s, NEG)
        m = jnp.max(s, axis=0, keepdims=True)
        p = jnp.where(mask_c, jnp.exp(s - m), 0.0)
        l = jnp.sum(p, axis=0, keepdims=True)
        p = p / jnp.where(l > 0.0, l, 1.0)
        p_sum = p_sum + p
        o_c.append(jnp.dot(vc_t, p.astype(bf16), preferred_element_type=f32))

    p_hi = p_sum.astype(bf16)
    p_lo = (p_sum - p_hi.astype(f32)).astype(bf16)
    ov = ov_ref[...]
    imp_t = (jnp.dot(ov, p_hi, preferred_element_type=f32)
             + jnp.dot(ov, p_lo, preferred_element_type=f32))
    blk = lax.broadcasted_iota(jnp.int32, (n_blk, 1), 0)
    cur = t_row // SEL_LEN
    forced = (blk == 0) | (blk == cur) | (blk == cur - 1)
    admissible = blk * SEL_LEN <= t_row
    imp_t = jnp.where(forced, jnp.inf, imp_t)
    imp_t = jnp.where(admissible, imp_t, -jnp.inf)
    n_chunks = n_blk // SUBLANE
    chunks = [imp_t[c * SUBLANE:(c + 1) * SUBLANE, :] for c in range(n_chunks)]
    ranks = [jnp.zeros((SUBLANE, NSA_TQ), f32) for _ in range(n_chunks)]
    sub = lax.broadcasted_iota(jnp.int32, (SUBLANE, NSA_TQ), 0)
    for i in range(n_blk):
        ci, si = divmod(i, SUBLANE)
        row = jnp.broadcast_to(chunks[ci][si:si + 1, :], (SUBLANE, NSA_TQ))
        for c in range(n_chunks):
            if c > ci:
                beats = jnp.where(row >= chunks[c], 1.0, 0.0)
            elif c < ci:
                beats = jnp.where(row > chunks[c], 1.0, 0.0)
            else:
                tie = jnp.where(sub > si, 1.0, 0.0)
                beats = jnp.where(row > chunks[c], 1.0, jnp.where(row == chunks[c], tie, 0.0))
            ranks[c] = ranks[c] + beats
    rank = jnp.concatenate(ranks, axis=0)
    sel_t = jnp.where((rank < n_top) & admissible, 1.0, 0.0).astype(bf16)

    n_live = qi // (NSA_TK // NSA_TQ) + 1
    for j in range(n_ktiles):
        @pl.when(j < n_live)
        def _():
            hit = jnp.dot(exp_ref[j * NSA_TK:(j + 1) * NSA_TK, :], sel_t,
                          preferred_element_type=f32)
            kpos = j * NSA_TK + lax.broadcasted_iota(jnp.int32, (NSA_TK, 1), 0)
            bias_ref[j] = jnp.where((hit > 0.5) & (kpos <= t_row), 0.0, NEG)

    def init_state():
        return (jnp.full((1, NSA_TQ), NEG, f32), jnp.zeros((1, NSA_TQ), f32),
                jnp.zeros((HEAD_DIM, NSA_TQ), f32))

    def sel_body(j, states):
        states = list(states)
        for c in range(NSA_TK // NSA_SUB):
            rows = slice(c * NSA_SUB, (c + 1) * NSA_SUB)
            k = ks_ref[0, 0, j, rows, :]
            v_t = vst_ref[0, 0, j, :, rows]
            bias = bias_ref[j, rows, :]
            for r in range(NSA_REP):
                s = jnp.dot(k, q_ref[0, r], preferred_element_type=f32) + bias
                states[r] = _flash_update_t(states[r], s, v_t)
        return tuple(states)

    st_s = lax.fori_loop(0, n_live, sel_body, tuple(init_state() for _ in range(NSA_REP)))

    st_w = [init_state() for _ in range(NSA_REP)]
    for w in range(NSA_WIN_TILES):
        tile = qi - (NSA_WIN_TILES - 1) + w
        tix = jnp.maximum(tile, 0)
        k = kw_ref[0, 0, tix]
        v_t = vwt_ref[0, 0, tix]
        kpos = tile * NSA_TW + lax.broadcasted_iota(jnp.int32, (NSA_TW, 1), 0)
        diff = t_row - kpos
        mask = (kpos >= 0) & (diff >= 0) & (diff < WINDOW)
        for r in range(NSA_REP):
            s = jnp.where(mask, jnp.dot(k, q_ref[0, r], preferred_element_type=f32), NEG)
            st_w[r] = _flash_update_t(st_w[r], s, v_t, mask)

    g = g_ref[0, 0]
    for r in range(NSA_REP):
        o_s = st_s[r][2] / st_s[r][1]
        o_w = st_w[r][2] / st_w[r][1]
        o = (g[r:r + 1, :] * o_c[r] + g[NSA_REP + r:NSA_REP + r + 1, :] * o_s
             + g[2 * NSA_REP + r:2 * NSA_REP + r + 1, :] * o_w)
        o_ref[0, r] = o.astype(o_ref.dtype)


def nsa_attention_pallas(q, kc, vc, ks, vs, kw, vw, gates):
    b, s = q.shape[0], q.shape[1]
    bf16 = jnp.bfloat16
    n_cmp = s // CMP_STRIDE - CMP_LEN // CMP_STRIDE + 1
    n_cmp_pad = -(-n_cmp // LANE) * LANE
    n_blk = s // SEL_LEN
    n_top = min(SEL_BLOCKS, n_blk)
    n_kt = s // NSA_TK
    n_wt = s // NSA_TW
    assert s % NSA_TK == 0 and n_blk % SUBLANE == 0

    pad_c = ((0, 0), (0, n_cmp_pad - kc.shape[1]), (0, 0), (0, 0))
    kcp = jnp.pad(kc, pad_c).transpose(0, 2, 1, 3).astype(bf16)
    vct = jnp.pad(vc, pad_c).transpose(0, 2, 3, 1).astype(bf16)

    def key_tiles(k, tk):
        return k.reshape(b, s // tk, tk, NSA_GROUPS, HEAD_DIM).transpose(0, 3, 1, 2, 4).astype(bf16)

    def val_tiles(v, tk):
        return v.reshape(b, s // tk, tk, NSA_GROUPS, HEAD_DIM).transpose(0, 3, 1, 4, 2).astype(bf16)

    q_t = q.reshape(b, s, NSA_HEADS, HEAD_DIM).transpose(0, 2, 3, 1)
    gates_t = gates.transpose(0, 3, 2, 4, 1).reshape(b, NSA_GROUPS, 3 * NSA_REP, s)

    ci = np.arange(n_cmp_pad)[None, :] * CMP_STRIDE
    sj = np.arange(n_blk)[:, None] * SEL_LEN
    ov = (ci < sj + SEL_LEN) & (ci + CMP_LEN > sj) & (np.arange(n_cmp_pad)[None, :] < n_cmp)
    expand = (np.arange(s)[:, None] // SEL_LEN) == np.arange(n_blk)[None, :]

    per_bg = lambda bi, gi, i: (bi, gi, 0, 0)
    per_bg5 = lambda bi, gi, i: (bi, gi, 0, 0, 0)
    per_q = lambda bi, gi, i: (bi, gi, 0, i)
    out_t = pl.pallas_call(
        functools.partial(_nsa_kernel, n_blk=n_blk, n_top=n_top, n_ktiles=n_kt),
        grid=(b, NSA_GROUPS, s // NSA_TQ),
        in_specs=[
            pl.BlockSpec((1, NSA_REP, HEAD_DIM, NSA_TQ), per_q),
            pl.BlockSpec((1, 1, n_cmp_pad, HEAD_DIM), per_bg),
            pl.BlockSpec((1, 1, HEAD_DIM, n_cmp_pad), per_bg),
            pl.BlockSpec((1, 1, n_kt, NSA_TK, HEAD_DIM), per_bg5),
            pl.BlockSpec((1, 1, n_kt, HEAD_DIM, NSA_TK), per_bg5),
            pl.BlockSpec((1, 1, n_wt, NSA_TW, HEAD_DIM), per_bg5),
            pl.BlockSpec((1, 1, n_wt, HEAD_DIM, NSA_TW), per_bg5),
            pl.BlockSpec((1, 1, 3 * NSA_REP, NSA_TQ), per_q),
            pl.BlockSpec((n_blk, n_cmp_pad), lambda bi, gi, i: (0, 0)),
            pl.BlockSpec((s, n_blk), lambda bi, gi, i: (0, 0)),
        ],
        out_specs=pl.BlockSpec((1, NSA_REP, HEAD_DIM, NSA_TQ), per_q),
        out_shape=jax.ShapeDtypeStruct((b, NSA_HEADS, HEAD_DIM, s), bf16),
        scratch_shapes=[pltpu.VMEM((n_kt, NSA_TK, NSA_TQ), jnp.float32)],
        compiler_params=pltpu.CompilerParams(
            dimension_semantics=("parallel", "parallel", "arbitrary"),
            vmem_limit_bytes=VMEM_LIMIT),
        name="nsa_attention",
    )(q_t, kcp, vct, key_tiles(ks, NSA_TK), val_tiles(vs, NSA_TK), key_tiles(kw, NSA_TW),
      val_tiles(vw, NSA_TW), gates_t, jnp.asarray(ov, bf16), jnp.asarray(expand, bf16))
    return out_t.transpose(0, 3, 1, 2).reshape(b, s, MIX_A)


DSA_TQ = 128
DSA_TK = 512
DSA_SUB = 128
F32_ORDER_MASK = 0x7FFFFFFF
F32_NEG_INF_BITS = -8388608
F32_POS_INF_BITS = 0x7F800000


def _ordered_from_bits(bits):
    return jnp.where(bits >= 0, bits, bits ^ F32_ORDER_MASK)


def _sublane_group_sum(x):
    ways = 4
    g = x.reshape(ways, x.shape[0] // (SUBLANE * ways), SUBLANE, x.shape[1])
    return jnp.sum(jnp.sum(g, axis=1), axis=0)


def _dsa_kernel(iq_ref, w_ref, ik_ref, q_ref, k_ref, vt_ref, tril_ref, o_ref, sc_ref, *, n_keep):
    f32 = jnp.float32
    bf16 = jnp.bfloat16
    qi = pl.program_id(1)
    q0 = qi * DSA_TQ
    t_row = q0 + lax.broadcasted_iota(jnp.int32, (1, DSA_TQ), 1)
    n_live = qi // (DSA_TK // DSA_TQ) + 1
    key_off = lax.broadcasted_iota(jnp.int32, (DSA_TK, 1), 0)

    w = w_ref[0]

    def score_body(j, carry):
        ik = ik_ref[0, j]
        acc = jnp.zeros((DSA_TK, DSA_TQ), f32)
        for h in range(IDX_HEADS):
            logit = jnp.dot(ik, iq_ref[0, h], preferred_element_type=f32)
            acc = acc + jnp.maximum(logit, 0.0) * w[h:h + 1, :]
        causal = (j * DSA_TK + key_off) <= t_row
        sc_ref[j] = jnp.where(causal, acc + 0.0, -jnp.inf)
        return carry

    lax.fori_loop(0, n_live, score_body, 0)

    def count(pred):
        def body(j, acc):
            return acc + _sublane_group_sum(jnp.where(pred(sc_ref[j]), 1.0, 0.0))
        part = lax.fori_loop(0, n_live, body, jnp.zeros((SUBLANE, DSA_TQ), f32))
        return jnp.sum(part, axis=0, keepdims=True)

    def bisect_body(_, lohi):
        lo, hi = lohi
        mid = (lo >> 1) + (hi >> 1) + (lo & hi & 1)
        thr = lax.bitcast_convert_type(_ordered_from_bits(mid), f32)
        ok = count(lambda sc: sc >= thr) >= n_keep
        return jnp.where(ok, mid, lo), jnp.where(ok, hi, mid)

    lo0 = jnp.full((1, DSA_TQ), F32_NEG_INF_BITS ^ F32_ORDER_MASK, jnp.int32)
    hi0 = jnp.full((1, DSA_TQ), F32_POS_INF_BITS + 1, jnp.int32)
    lo, _ = lax.fori_loop(0, 32, bisect_body, (lo0, hi0))
    thr = lax.bitcast_convert_type(_ordered_from_bits(lo), f32)
    need = n_keep - count(lambda sc: sc > thr)

    tril = tril_ref[...]

    def mask_body(j, ties_before):
        sc = sc_ref[j]
        eq = sc == thr
        pref = ties_before + jnp.dot(tril, jnp.where(eq, 1.0, 0.0).astype(bf16),
                                     preferred_element_type=f32)
        keep_tie = jnp.where(pref <= need, 0.0, NEG)
        bias = jnp.where(sc > thr, 0.0, jnp.where(eq, keep_tie, NEG))
        causal = (j * DSA_TK + key_off) <= t_row
        sc_ref[j] = jnp.where(causal, bias, NEG)
        return pref[DSA_TK - 1:DSA_TK, :]

    lax.fori_loop(0, n_live, mask_body, jnp.zeros((1, DSA_TQ), f32))

    def att_body(j, states):
        states = list(states)
        for c in range(DSA_TK // DSA_SUB):
            rows = slice(c * DSA_SUB, (c + 1) * DSA_SUB)
            k = k_ref[0, j, rows, :]
            v_t = vt_ref[0, j, :, rows]
            bias = sc_ref[j, rows, :]
            for h in range(DSA_HEADS):
                s = jnp.dot(k, q_ref[0, h], preferred_element_type=f32) + bias
                states[h] = _flash_update_t(states[h], s, v_t)
        return tuple(states)

    init = tuple((jnp.full((1, DSA_TQ), NEG, f32), jnp.zeros((1, DSA_TQ), f32),
                  jnp.zeros((HEAD_DIM, DSA_TQ), f32)) for _ in range(DSA_HEADS))
    st = lax.fori_loop(0, n_live, att_body, init)
    for h in range(DSA_HEADS):
        o_ref[0, h] = (st[h][2] / st[h][1]).astype(o_ref.dtype)


def dsa_attention_pallas(q2, k, v, iq2, ik, w):
    b, s = q2.shape[0], q2.shape[1]
    bf16 = jnp.bfloat16
    n_keep = min(DSA_TOPK, s // 4)
    n_kt = s // DSA_TK
    assert s % DSA_TK == 0

    def heads_t(x, nh, dh):
        return x.reshape(b, s, nh, dh).transpose(0, 2, 3, 1)

    def key_tiles(x):
        return x.reshape(b, n_kt, DSA_TK, x.shape[-1]).astype(bf16)

    v_t = v.reshape(b, n_kt, DSA_TK, HEAD_DIM).transpose(0, 1, 3, 2).astype(bf16)
    tril = np.arange(DSA_TK)[:, None] >= np.arange(DSA_TK)[None, :]

    per_b = lambda bi, i: (bi, 0, 0, 0)
    per_q = lambda bi, i: (bi, 0, 0, i)
    out_t = pl.pallas_call(
        functools.partial(_dsa_kernel, n_keep=n_keep),
        grid=(b, s // DSA_TQ),
        in_specs=[
            pl.BlockSpec((1, IDX_HEADS, IDX_DIM, DSA_TQ), per_q),
            pl.BlockSpec((1, IDX_HEADS, DSA_TQ), lambda bi, i: (bi, 0, i)),
            pl.BlockSpec((1, n_kt, DSA_TK, IDX_DIM), per_b),
            pl.BlockSpec((1, DSA_HEADS, HEAD_DIM, DSA_TQ), per_q),
            pl.BlockSpec((1, n_kt, DSA_TK, HEAD_DIM), per_b),
            pl.BlockSpec((1, n_kt, HEAD_DIM, DSA_TK), per_b),
            pl.BlockSpec((DSA_TK, DSA_TK), lambda bi, i: (0, 0)),
        ],
        out_specs=pl.BlockSpec((1, DSA_HEADS, HEAD_DIM, DSA_TQ), per_q),
        out_shape=jax.ShapeDtypeStruct((b, DSA_HEADS, HEAD_DIM, s), bf16),
        scratch_shapes=[pltpu.VMEM((n_kt, DSA_TK, DSA_TQ), jnp.float32)],
        compiler_params=pltpu.CompilerParams(
            dimension_semantics=("parallel", "arbitrary"),
            vmem_limit_bytes=VMEM_LIMIT),
        name="dsa_attention",
    )(heads_t(iq2, IDX_HEADS, IDX_DIM), w.transpose(0, 2, 1), key_tiles(ik),
      heads_t(q2, DSA_HEADS, HEAD_DIM), key_tiles(k), v_t, jnp.asarray(tril, bf16))
    return out_t.transpose(0, 3, 1, 2).reshape(b, s, MIX_B)


PEER_TT = 128
PEER_SLOTS = PEER_HEADS * PEER_TOPK
PEER_CAND_A0 = PEER_HALF_TOPK
PEER_CAND_SQ = SUBLANE


def _peer_cand_flat_ids():
    ids = [0 * PEER_HALF_TOPK + bb for bb in range(PEER_CAND_A0)]
    for a in range(1, PEER_CAND_SQ):
        ids += [a * PEER_HALF_TOPK + bb for bb in range(PEER_CAND_SQ)]
    ids += [a * PEER_HALF_TOPK for a in range(PEER_CAND_SQ, PEER_HALF_TOPK)]
    return np.asarray(ids, np.int32)


def _extract_top(cur, ids, n):
    vals, picks = [], []
    for _ in range(n):
        m = jnp.max(cur, axis=0, keepdims=True)
        pick = jnp.min(jnp.where(cur == m, ids, jnp.int32(2 ** 30)), axis=0, keepdims=True)
        vals.append(m)
        picks.append(pick)
        cur = jnp.where(ids == pick, -jnp.inf, cur)
    return vals, picks


def _pair_grid(first, second, op):
    pieces = [op(first[0:1], second)]
    for a in range(1, PEER_CAND_SQ):
        pieces.append(op(first[a:a + 1], second[0:PEER_CAND_SQ]))
    pieces.append(op(first[PEER_CAND_SQ:], second[0:1]))
    return jnp.concatenate(pieces, axis=0)


def _peer_topk_kernel(x_ref, g_ref, wq_ref, sk_ref, fid_ref, h_ref, eidx_ref, gate_ref):
    f32 = jnp.float32
    bf16 = jnp.bfloat16
    x = x_ref[...]
    h = x * lax.rsqrt(jnp.mean(x * x, axis=-1, keepdims=True) + EPS) * g_ref[...]
    h_ref[...] = h
    q = jnp.dot(h.astype(bf16), wq_ref[...], preferred_element_type=f32).astype(bf16)
    key_ids = lax.broadcasted_iota(jnp.int32, (N_KEYS, PEER_TT), 0)
    fid = fid_ref[...]
    nt_dims = (((1,), (1,)), ((), ()))
    half = PEER_QDIM // 2
    for hd in range(PEER_HEADS):
        tops = []
        for c in range(2):
            col = (hd * 2 + c) * half
            s_t = lax.dot_general(sk_ref[c], q[:, col:col + half], nt_dims,
                                  preferred_element_type=f32)
            vals, picks = _extract_top(s_t, key_ids, PEER_HALF_TOPK)
            tops.append((jnp.concatenate(vals, axis=0), jnp.concatenate(picks, axis=0)))
        (v1, i1), (v2, i2) = tops
        cand = _pair_grid(v1, v2, lambda a, b: a + b)
        cexp = _pair_grid(i1, i2, lambda a, b: a * N_KEYS + b)
        vals, picks = _extract_top(cand, fid, PEER_TOPK)
        top = jnp.concatenate(vals, axis=0)
        eids = [jnp.max(jnp.where(fid == p, cexp, -1), axis=0, keepdims=True) for p in picks]
        ex = jnp.exp(top - top[0:1])
        gate = ex / jnp.sum(ex, axis=0, keepdims=True)
        eidx_ref[0, hd * PEER_TOPK:(hd + 1) * PEER_TOPK, :] = jnp.concatenate(eids, axis=0)
        gate_ref[0, hd * PEER_TOPK:(hd + 1) * PEER_TOPK, :] = gate


def peer_topk_pallas(x2d, ffn_norm, wq, sub_keys):
    t, d = x2d.shape
    n_tiles = t // PEER_TT
    fid = np.broadcast_to(_peer_cand_flat_ids()[:, None], (_peer_cand_flat_ids().shape[0], PEER_TT))
    n_cand = fid.shape[0]
    return pl.pallas_call(
        _peer_topk_kernel,
        grid=(n_tiles,),
        in_specs=[
            pl.BlockSpec((PEER_TT, d), lambda i: (i, 0)),
            pl.BlockSpec((1, d), lambda i: (0, 0)),
            pl.BlockSpec((d, PEER_HEADS * PEER_QDIM), lambda i: (0, 0)),
            pl.BlockSpec((2, N_KEYS, PEER_QDIM // 2), lambda i: (0, 0, 0)),
            pl.BlockSpec((n_cand, PEER_TT), lambda i: (0, 0)),
        ],
        out_specs=[
            pl.BlockSpec((PEER_TT, d), lambda i: (i, 0)),
            pl.BlockSpec((1, PEER_SLOTS, PEER_TT), lambda i: (i, 0, 0)),
            pl.BlockSpec((1, PEER_SLOTS, PEER_TT), lambda i: (i, 0, 0)),
        ],
        out_shape=[
            jax.ShapeDtypeStruct((t, d), jnp.float32),
            jax.ShapeDtypeStruct((n_tiles, PEER_SLOTS, PEER_TT), jnp.int32),
            jax.ShapeDtypeStruct((n_tiles, PEER_SLOTS, PEER_TT), jnp.float32),
        ],
        compiler_params=pltpu.CompilerParams(
            dimension_semantics=("parallel",), vmem_limit_bytes=VMEM_LIMIT),
        name="peer_topk",
    )(x2d, ffn_norm.reshape(1, d), wq.astype(jnp.bfloat16), sub_keys.astype(jnp.bfloat16),
      jnp.asarray(fid))


PEER_GT = 8
PEER_ROWS = PEER_GT * PEER_SLOTS
PEER_NSLOT = 2
BF16_HI_MASK = 0xFFFF0000


def _pack_bf16_pair(w):
    half = w.shape[1] // 2
    bits = lax.bitcast_convert_type(w.astype(jnp.bfloat16), jnp.uint16).astype(jnp.uint32)
    return bits[:, :half] | (bits[:, half:] << 16)


def _unpack_bf16_pair(words):
    lo = lax.bitcast_convert_type(words << 16, jnp.float32)
    hi = lax.bitcast_convert_type(words & jnp.uint32(BF16_HI_MASK), jnp.float32)
    return lo, hi


def _peer_eval_kernel(idx_ref, h_ref, gate_ref, x_ref, uv_hbm, o_ref, buf, sem):
    j = pl.program_id(0)
    n_blocks = pl.num_programs(0) - 1
    lane = lax.broadcasted_iota(jnp.int32, (PEER_SLOTS, PEER_TT), 1)
    lanes_per_block = PEER_NSLOT * PEER_GT
    lane0 = ((j - 1) % (PEER_TT // lanes_per_block)) * lanes_per_block
    half = D_MODEL // 2

    def evaluate(slot, tok):
        row0 = slot * PEER_GT + tok
        words = buf[slot, tok * PEER_SLOTS:(tok + 1) * PEER_SLOTS, :]
        u_lo, u_hi = _unpack_bf16_pair(words[:, :half])
        v_lo, v_hi = _unpack_bf16_pair(words[:, half:])
        h = h_ref[row0:row0 + 1, :]
        act = jnp.sum(u_lo * h[:, :half] + u_hi * h[:, half:], axis=-1, keepdims=True)
        gate = jnp.sum(jnp.where(lane == lane0 + row0, gate_ref[0], 0.0), axis=-1, keepdims=True)
        wgt = gate * jax.nn.gelu(act)
        o_ref[row0:row0 + 1, :half] = (x_ref[row0:row0 + 1, :half]
                                       + jnp.sum(wgt * v_lo, axis=0, keepdims=True))
        o_ref[row0:row0 + 1, half:] = (x_ref[row0:row0 + 1, half:]
                                       + jnp.sum(wgt * v_hi, axis=0, keepdims=True))

    def issue(slot, tok):
        for r in range(tok * PEER_SLOTS, (tok + 1) * PEER_SLOTS):
            e = idx_ref[0, 0, slot * PEER_ROWS + r]
            pltpu.make_async_copy(uv_hbm.at[e], buf.at[slot, pl.ds(r, 1)], sem.at[slot]).start()

    def step(do_evaluate, do_issue):
        for slot in range(PEER_NSLOT):
            if do_evaluate:
                pltpu.make_async_copy(uv_hbm.at[pl.ds(0, PEER_ROWS), 0], buf.at[slot],
                                      sem.at[slot]).wait()
            for tok in range(PEER_GT):
                if do_evaluate:
                    evaluate(slot, tok)
                if do_issue:
                    issue(slot, tok)

    @pl.when(j == 0)
    def _():
        step(False, True)

    @pl.when((j > 0) & (j < n_blocks))
    def _():
        step(True, True)

    @pl.when(j == n_blocks)
    def _():
        step(True, False)


def peer_eval_pallas(x2d, h2d, eidx_t, gate_t, u, v):
    t, d = x2d.shape
    blk = PEER_NSLOT * PEER_GT
    n_blocks = t // blk
    uv = jnp.concatenate([_pack_bf16_pair(u), _pack_bf16_pair(v)], axis=1).reshape(u.shape[0], 1, d)
    idx = eidx_t.transpose(0, 2, 1).reshape(n_blocks, 1, PEER_NSLOT * PEER_ROWS)
    per_tile = PEER_TT // blk
    prev = lambda j: jnp.maximum(j - 1, 0)
    return pl.pallas_call(
        _peer_eval_kernel,
        grid=(n_blocks + 1,),
        in_specs=[
            pl.BlockSpec((1, 1, PEER_NSLOT * PEER_ROWS),
                         lambda j: (jnp.minimum(j, n_blocks - 1), 0, 0), memory_space=pltpu.SMEM),
            pl.BlockSpec((blk, d), lambda j: (prev(j), 0)),
            pl.BlockSpec((1, PEER_SLOTS, PEER_TT), lambda j: (prev(j) // per_tile, 0, 0)),
            pl.BlockSpec((blk, d), lambda j: (prev(j), 0)),
            pl.BlockSpec(memory_space=pl.ANY),
        ],
        out_specs=pl.BlockSpec((blk, d), lambda j: (prev(j), 0)),
        out_shape=jax.ShapeDtypeStruct((t, d), jnp.float32),
        scratch_shapes=[pltpu.VMEM((PEER_NSLOT, PEER_ROWS, d), jnp.uint32),
                        pltpu.SemaphoreType.DMA((PEER_NSLOT,))],
        compiler_params=pltpu.CompilerParams(
            dimension_semantics=("arbitrary",), vmem_limit_bytes=VMEM_LIMIT),
        name="peer_eval",
    )(idx, h2d, gate_t, x2d, uv)


def peer_pallas(x, ffn_norm, wq, sub_keys, u, v):
    b, s, d = x.shape
    x2d = x.reshape(b * s, d)
    h2d, eidx_t, gate_t = peer_topk_pallas(x2d, ffn_norm, wq, sub_keys)
    return peer_eval_pallas(x2d, h2d, eidx_t, gate_t, u, v).reshape(b, s, d)


def _rms(x):
    xf = x.astype(jnp.float32)
    return xf * lax.rsqrt(jnp.mean(xf * xf, axis=-1, keepdims=True) + EPS)


def rms_norm(x, g):
    return (_rms(x) * g.astype(jnp.float32)).astype(x.dtype)


def rope(x, pos):
    half = ROT_DIM // 2
    inv = ROPE_THETA ** (-jnp.arange(half, dtype=jnp.float32) / half)
    ang = pos.astype(jnp.float32)[..., None] * inv
    cos = jnp.cos(ang)[:, :, None, :]
    sin = jnp.sin(ang)[:, :, None, :]
    xf = x.astype(jnp.float32)
    x1, x2, rest = xf[..., :half], xf[..., half:ROT_DIM], xf[..., ROT_DIM:]
    return jnp.concatenate([x1 * cos - x2 * sin, x2 * cos + x1 * sin, rest], axis=-1).astype(x.dtype)


def masked_softmax(s, mask):
    p = jax.nn.softmax(jnp.where(mask, s, NEG), axis=-1)
    return jnp.where(mask, p, 0.0)


def compress(t, pos_emb, w1, w2):
    b, s, g, d = t.shape
    r = CMP_LEN // CMP_STRIDE
    n_chunk = s // CMP_STRIDE
    n_cmp = n_chunk - r + 1
    c = t.reshape(b, n_chunk, CMP_STRIDE, g, d)
    blocks = jnp.concatenate([c[:, j:j + n_cmp] for j in range(r)], axis=2)
    blocks = blocks + pos_emb[None, None, :, None, :].astype(t.dtype)
    flat = blocks.transpose(0, 1, 3, 2, 4).reshape(b, n_cmp, g, CMP_LEN * d)
    return jax.nn.gelu(flat @ w1) @ w2


def nsa_attention(q, kc, vc, ks, vs, kw, vw, gates):
    b, s = q.shape[0], q.shape[1]
    n_cmp = kc.shape[1]
    n_sel_blocks = s // SEL_LEN
    n_top = min(SEL_BLOCKS, n_sel_blocks)
    scale = HEAD_DIM ** -0.5
    dt = q.dtype
    cmp_end = jnp.asarray(np.arange(n_cmp) * CMP_STRIDE + CMP_LEN - 1)
    ci = np.arange(n_cmp)[:, None] * CMP_STRIDE
    sj = np.arange(n_sel_blocks)[None, :] * SEL_LEN
    overlap = jnp.asarray(((ci < sj + SEL_LEN) & (ci + CMP_LEN > sj)).astype(np.float32))
    ks_blk = ks.reshape(b, n_sel_blocks, SEL_LEN, NSA_GROUPS, HEAD_DIM).transpose(0, 3, 1, 2, 4)
    vs_blk = vs.reshape(b, n_sel_blocks, SEL_LEN, NSA_GROUPS, HEAD_DIM).transpose(0, 3, 1, 2, 4)
    kw_pad = jnp.pad(kw, ((0, 0), (WINDOW, 0), (0, 0), (0, 0)))
    vw_pad = jnp.pad(vw, ((0, 0), (WINDOW, 0), (0, 0), (0, 0)))
    bix = jnp.arange(b)[:, None, None, None]
    gix = jnp.arange(NSA_GROUPS)[None, :, None, None]
    blk_ids = jnp.arange(n_sel_blocks)
    in_blk = jnp.arange(SEL_LEN)
    win_off = jnp.arange(WINDOW + Q_BLOCK) - WINDOW

    def block(bi):
        q0 = bi * Q_BLOCK
        t = q0 + jnp.arange(Q_BLOCK)
        qg = lax.dynamic_slice_in_dim(q, q0, Q_BLOCK, 1).reshape(b, Q_BLOCK, NSA_GROUPS, NSA_REP, HEAD_DIM)
        s_c = jnp.einsum('bqgrd,bngd->bgrqn', qg, kc).astype(jnp.float32) * scale
        p_c = masked_softmax(s_c, cmp_end[None, :] <= t[:, None])
        o_c = jnp.einsum('bgrqn,bngd->bqgrd', p_c.astype(dt), vc)
        imp = jnp.einsum('bgrqn,nj->bgqj', p_c, overlap)
        cur = (t // SEL_LEN)[:, None]
        forced = (blk_ids[None] == 0) | (blk_ids[None] == cur) | (blk_ids[None] == cur - 1)
        imp = jnp.where(forced, jnp.inf, imp)
        imp = jnp.where(blk_ids[None] * SEL_LEN <= t[:, None], imp, -jnp.inf)
        _, sel = lax.top_k(imp, n_top)
        k_g = ks_blk[bix, gix, sel].reshape(b, NSA_GROUPS, Q_BLOCK, n_top * SEL_LEN, HEAD_DIM)
        v_g = vs_blk[bix, gix, sel].reshape(b, NSA_GROUPS, Q_BLOCK, n_top * SEL_LEN, HEAD_DIM)
        key_pos = (sel[..., None] * SEL_LEN + in_blk).reshape(b, NSA_GROUPS, 1, Q_BLOCK, n_top * SEL_LEN)
        s_s = jnp.einsum('bqgrd,bgqmd->bgrqm', qg, k_g).astype(jnp.float32) * scale
        p_s = masked_softmax(s_s, key_pos <= t[:, None])
        o_s = jnp.einsum('bgrqm,bgqmd->bqgrd', p_s.astype(dt), v_g)
        k_w = lax.dynamic_slice_in_dim(kw_pad, q0, WINDOW + Q_BLOCK, 1)
        v_w = lax.dynamic_slice_in_dim(vw_pad, q0, WINDOW + Q_BLOCK, 1)
        kpos = q0 + win_off
        diff = t[:, None] - kpos[None, :]
        mask_w = (kpos[None, :] >= 0) & (diff >= 0) & (diff < WINDOW)
        s_w = jnp.einsum('bqgrd,bkgd->bgrqk', qg, k_w).astype(jnp.float32) * scale
        p_w = masked_softmax(s_w, mask_w)
        o_w = jnp.einsum('bgrqk,bkgd->bqgrd', p_w.astype(dt), v_w)
        g = lax.dynamic_slice_in_dim(gates, q0, Q_BLOCK, 1)[..., None]
        o = g[:, :, 0] * o_c + g[:, :, 1] * o_s + g[:, :, 2] * o_w
        return o.reshape(b, Q_BLOCK, MIX_A)

    out = lax.map(block, jnp.arange(s // Q_BLOCK))
    return out.transpose(1, 0, 2, 3).reshape(b, s, MIX_A)


def dsa_attention(q, k, v, iq, ik, iw):
    b, s = q.shape[0], q.shape[1]
    n_keep = min(DSA_TOPK, s // 4)
    scale = HEAD_DIM ** -0.5
    dt = q.dtype
    key_pos = jnp.arange(s)
    bix = jnp.arange(b)[:, None, None]

    def block(bi):
        q0 = bi * Q_BLOCK
        t = q0 + jnp.arange(Q_BLOCK)
        qb = lax.dynamic_slice_in_dim(q, q0, Q_BLOCK, 1)
        iqb = lax.dynamic_slice_in_dim(iq, q0, Q_BLOCK, 1)
        iwb = lax.dynamic_slice_in_dim(iw, q0, Q_BLOCK, 1).astype(jnp.float32) * IDX_HEADS ** -0.5
        logits = jnp.einsum('bqhd,bsd->bqhs', iqb, ik).astype(jnp.float32) * IDX_DIM ** -0.5
        score = jnp.einsum('bqhs,bqh->bqs', jax.nn.relu(logits), iwb)
        score = jnp.where(key_pos[None, None, :] <= t[None, :, None], score, -jnp.inf)
        _, sel = lax.top_k(score, n_keep)
        k_g = k[bix, sel]
        v_g = v[bix, sel]
        att = jnp.einsum('bqhd,bqkd->bhqk', qb, k_g).astype(jnp.float32) * scale
        p = masked_softmax(att, (sel <= t[None, :, None])[:, None])
        o = jnp.einsum('bhqk,bqkd->bqhd', p.astype(dt), v_g)
        return o.reshape(b, Q_BLOCK, MIX_B)

    out = lax.map(block, jnp.arange(s // Q_BLOCK))
    return out.transpose(1, 0, 2, 3).reshape(b, s, MIX_B)


def peer(h, wq, sub_keys, u, v):
    b, s, d = h.shape
    dt = h.dtype
    tok = h.reshape(b * s // PEER_TOKEN_BLOCK, PEER_TOKEN_BLOCK, d)

    def block(xb):
        q = (xb @ wq).reshape(PEER_TOKEN_BLOCK, PEER_HEADS, 2, PEER_QDIM // 2)
        s1 = jnp.einsum('thd,kd->thk', q[:, :, 0], sub_keys[0]).astype(jnp.float32)
        s2 = jnp.einsum('thd,kd->thk', q[:, :, 1], sub_keys[1]).astype(jnp.float32)
        v1, i1 = lax.top_k(s1, PEER_HALF_TOPK)
        v2, i2 = lax.top_k(s2, PEER_HALF_TOPK)
        cand = (v1[..., :, None] + v2[..., None, :]).reshape(PEER_TOKEN_BLOCK, PEER_HEADS, -1)
        cidx = (i1[..., :, None] * N_KEYS + i2[..., None, :]).reshape(PEER_TOKEN_BLOCK, PEER_HEADS, -1)
        top, pos = lax.top_k(cand, PEER_TOPK)
        eidx = jnp.take_along_axis(cidx, pos, axis=-1)
        gate = jax.nn.softmax(top, axis=-1)
        act = jax.nn.gelu(jnp.einsum('td,thkd->thk', xb, u[eidx]).astype(jnp.float32))
        return jnp.einsum('thk,thkd->td', (gate * act).astype(dt), v[eidx])

    return lax.map(block, tok).reshape(b, s, d)


def hybrid_layer(x, p_i, positions, attn_norm, w_in, nsa_qk_gain, cmp_pos, cmp_w1, cmp_w2,
                 dsa_qk_gain, w_branch_a, w_branch_b, w_out, ffn_norm, peer_wq, peer_sub_keys,
                 peer_u, peer_v, ple_w, ple_gate_w, ple_norm):
    b, s, d = x.shape
    t = b * s
    bf16 = jnp.bfloat16
    x2d = x.reshape(t, d)

    small_cols, mg_cols = _in_proj_column_order()
    w_small = jnp.pad(w_in[:, small_cols], ((0, 0), (0, PROJ_SMALL - small_cols.size))).astype(bf16)
    proj_small = norm_matmul(x2d, attn_norm, w_small, tn=PROJ_SMALL // 5)
    mg = norm_matmul(x2d, attn_norm, w_in[:, mg_cols].astype(bf16))
    qa, qb, iq, kvc, ks, vs, kw, vw, kbvb, misc = prep_pallas(
        proj_small, positions.reshape(t, 1), nsa_qk_gain, dsa_qk_gain)

    cmp = compress_pallas(kvc.reshape(b, s, -1), positions, cmp_pos, cmp_w1, cmp_w2, nsa_qk_gain[1])
    per_group = lambda a: a.reshape(b, -1, NSA_GROUPS, HEAD_DIM)
    gates = misc[:, IDX_DIM + IDX_HEADS:IDX_DIM + IDX_HEADS + 3 * NSA_HEADS]
    ya = nsa_attention_pallas(qa.reshape(b, s, MIX_A), per_group(cmp[0]), per_group(cmp[1]),
                              per_group(ks), per_group(vs), per_group(kw), per_group(vw),
                              gates.reshape(b, s, 3, NSA_GROUPS, NSA_REP))

    kbvb = kbvb.reshape(b, s, 2, HEAD_DIM)
    yb = dsa_attention_pallas(qb.reshape(b, s, MIX_B), kbvb[:, :, 0], kbvb[:, :, 1],
                              iq.reshape(b, s, -1), misc[:, :IDX_DIM].reshape(b, s, IDX_DIM),
                              misc[:, IDX_DIM:IDX_DIM + IDX_HEADS].reshape(b, s, IDX_HEADS))

    x2d = merge_pallas(x2d, ya.reshape(t, MIX_A), yb.reshape(t, MIX_B), mg,
                       w_branch_a, w_branch_b, w_out)
    x2d = peer_pallas(x2d.reshape(b, s, d), ffn_norm, peer_wq, peer_sub_keys, peer_u, peer_v)
    x2d = ple_pallas(x2d.reshape(t, d), p_i.reshape(t, PLE_DIM), ple_gate_w, ple_w, ple_norm)
    return x2d.reshape(b, s, d)


def kernel(x, p, positions, attn_norm, w_in, nsa_qk_gain, cmp_pos, cmp_w1, cmp_w2,
           dsa_qk_gain, w_branch_a, w_branch_b, w_out, ffn_norm, peer_wq, peer_sub_keys,
           peer_u, peer_v, ple_w, ple_gate_w, ple_norm):
    for i in range(DEPTH):
        x = hybrid_layer(x, p[i], positions, attn_norm[i], w_in[i], nsa_qk_gain[i], cmp_pos[i],
                         cmp_w1[i], cmp_w2[i], dsa_qk_gain[i], w_branch_a[i], w_branch_b[i],
                         w_out[i], ffn_norm[i], peer_wq[i], peer_sub_keys[i], peer_u[i],
                         peer_v[i], ple_w[i], ple_gate_w[i], ple_norm[i])
    return x
```

```python
import functools

import numpy as np
import jax
import jax.numpy as jnp
from jax import lax
from jax.experimental import pallas as pl
from jax.experimental.pallas import tpu as pltpu

D_MODEL = 1024
BATCH = 8
SEQ = 4096
DEPTH = 2

HEAD_DIM = 64
ROT_DIM = HEAD_DIM // 4
ROPE_THETA = 500000.0
Q_BLOCK = 128
NEG = -1e30
EPS = 1e-6

NSA_HEADS = 8
NSA_GROUPS = 2
NSA_REP = NSA_HEADS // NSA_GROUPS
CMP_LEN = 32
CMP_STRIDE = 16
CMP_HIDDEN = 256
SEL_LEN = 64
SEL_BLOCKS = 16
WINDOW = 512

DSA_HEADS = 8
IDX_HEADS = 8
IDX_DIM = 64
DSA_TOPK = 256

PEER_HEADS = 8
PEER_QDIM = 256
N_KEYS = 128
N_EXPERTS = N_KEYS * N_KEYS
PEER_HALF_TOPK = 16
PEER_TOPK = 16
PEER_TOKEN_BLOCK = 128

PLE_DIM = 256

MIX_A = NSA_HEADS * HEAD_DIM
MIX_B = DSA_HEADS * HEAD_DIM
IN_SIZES = (
    MIX_A,
    6 * NSA_GROUPS * HEAD_DIM,
    3 * NSA_HEADS,
    MIX_B,
    2 * HEAD_DIM,
    IDX_HEADS * IDX_DIM,
    IDX_DIM,
    IDX_HEADS,
    2 * D_MODEL,
)
IN_COLS = sum(IN_SIZES)

LANE = 128
SUBLANE = 8
VMEM_LIMIT = 48 * 1024 * 1024


def _norm_matmul_kernel(x_ref, g_ref, w_ref, o_ref):
    x = x_ref[...]
    h = x * lax.rsqrt(jnp.mean(x * x, axis=-1, keepdims=True) + EPS) * g_ref[...]
    o_ref[...] = jnp.dot(h.astype(jnp.bfloat16), w_ref[...],
                         preferred_element_type=jnp.float32)


def norm_matmul(x2d, g, w_bf16, *, tm=512, tn=512):
    m, k = x2d.shape
    n = w_bf16.shape[1]
    assert m % tm == 0 and n % tn == 0
    return pl.pallas_call(
        _norm_matmul_kernel,
        grid=(m // tm, n // tn),
        in_specs=[
            pl.BlockSpec((tm, k), lambda i, j: (i, 0)),
            pl.BlockSpec((1, k), lambda i, j: (0, 0)),
            pl.BlockSpec((k, tn), lambda i, j: (0, j)),
        ],
        out_specs=pl.BlockSpec((tm, tn), lambda i, j: (i, j)),
        out_shape=jax.ShapeDtypeStruct((m, n), jnp.float32),
        compiler_params=pltpu.CompilerParams(
            dimension_semantics=("parallel", "arbitrary"),
            vmem_limit_bytes=VMEM_LIMIT),
        name="norm_matmul",
    )(x2d, g.reshape(1, k), w_bf16)


SEG_NQ, SEG_DQ, SEG_IQ = 0, MIX_A, MIX_A + MIX_B
SEG_NKV = SEG_IQ + IDX_HEADS * IDX_DIM
SEG_DKV = SEG_NKV + 6 * NSA_GROUPS * HEAD_DIM
SEG_MISC = SEG_DKV + 2 * HEAD_DIM
PROJ_SMALL = SEG_MISC + LANE
PREP_TM = 256


def _in_proj_column_order():
    offs = np.concatenate([[0], np.cumsum(IN_SIZES)])
    seg = lambda i: np.arange(offs[i], offs[i + 1])
    nq, nkv, ngate, dq, dkv, iq, ik, iw, mg = (seg(i) for i in range(len(IN_SIZES)))
    small = np.concatenate([nq, dq, iq, nkv, dkv, ik, iw, ngate])
    return small, mg


def _rope_lane_constants():
    half = ROT_DIM // 2
    d = np.arange(LANE) % HEAD_DIM
    inv = np.where(d < ROT_DIM, ROPE_THETA ** (-(d % half) / half), 0.0)
    sign = np.where(d < half, -1.0, 1.0)
    return jnp.asarray(np.stack([inv, sign]), jnp.float32)


def _rope_tables(pos_col, rope_const):
    ang = pos_col.astype(jnp.float32) * rope_const[0:1, :]
    return jnp.cos(ang), jnp.sin(ang) * rope_const[1:2, :]


def _rope_apply(x, cos, sin):
    half = ROT_DIM // 2
    w = x.shape[-1]
    d = lax.broadcasted_iota(jnp.int32, (1, w), 1) % HEAD_DIM
    partner = jnp.where(d < half, pltpu.roll(x, w - half, 1), pltpu.roll(x, half, 1))
    return x * cos + partner * sin


def _head_sumsq(x, ones_bd):
    sq = x * x
    hi = sq.astype(jnp.bfloat16)
    lo = (sq - hi.astype(jnp.float32)).astype(jnp.bfloat16)
    w = x.shape[-1]
    bd = ones_bd[:w, :w]
    return (jnp.dot(hi, bd, preferred_element_type=jnp.float32)
            + jnp.dot(lo, bd, preferred_element_type=jnp.float32))


def _head_norm(x, gain, ones_bd):
    return x * lax.rsqrt(_head_sumsq(x, ones_bd) * (1.0 / HEAD_DIM) + EPS) * gain


def _prep_kernel(proj_ref, pos_ref, rc_ref, gq_ref, gdq_ref, gkv_ref, gkb_ref, bd_ref,
                 qa_ref, qb_ref, iq_ref, kvc_ref, ks_ref, vs_ref, kw_ref, vw_ref, kbvb_ref, misc_ref):
    bf16 = jnp.bfloat16
    bd = bd_ref[...]
    cos1, sin1 = _rope_tables(pos_ref[...], rc_ref[...])
    cos4, sin4 = jnp.tile(cos1, (1, 4)), jnp.tile(sin1, (1, 4))
    q_scale = HEAD_DIM ** -0.5

    nq = proj_ref[:, SEG_NQ:SEG_NQ + MIX_A]
    qa_ref[...] = (_rope_apply(_head_norm(nq, gq_ref[...], bd), cos4, sin4) * q_scale).astype(bf16)
    dq = proj_ref[:, SEG_DQ:SEG_DQ + MIX_B]
    qb_ref[...] = (_rope_apply(_head_norm(dq, gdq_ref[...], bd), cos4, sin4) * q_scale).astype(bf16)
    iq = proj_ref[:, SEG_IQ:SEG_IQ + IDX_HEADS * IDX_DIM]
    iq_ref[...] = (_rope_apply(iq, cos4, sin4) * IDX_DIM ** -0.5).astype(bf16)

    grp = NSA_GROUPS * HEAD_DIM
    kvc_ref[...] = proj_ref[:, SEG_NKV:SEG_NKV + 2 * grp]
    ks = proj_ref[:, SEG_NKV + 2 * grp:SEG_NKV + 3 * grp]
    ks_ref[...] = _rope_apply(_head_norm(ks, gkv_ref[0:1, :], bd), cos1, sin1).astype(bf16)
    vs_ref[...] = proj_ref[:, SEG_NKV + 3 * grp:SEG_NKV + 4 * grp].astype(bf16)
    kw = proj_ref[:, SEG_NKV + 4 * grp:SEG_NKV + 5 * grp]
    kw_ref[...] = _rope_apply(_head_norm(kw, gkv_ref[1:2, :], bd), cos1, sin1).astype(bf16)
    vw_ref[...] = proj_ref[:, SEG_NKV + 5 * grp:SEG_NKV + 6 * grp].astype(bf16)

    lane = lax.broadcasted_iota(jnp.int32, (1, LANE), 1)
    dkv = proj_ref[:, SEG_DKV:SEG_DKV + LANE]
    kb = _rope_apply(_head_norm(dkv, gkb_ref[...], bd), cos1, sin1)
    kbvb_ref[...] = jnp.where(lane < HEAD_DIM, kb, dkv).astype(bf16)

    misc = proj_ref[:, SEG_MISC:SEG_MISC + LANE]
    ik = _rope_apply(misc, cos1, sin1)
    misc_ref[...] = jnp.where(lane < IDX_DIM, ik,
                              jnp.where(lane < IDX_DIM + IDX_HEADS, misc * IDX_HEADS ** -0.5,
                                        jax.nn.sigmoid(misc)))


def prep_pallas(proj_small, pos_col, nsa_qk_gain, dsa_qk_gain):
    t = proj_small.shape[0]
    bf16 = jnp.bfloat16
    f32 = jnp.float32
    gq = jnp.tile(nsa_qk_gain[0], NSA_HEADS).reshape(1, MIX_A)
    gdq = jnp.tile(dsa_qk_gain[0], DSA_HEADS).reshape(1, MIX_B)
    gkv = jnp.stack([jnp.tile(nsa_qk_gain[2], NSA_GROUPS), jnp.tile(nsa_qk_gain[3], NSA_GROUPS)])
    gkb = jnp.tile(dsa_qk_gain[1], 2).reshape(1, LANE)
    head_of = np.arange(MIX_A) // HEAD_DIM
    bd = jnp.asarray(head_of[:, None] == head_of[None, :], bf16)
    rc = _rope_lane_constants()
    row = lambda w: pl.BlockSpec((PREP_TM, w), lambda i: (i, 0))
    full = lambda a: pl.BlockSpec(a.shape, lambda i: (0,) * a.ndim)
    widths = [MIX_A, MIX_B, IDX_HEADS * IDX_DIM, 2 * LANE, LANE, LANE, LANE, LANE, LANE, LANE]
    dtypes = [bf16, bf16, bf16, f32, bf16, bf16, bf16, bf16, bf16, f32]
    return pl.pallas_call(
        _prep_kernel,
        grid=(t // PREP_TM,),
        in_specs=[row(PROJ_SMALL), row(1), full(rc), full(gq), full(gdq), full(gkv), full(gkb),
                  full(bd)],
        out_specs=[row(w) for w in widths],
        out_shape=[jax.ShapeDtypeStruct((t, w), dt) for w, dt in zip(widths, dtypes)],
        compiler_params=pltpu.CompilerParams(
            dimension_semantics=("parallel",), vmem_limit_bytes=VMEM_LIMIT),
        name="proj_prep",
    )(proj_small, pos_col, rc, gq, gdq, gkv, gkb, bd)


def _compress_kernel(flat_ref, pe_ref, w1_ref, w2_ref, pos_ref, rc_ref, gain_ref, bd_ref, o_ref):
    bf16 = jnp.bfloat16
    f32 = jnp.float32
    outs = []
    for g in range(NSA_GROUPS):
        xin = (flat_ref[0, 0, g] + pe_ref[0]).astype(bf16)
        hid = jax.nn.gelu(jnp.dot(xin, w1_ref[0], preferred_element_type=f32))
        outs.append(jnp.dot(hid.astype(bf16), w2_ref[0], preferred_element_type=f32))
    out = jnp.concatenate(outs, axis=-1)

    @pl.when(pl.program_id(0) == 0)
    def _():
        cos, sin = _rope_tables(pos_ref[0], rc_ref[...])
        o_ref[0, 0] = _rope_apply(_head_norm(out, gain_ref[...], bd_ref[...]), cos, sin)

    @pl.when(pl.program_id(0) != 0)
    def _():
        o_ref[0, 0] = out


def compress_pallas(kvc, positions, cmp_pos, cmp_w1, cmp_w2, k_gain):
    b, s, _ = kvc.shape
    bf16 = jnp.bfloat16
    n_chunk = s // CMP_STRIDE
    n_cmp = n_chunk - CMP_LEN // CMP_STRIDE + 1
    n_pad = -(-n_cmp // LANE) * LANE
    c = kvc.reshape(b, n_chunk, CMP_STRIDE, 2, NSA_GROUPS, HEAD_DIM).transpose(3, 0, 4, 1, 2, 5)
    c = c.reshape(2, b, NSA_GROUPS, n_chunk, CMP_STRIDE * HEAD_DIM)
    flat = jnp.concatenate([c[..., j:j + n_cmp, :] for j in range(CMP_LEN // CMP_STRIDE)], axis=-1)
    flat = jnp.pad(flat, ((0, 0),) * 3 + ((0, n_pad - n_cmp), (0, 0)))
    pe = cmp_pos.reshape(2, 1, CMP_LEN * HEAD_DIM)
    pos_c = positions[:, CMP_LEN - 1::CMP_STRIDE][:, :n_cmp]
    pos_c = jnp.pad(pos_c, ((0, 0), (0, n_pad - n_cmp))).reshape(b, n_pad, 1)
    gain = jnp.tile(k_gain, NSA_GROUPS).reshape(1, LANE)
    head_of = np.arange(LANE) // HEAD_DIM
    bd = jnp.asarray(head_of[:, None] == head_of[None, :], bf16)
    kdim = CMP_LEN * HEAD_DIM
    return pl.pallas_call(
        _compress_kernel,
        grid=(2, b),
        in_specs=[
            pl.BlockSpec((1, 1, NSA_GROUPS, n_pad, kdim), lambda w, bi: (w, bi, 0, 0, 0)),
            pl.BlockSpec((1, 1, kdim), lambda w, bi: (w, 0, 0)),
            pl.BlockSpec((1, kdim, CMP_HIDDEN), lambda w, bi: (w, 0, 0)),
            pl.BlockSpec((1, CMP_HIDDEN, HEAD_DIM), lambda w, bi: (w, 0, 0)),
            pl.BlockSpec((1, n_pad, 1), lambda w, bi: (bi, 0, 0)),
            pl.BlockSpec((2, LANE), lambda w, bi: (0, 0)),
            pl.BlockSpec((1, LANE), lambda w, bi: (0, 0)),
            pl.BlockSpec((LANE, LANE), lambda w, bi: (0, 0)),
        ],
        out_specs=pl.BlockSpec((1, 1, n_pad, LANE), lambda w, bi: (w, bi, 0, 0)),
        out_shape=jax.ShapeDtypeStruct((2, b, n_pad, LANE), jnp.float32),
        compiler_params=pltpu.CompilerParams(
            dimension_semantics=("arbitrary", "arbitrary"), vmem_limit_bytes=VMEM_LIMIT),
        name="compress",
    )(flat, pe, cmp_w1.astype(bf16), cmp_w2.astype(bf16), pos_c, _rope_lane_constants(), gain, bd)


ROW_TM = 512


def _merge_kernel(x_ref, ya_ref, yb_ref, mg_ref, wa_ref, wb_ref, wo_ref, o_ref):
    f32 = jnp.float32
    a = jnp.dot(ya_ref[...], wa_ref[...], preferred_element_type=f32)
    b = jnp.dot(yb_ref[...], wb_ref[...], preferred_element_type=f32)
    g = jax.nn.sigmoid(mg_ref[...])
    mix = g[:, :D_MODEL] * a + g[:, D_MODEL:] * b
    o_ref[...] = x_ref[...] + jnp.dot(mix.astype(jnp.bfloat16), wo_ref[...],
                                      preferred_element_type=f32)


def merge_pallas(x2d, ya, yb, mg, wa, wb, wo):
    t, d = x2d.shape
    bf16 = jnp.bfloat16
    row = lambda w: pl.BlockSpec((ROW_TM, w), lambda i: (i, 0))
    full = lambda r, c: pl.BlockSpec((r, c), lambda i: (0, 0))
    return pl.pallas_call(
        _merge_kernel,
        grid=(t // ROW_TM,),
        in_specs=[row(d), row(MIX_A), row(MIX_B), row(2 * d),
                  full(MIX_A, d), full(MIX_B, d), full(d, d)],
        out_specs=row(d),
        out_shape=jax.ShapeDtypeStruct((t, d), jnp.float32),
        compiler_params=pltpu.CompilerParams(
            dimension_semantics=("parallel",), vmem_limit_bytes=VMEM_LIMIT),
        name="merge",
    )(x2d, ya, yb, mg, wa.astype(bf16), wb.astype(bf16), wo.astype(bf16))


def _ple_kernel(x_ref, p_ref, wg_ref, wp_ref, gn_ref, o_ref):
    f32 = jnp.float32
    bf16 = jnp.bfloat16
    x = x_ref[...]
    r = x * lax.rsqrt(jnp.mean(x * x, axis=-1, keepdims=True) + EPS)
    gate = jax.nn.sigmoid(jnp.dot(r.astype(bf16), wg_ref[...], preferred_element_type=f32))
    e = jnp.dot(p_ref[...].astype(bf16), wp_ref[...], preferred_element_type=f32)
    e = e * lax.rsqrt(jnp.mean(e * e, axis=-1, keepdims=True) + EPS) * gn_ref[...]
    o_ref[...] = x + gate * e


def ple_pallas(x2d, p2d, wg, wp, gn):
    t, d = x2d.shape
    bf16 = jnp.bfloat16
    row = lambda w: pl.BlockSpec((ROW_TM, w), lambda i: (i, 0))
    full = lambda r, c: pl.BlockSpec((r, c), lambda i: (0, 0))
    return pl.pallas_call(
        _ple_kernel,
        grid=(t // ROW_TM,),
        in_specs=[row(d), row(PLE_DIM), full(d, d), full(PLE_DIM, d), full(1, d)],
        out_specs=row(d),
        out_shape=jax.ShapeDtypeStruct((t, d), jnp.float32),
        compiler_params=pltpu.CompilerParams(
            dimension_semantics=("parallel",), vmem_limit_bytes=VMEM_LIMIT),
        name="ple",
    )(x2d, p2d, wg.astype(bf16), wp.astype(bf16), gn.reshape(1, d))


NSA_TQ = 128
NSA_TK = 512
NSA_TW = 128
NSA_WIN_TILES = (WINDOW + NSA_TQ) // NSA_TW
NSA_SUB = 128


def _flash_update_t(state, s, v_t, mask=None):
    m, l, acc = state
    m_new = jnp.maximum(m, jnp.max(s, axis=0, keepdims=True))
    alpha = jnp.exp(m - m_new)
    p = jnp.exp(s - m_new)
    if mask is not None:
        p = jnp.where(mask, p, 0.0)
    l = alpha * l + jnp.sum(p, axis=0, keepdims=True)
    acc = alpha * acc + jnp.dot(v_t, p.astype(jnp.bfloat16), preferred_element_type=jnp.float32)
    return m_new, l, acc


def _nsa_kernel(q_ref, kc_ref, vct_ref, ks_ref, vst_ref, kw_ref, vwt_ref, g_ref, ov_ref, exp_ref,
                o_ref, bias_ref, *, n_blk, n_top, n_ktiles):
    f32 = jnp.float32
    bf16 = jnp.bfloat16
    qi = pl.program_id(2)
    q0 = qi * NSA_TQ
    t_row = q0 + lax.broadcasted_iota(jnp.int32, (1, NSA_TQ), 1)

    n_cmp_pad = kc_ref.shape[-2]
    cmp_end = lax.broadcasted_iota(jnp.int32, (n_cmp_pad, 1), 0) * CMP_STRIDE + (CMP_LEN - 1)
    mask_c = cmp_end <= t_row
    kc = kc_ref[0, 0]
    vc_t = vct_ref[0, 0]
    p_sum = jnp.zeros((n_cmp_pad, NSA_TQ), f32)
    o_c = []
    for r in range(NSA_REP):
        s = jnp.dot(kc, q_ref[0, r], preferred_element_type=f32)
        s = jnp.where(mask_c, s, NEG)
        m = jnp.max(s, axis=0, keepdims=True)
        p = jnp.where(mask_c, jnp.exp(s - m), 0.0)
        l = jnp.sum(p, axis=0, keepdims=True)
        p = p / jnp.where(l > 0.0, l, 1.0)
        p_sum = p_sum + p
        o_c.append(jnp.dot(vc_t, p.astype(bf16), preferred_element_type=f32))

    p_hi = p_sum.astype(bf16)
    p_lo = (p_sum - p_hi.astype(f32)).astype(bf16)
    ov = ov_ref[...]
    imp_t = (jnp.dot(ov, p_hi, preferred_element_type=f32)
             + jnp.dot(ov, p_lo, preferred_element_type=f32))
    blk = lax.broadcasted_iota(jnp.int32, (n_blk, 1), 0)
    cur = t_row // SEL_LEN
    forced = (blk == 0) | (blk == cur) | (blk == cur - 1)
    admissible = blk * SEL_LEN <= t_row
    imp_t = jnp.where(forced, jnp.inf, imp_t)
    imp_t = jnp.where(admissible, imp_t, -jnp.inf)
    n_chunks = n_blk // SUBLANE
    chunks = [imp_t[c * SUBLANE:(c + 1) * SUBLANE, :] for c in range(n_chunks)]
    ranks = [jnp.zeros((SUBLANE, NSA_TQ), f32) for _ in range(n_chunks)]
    sub = lax.broadcasted_iota(jnp.int32, (SUBLANE, NSA_TQ), 0)
    for i in range(n_blk):
        ci, si = divmod(i, SUBLANE)
        row = jnp.broadcast_to(chunks[ci][si:si + 1, :], (SUBLANE, NSA_TQ))
        for c in range(n_chunks):
            if c > ci:
                beats = jnp.where(row >= chunks[c], 1.0, 0.0)
            elif c < ci:
                beats = jnp.where(row > chunks[c], 1.0, 0.0)
            else:
                tie = jnp.where(sub > si, 1.0, 0.0)
                beats = jnp.where(row > chunks[c], 1.0, jnp.where(row == chunks[c], tie, 0.0))
            ranks[c] = ranks[c] + beats
    rank = jnp.concatenate(ranks, axis=0)
    sel_t = jnp.where((rank < n_top) & admissible, 1.0, 0.0).astype(bf16)

    n_live = qi // (NSA_TK // NSA_TQ) + 1
    for j in range(n_ktiles):
        @pl.when(j < n_live)
        def _():
            hit = jnp.dot(exp_ref[j * NSA_TK:(j + 1) * NSA_TK, :], sel_t,
                          preferred_element_type=f32)
            kpos = j * NSA_TK + lax.broadcasted_iota(jnp.int32, (NSA_TK, 1), 0)
            bias_ref[j] = jnp.where((hit > 0.5) & (kpos <= t_row), 0.0, NEG)

    def init_state():
        return (jnp.full((1, NSA_TQ), NEG, f32), jnp.zeros((1, NSA_TQ), f32),
                jnp.zeros((HEAD_DIM, NSA_TQ), f32))

    def sel_body(j, states):
        states = list(states)
        for c in range(NSA_TK // NSA_SUB):
            rows = slice(c * NSA_SUB, (c + 1) * NSA_SUB)
            k = ks_ref[0, 0, j, rows, :]
            v_t = vst_ref[0, 0, j, :, rows]
            bias = bias_ref[j, rows, :]
            for r in range(NSA_REP):
                s = jnp.dot(k, q_ref[0, r], preferred_element_type=f32) + bias
                states[r] = _flash_update_t(states[r], s, v_t)
        return tuple(states)

    st_s = lax.fori_loop(0, n_live, sel_body, tuple(init_state() for _ in range(NSA_REP)))

    st_w = [init_state() for _ in range(NSA_REP)]
    for w in range(NSA_WIN_TILES):
        tile = qi - (NSA_WIN_TILES - 1) + w
        tix = jnp.maximum(tile, 0)
        k = kw_ref[0, 0, tix]
        v_t = vwt_ref[0, 0, tix]
        kpos = tile * NSA_TW + lax.broadcasted_iota(jnp.int32, (NSA_TW, 1), 0)
        diff = t_row - kpos
        mask = (kpos >= 0) & (diff >= 0) & (diff < WINDOW)
        for r in range(NSA_REP):
            s = jnp.where(mask, jnp.dot(k, q_ref[0, r], preferred_element_type=f32), NEG)
            st_w[r] = _flash_update_t(st_w[r], s, v_t, mask)

    g = g_ref[0, 0]
    for r in range(NSA_REP):
        o_s = st_s[r][2] / st_s[r][1]
        o_w = st_w[r][2] / st_w[r][1]
        o = (g[r:r + 1, :] * o_c[r] + g[NSA_REP + r:NSA_REP + r + 1, :] * o_s
             + g[2 * NSA_REP + r:2 * NSA_REP + r + 1, :] * o_w)
        o_ref[0, r] = o.astype(o_ref.dtype)


def nsa_attention_pallas(q, kc, vc, ks, vs, kw, vw, gates):
    b, s = q.shape[0], q.shape[1]
    bf16 = jnp.bfloat16
    n_cmp = s // CMP_STRIDE - CMP_LEN // CMP_STRIDE + 1
    n_cmp_pad = -(-n_cmp // LANE) * LANE
    n_blk = s // SEL_LEN
    n_top = min(SEL_BLOCKS, n_blk)
    n_kt = s // NSA_TK
    n_wt = s // NSA_TW
    assert s % NSA_TK == 0 and n_blk % SUBLANE == 0

    pad_c = ((0, 0), (0, n_cmp_pad - kc.shape[1]), (0, 0), (0, 0))
    kcp = jnp.pad(kc, pad_c).transpose(0, 2, 1, 3).astype(bf16)
    vct = jnp.pad(vc, pad_c).transpose(0, 2, 3, 1).astype(bf16)

    def key_tiles(k, tk):
        return k.reshape(b, s // tk, tk, NSA_GROUPS, HEAD_DIM).transpose(0, 3, 1, 2, 4).astype(bf16)

    def val_tiles(v, tk):
        return v.reshape(b, s // tk, tk, NSA_GROUPS, HEAD_DIM).transpose(0, 3, 1, 4, 2).astype(bf16)

    q_t = q.reshape(b, s, NSA_HEADS, HEAD_DIM).transpose(0, 2, 3, 1)
    gates_t = gates.transpose(0, 3, 2, 4, 1).reshape(b, NSA_GROUPS, 3 * NSA_REP, s)

    ci = np.arange(n_cmp_pad)[None, :] * CMP_STRIDE
    sj = np.arange(n_blk)[:, None] * SEL_LEN
    ov = (ci < sj + SEL_LEN) & (ci + CMP_LEN > sj) & (np.arange(n_cmp_pad)[None, :] < n_cmp)
    expand = (np.arange(s)[:, None] // SEL_LEN) == np.arange(n_blk)[None, :]

    per_bg = lambda bi, gi, i: (bi, gi, 0, 0)
    per_bg5 = lambda bi, gi, i: (bi, gi, 0, 0, 0)
    per_q = lambda bi, gi, i: (bi, gi, 0, i)
    out_t = pl.pallas_call(
        functools.partial(_nsa_kernel, n_blk=n_blk, n_top=n_top, n_ktiles=n_kt),
        grid=(b, NSA_GROUPS, s // NSA_TQ),
        in_specs=[
            pl.BlockSpec((1, NSA_REP, HEAD_DIM, NSA_TQ), per_q),
            pl.BlockSpec((1, 1, n_cmp_pad, HEAD_DIM), per_bg),
            pl.BlockSpec((1, 1, HEAD_DIM, n_cmp_pad), per_bg),
            pl.BlockSpec((1, 1, n_kt, NSA_TK, HEAD_DIM), per_bg5),
            pl.BlockSpec((1, 1, n_kt, HEAD_DIM, NSA_TK), per_bg5),
            pl.BlockSpec((1, 1, n_wt, NSA_TW, HEAD_DIM), per_bg5),
            pl.BlockSpec((1, 1, n_wt, HEAD_DIM, NSA_TW), per_bg5),
            pl.BlockSpec((1, 1, 3 * NSA_REP, NSA_TQ), per_q),
            pl.BlockSpec((n_blk, n_cmp_pad), lambda bi, gi, i: (0, 0)),
            pl.BlockSpec((s, n_blk), lambda bi, gi, i: (0, 0)),
        ],
        out_specs=pl.BlockSpec((1, NSA_REP, HEAD_DIM, NSA_TQ), per_q),
        out_shape=jax.ShapeDtypeStruct((b, NSA_HEADS, HEAD_DIM, s), bf16),
        scratch_shapes=[pltpu.VMEM((n_kt, NSA_TK, NSA_TQ), jnp.float32)],
        compiler_params=pltpu.CompilerParams(
            dimension_semantics=("parallel", "parallel", "arbitrary"),
            vmem_limit_bytes=VMEM_LIMIT),
        name="nsa_attention",
    )(q_t, kcp, vct, key_tiles(ks, NSA_TK), val_tiles(vs, NSA_TK), key_tiles(kw, NSA_TW),
      val_tiles(vw, NSA_TW), gates_t, jnp.asarray(ov, bf16), jnp.asarray(expand, bf16))
    return out_t.transpose(0, 3, 1, 2).reshape(b, s, MIX_A)


DSA_TQ = 128
DSA_TK = 512
DSA_SUB = 128
F32_ORDER_MASK = 0x7FFFFFFF
F32_NEG_INF_BITS = -8388608
F32_POS_INF_BITS = 0x7F800000


def _ordered_from_bits(bits):
    return jnp.where(bits >= 0, bits, bits ^ F32_ORDER_MASK)


def _sublane_group_sum(x):
    ways = 4
    g = x.reshape(ways, x.shape[0] // (SUBLANE * ways), SUBLANE, x.shape[1])
    return jnp.sum(jnp.sum(g, axis=1), axis=0)


def _dsa_kernel(iq_ref, w_ref, ik_ref, q_ref, k_ref, vt_ref, tril_ref, o_ref, sc_ref, *, n_keep):
    f32 = jnp.float32
    bf16 = jnp.bfloat16
    qi = pl.program_id(1)
    q0 = qi * DSA_TQ
    t_row = q0 + lax.broadcasted_iota(jnp.int32, (1, DSA_TQ), 1)
    n_live = qi // (DSA_TK // DSA_TQ) + 1
    key_off = lax.broadcasted_iota(jnp.int32, (DSA_TK, 1), 0)

    w = w_ref[0]

    def score_body(j, carry):
        ik = ik_ref[0, j]
        acc = jnp.zeros((DSA_TK, DSA_TQ), f32)
        for h in range(IDX_HEADS):
            logit = jnp.dot(ik, iq_ref[0, h], preferred_element_type=f32)
            acc = acc + jnp.maximum(logit, 0.0) * w[h:h + 1, :]
        causal = (j * DSA_TK + key_off) <= t_row
        sc_ref[j] = jnp.where(causal, acc + 0.0, -jnp.inf)
        return carry

    lax.fori_loop(0, n_live, score_body, 0)

    def count(pred):
        def body(j, acc):
            return acc + _sublane_group_sum(jnp.where(pred(sc_ref[j]), 1.0, 0.0))
        part = lax.fori_loop(0, n_live, body, jnp.zeros((SUBLANE, DSA_TQ), f32))
        return jnp.sum(part, axis=0, keepdims=True)

    def bisect_body(_, lohi):
        lo, hi = lohi
        mid = (lo >> 1) + (hi >> 1) + (lo & hi & 1)
        thr = lax.bitcast_convert_type(_ordered_from_bits(mid), f32)
        ok = count(lambda sc: sc >= thr) >= n_keep
        return jnp.where(ok, mid, lo), jnp.where(ok, hi, mid)

    lo0 = jnp.full((1, DSA_TQ), F32_NEG_INF_BITS ^ F32_ORDER_MASK, jnp.int32)
    hi0 = jnp.full((1, DSA_TQ), F32_POS_INF_BITS + 1, jnp.int32)
    lo, _ = lax.fori_loop(0, 32, bisect_body, (lo0, hi0))
    thr = lax.bitcast_convert_type(_ordered_from_bits(lo), f32)
    need = n_keep - count(lambda sc: sc > thr)

    tril = tril_ref[...]

    def mask_body(j, ties_before):
        sc = sc_ref[j]
        eq = sc == thr
        pref = ties_before + jnp.dot(tril, jnp.where(eq, 1.0, 0.0).astype(bf16),
                                     preferred_element_type=f32)
        keep_tie = jnp.where(pref <= need, 0.0, NEG)
        bias = jnp.where(sc > thr, 0.0, jnp.where(eq, keep_tie, NEG))
        causal = (j * DSA_TK + key_off) <= t_row
        sc_ref[j] = jnp.where(causal, bias, NEG)
        return pref[DSA_TK - 1:DSA_TK, :]

    lax.fori_loop(0, n_live, mask_body, jnp.zeros((1, DSA_TQ), f32))

    def att_body(j, states):
        states = list(states)
        for c in range(DSA_TK // DSA_SUB):
            rows = slice(c * DSA_SUB, (c + 1) * DSA_SUB)
            k = k_ref[0, j, rows, :]
            v_t = vt_ref[0, j, :, rows]
            bias = sc_ref[j, rows, :]
            for h in range(DSA_HEADS):
                s = jnp.dot(k, q_ref[0, h], preferred_element_type=f32) + bias
                states[h] = _flash_update_t(states[h], s, v_t)
        return tuple(states)

    init = tuple((jnp.full((1, DSA_TQ), NEG, f32), jnp.zeros((1, DSA_TQ), f32),
                  jnp.zeros((HEAD_DIM, DSA_TQ), f32)) for _ in range(DSA_HEADS))
    st = lax.fori_loop(0, n_live, att_body, init)
    for h in range(DSA_HEADS):
        o_ref[0, h] = (st[h][2] / st[h][1]).astype(o_ref.dtype)


def dsa_attention_pallas(q2, k, v, iq2, ik, w):
    b, s = q2.shape[0], q2.shape[1]
    bf16 = jnp.bfloat16
    n_keep = min(DSA_TOPK, s // 4)
    n_kt = s // DSA_TK
    assert s % DSA_TK == 0

    def heads_t(x, nh, dh):
        return x.reshape(b, s, nh, dh).transpose(0, 2, 3, 1)

    def key_tiles(x):
        return x.reshape(b, n_kt, DSA_TK, x.shape[-1]).astype(bf16)

    v_t = v.reshape(b, n_kt, DSA_TK, HEAD_DIM).transpose(0, 1, 3, 2).astype(bf16)
    tril = np.arange(DSA_TK)[:, None] >= np.arange(DSA_TK)[None, :]

    per_b = lambda bi, i: (bi, 0, 0, 0)
    per_q = lambda bi, i: (bi, 0, 0, i)
    out_t = pl.pallas_call(
        functools.partial(_dsa_kernel, n_keep=n_keep),
        grid=(b, s // DSA_TQ),
        in_specs=[
            pl.BlockSpec((1, IDX_HEADS, IDX_DIM, DSA_TQ), per_q),
            pl.BlockSpec((1, IDX_HEADS, DSA_TQ), lambda bi, i: (bi, 0, i)),
            pl.BlockSpec((1, n_kt, DSA_TK, IDX_DIM), per_b),
            pl.BlockSpec((1, DSA_HEADS, HEAD_DIM, DSA_TQ), per_q),
            pl.BlockSpec((1, n_kt, DSA_TK, HEAD_DIM), per_b),
            pl.BlockSpec((1, n_kt, HEAD_DIM, DSA_TK), per_b),
            pl.BlockSpec((DSA_TK, DSA_TK), lambda bi, i: (0, 0)),
        ],
        out_specs=pl.BlockSpec((1, DSA_HEADS, HEAD_DIM, DSA_TQ), per_q),
        out_shape=jax.ShapeDtypeStruct((b, DSA_HEADS, HEAD_DIM, s), bf16),
        scratch_shapes=[pltpu.VMEM((n_kt, DSA_TK, DSA_TQ), jnp.float32)],
        compiler_params=pltpu.CompilerParams(
            dimension_semantics=("parallel", "arbitrary"),
            vmem_limit_bytes=VMEM_LIMIT),
        name="dsa_attention",
    )(heads_t(iq2, IDX_HEADS, IDX_DIM), w.transpose(0, 2, 1), key_tiles(ik),
      heads_t(q2, DSA_HEADS, HEAD_DIM), key_tiles(k), v_t, jnp.asarray(tril, bf16))
    return out_t.transpose(0, 3, 1, 2).reshape(b, s, MIX_B)


PEER_TT = 128
PEER_SLOTS = PEER_HEADS * PEER_TOPK
PEER_CAND_A0 = PEER_HALF_TOPK
PEER_CAND_SQ = SUBLANE


def _peer_cand_flat_ids():
    ids = [0 * PEER_HALF_TOPK + bb for bb in range(PEER_CAND_A0)]
    for a in range(1, PEER_CAND_SQ):
        ids += [a * PEER_HALF_TOPK + bb for bb in range(PEER_CAND_SQ)]
    ids += [a * PEER_HALF_TOPK for a in range(PEER_CAND_SQ, PEER_HALF_TOPK)]
    return np.asarray(ids, np.int32)


def _extract_top(cur, ids, n):
    vals, picks = [], []
    for _ in range(n):
        m = jnp.max(cur, axis=0, keepdims=True)
        pick = jnp.min(jnp.where(cur == m, ids, jnp.int32(2 ** 30)), axis=0, keepdims=True)
        vals.append(m)
        picks.append(pick)
        cur = jnp.where(ids == pick, -jnp.inf, cur)
    return vals, picks


def _pair_grid(first, second, op):
    pieces = [op(first[0:1], second)]
    for a in range(1, PEER_CAND_SQ):
        pieces.append(op(first[a:a + 1], second[0:PEER_CAND_SQ]))
    pieces.append(op(first[PEER_CAND_SQ:], second[0:1]))
    return jnp.concatenate(pieces, axis=0)


def _peer_topk_kernel(x_ref, g_ref, wq_ref, sk_ref, fid_ref, h_ref, eidx_ref, gate_ref):
    f32 = jnp.float32
    bf16 = jnp.bfloat16
    x = x_ref[...]
    h = x * lax.rsqrt(jnp.mean(x * x, axis=-1, keepdims=True) + EPS) * g_ref[...]
    h_ref[...] = h
    q = jnp.dot(h.astype(bf16), wq_ref[...], preferred_element_type=f32).astype(bf16)
    key_ids = lax.broadcasted_iota(jnp.int32, (N_KEYS, PEER_TT), 0)
    fid = fid_ref[...]
    nt_dims = (((1,), (1,)), ((), ()))
    half = PEER_QDIM // 2
    for hd in range(PEER_HEADS):
        tops = []
        for c in range(2):
            col = (hd * 2 + c) * half
            s_t = lax.dot_general(sk_ref[c], q[:, col:col + half], nt_dims,
                                  preferred_element_type=f32)
            vals, picks = _extract_top(s_t, key_ids, PEER_HALF_TOPK)
            tops.append((jnp.concatenate(vals, axis=0), jnp.concatenate(picks, axis=0)))
        (v1, i1), (v2, i2) = tops
        cand = _pair_grid(v1, v2, lambda a, b: a + b)
        cexp = _pair_grid(i1, i2, lambda a, b: a * N_KEYS + b)
        vals, picks = _extract_top(cand, fid, PEER_TOPK)
        top = jnp.concatenate(vals, axis=0)
        eids = [jnp.max(jnp.where(fid == p, cexp, -1), axis=0, keepdims=True) for p in picks]
        ex = jnp.exp(top - top[0:1])
        gate = ex / jnp.sum(ex, axis=0, keepdims=True)
        eidx_ref[0, hd * PEER_TOPK:(hd + 1) * PEER_TOPK, :] = jnp.concatenate(eids, axis=0)
        gate_ref[0, hd * PEER_TOPK:(hd + 1) * PEER_TOPK, :] = gate


def peer_topk_pallas(x2d, ffn_norm, wq, sub_keys):
    t, d = x2d.shape
    n_tiles = t // PEER_TT
    fid = np.broadcast_to(_peer_cand_flat_ids()[:, None], (_peer_cand_flat_ids().shape[0], PEER_TT))
    n_cand = fid.shape[0]
    return pl.pallas_call(
        _peer_topk_kernel,
        grid=(n_tiles,),
        in_specs=[
            pl.BlockSpec((PEER_TT, d), lambda i: (i, 0)),
            pl.BlockSpec((1, d), lambda i: (0, 0)),
            pl.BlockSpec((d, PEER_HEADS * PEER_QDIM), lambda i: (0, 0)),
            pl.BlockSpec((2, N_KEYS, PEER_QDIM // 2), lambda i: (0, 0, 0)),
            pl.BlockSpec((n_cand, PEER_TT), lambda i: (0, 0)),
        ],
        out_specs=[
            pl.BlockSpec((PEER_TT, d), lambda i: (i, 0)),
            pl.BlockSpec((1, PEER_SLOTS, PEER_TT), lambda i: (i, 0, 0)),
            pl.BlockSpec((1, PEER_SLOTS, PEER_TT), lambda i: (i, 0, 0)),
        ],
        out_shape=[
            jax.ShapeDtypeStruct((t, d), jnp.float32),
            jax.ShapeDtypeStruct((n_tiles, PEER_SLOTS, PEER_TT), jnp.int32),
            jax.ShapeDtypeStruct((n_tiles, PEER_SLOTS, PEER_TT), jnp.float32),
        ],
        compiler_params=pltpu.CompilerParams(
            dimension_semantics=("parallel",), vmem_limit_bytes=VMEM_LIMIT),
        name="peer_topk",
    )(x2d, ffn_norm.reshape(1, d), wq.astype(jnp.bfloat16), sub_keys.astype(jnp.bfloat16),
      jnp.asarray(fid))


PEER_GT = 8
PEER_ROWS = PEER_GT * PEER_SLOTS
PEER_NSLOT = 2
PEER_DMA_QUEUES = 2
BF16_HI_MASK = 0xFFFF0000


def _pack_bf16_pair(w):
    half = w.shape[1] // 2
    bits = lax.bitcast_convert_type(w.astype(jnp.bfloat16), jnp.uint16).astype(jnp.uint32)
    return bits[:, :half] | (bits[:, half:] << 16)


def _unpack_bf16_pair(words):
    lo = lax.bitcast_convert_type(words << 16, jnp.float32)
    hi = lax.bitcast_convert_type(words & jnp.uint32(BF16_HI_MASK), jnp.float32)
    return lo, hi


def _peer_eval_kernel(idx_ref, h_ref, gate_ref, x_ref, uv_hbm, o_ref, buf, sem):
    j = pl.program_id(0)
    n_blocks = pl.num_programs(0) - 1
    lane = lax.broadcasted_iota(jnp.int32, (PEER_SLOTS, PEER_TT), 1)
    lanes_per_block = PEER_NSLOT * PEER_GT
    lane0 = ((j - 1) % (PEER_TT // lanes_per_block)) * lanes_per_block
    half = D_MODEL // 2

    def evaluate(slot, tok):
        row0 = slot * PEER_GT + tok
        words = buf[slot, tok * PEER_SLOTS:(tok + 1) * PEER_SLOTS, :]
        u_lo, u_hi = _unpack_bf16_pair(words[:, :half])
        v_lo, v_hi = _unpack_bf16_pair(words[:, half:])
        h = h_ref[row0:row0 + 1, :]
        act = jnp.sum(u_lo * h[:, :half] + u_hi * h[:, half:], axis=-1, keepdims=True)
        gate = jnp.sum(jnp.where(lane == lane0 + row0, gate_ref[0], 0.0), axis=-1, keepdims=True)
        wgt = gate * jax.nn.gelu(act)
        o_ref[row0:row0 + 1, :half] = (x_ref[row0:row0 + 1, :half]
                                       + jnp.sum(wgt * v_lo, axis=0, keepdims=True))
        o_ref[row0:row0 + 1, half:] = (x_ref[row0:row0 + 1, half:]
                                       + jnp.sum(wgt * v_hi, axis=0, keepdims=True))

    def issue(slot, tok):
        for r in range(tok * PEER_SLOTS, (tok + 1) * PEER_SLOTS):
            e = idx_ref[0, 0, slot * PEER_ROWS + r]
            pltpu.async_copy(uv_hbm.at[e], buf.at[slot, pl.ds(r, 1)], sem.at[slot],
                             priority=r % PEER_DMA_QUEUES)

    def step(do_evaluate, do_issue):
        for slot in range(PEER_NSLOT):
            if do_evaluate:
                pltpu.make_async_copy(uv_hbm.at[pl.ds(0, PEER_ROWS), 0], buf.at[slot],
                                      sem.at[slot]).wait()
            for tok in range(PEER_GT):
                if do_evaluate:
                    evaluate(slot, tok)
                if do_issue:
                    issue(slot, tok)

    @pl.when(j == 0)
    def _():
        step(False, True)

    @pl.when((j > 0) & (j < n_blocks))
    def _():
        step(True, True)

    @pl.when(j == n_blocks)
    def _():
        step(True, False)


def peer_eval_pallas(x2d, h2d, eidx_t, gate_t, u, v):
    t, d = x2d.shape
    blk = PEER_NSLOT * PEER_GT
    n_blocks = t // blk
    uv = jnp.concatenate([_pack_bf16_pair(u), _pack_bf16_pair(v)], axis=1).reshape(u.shape[0], 1, d)
    idx = eidx_t.transpose(0, 2, 1).reshape(n_blocks, 1, PEER_NSLOT * PEER_ROWS)
    per_tile = PEER_TT // blk
    prev = lambda j: jnp.maximum(j - 1, 0)
    return pl.pallas_call(
        _peer_eval_kernel,
        grid=(n_blocks + 1,),
        in_specs=[
            pl.BlockSpec((1, 1, PEER_NSLOT * PEER_ROWS),
                         lambda j: (jnp.minimum(j, n_blocks - 1), 0, 0), memory_space=pltpu.SMEM),
            pl.BlockSpec((blk, d), lambda j: (prev(j), 0)),
            pl.BlockSpec((1, PEER_SLOTS, PEER_TT), lambda j: (prev(j) // per_tile, 0, 0)),
            pl.BlockSpec((blk, d), lambda j: (prev(j), 0)),
            pl.BlockSpec(memory_space=pl.ANY),
        ],
        out_specs=pl.BlockSpec((blk, d), lambda j: (prev(j), 0)),
        out_shape=jax.ShapeDtypeStruct((t, d), jnp.float32),
        scratch_shapes=[pltpu.VMEM((PEER_NSLOT, PEER_ROWS, d), jnp.uint32),
                        pltpu.SemaphoreType.DMA((PEER_NSLOT,))],
        compiler_params=pltpu.CompilerParams(
            dimension_semantics=("arbitrary",), vmem_limit_bytes=VMEM_LIMIT),
        name="peer_eval",
    )(idx, h2d, gate_t, x2d, uv)


def peer_pallas(x, ffn_norm, wq, sub_keys, u, v):
    b, s, d = x.shape
    x2d = x.reshape(b * s, d)
    h2d, eidx_t, gate_t = peer_topk_pallas(x2d, ffn_norm, wq, sub_keys)
    return peer_eval_pallas(x2d, h2d, eidx_t, gate_t, u, v).reshape(b, s, d)


def _rms(x):
    xf = x.astype(jnp.float32)
    return xf * lax.rsqrt(jnp.mean(xf * xf, axis=-1, keepdims=True) + EPS)


def rms_norm(x, g):
    return (_rms(x) * g.astype(jnp.float32)).astype(x.dtype)


def rope(x, pos):
    half = ROT_DIM // 2
    inv = ROPE_THETA ** (-jnp.arange(half, dtype=jnp.float32) / half)
    ang = pos.astype(jnp.float32)[..., None] * inv
    cos = jnp.cos(ang)[:, :, None, :]
    sin = jnp.sin(ang)[:, :, None, :]
    xf = x.astype(jnp.float32)
    x1, x2, rest = xf[..., :half], xf[..., half:ROT_DIM], xf[..., ROT_DIM:]
    return jnp.concatenate([x1 * cos - x2 * sin, x2 * cos + x1 * sin, rest], axis=-1).astype(x.dtype)


def masked_softmax(s, mask):
    p = jax.nn.softmax(jnp.where(mask, s, NEG), axis=-1)
    return jnp.where(mask, p, 0.0)


def compress(t, pos_emb, w1, w2):
    b, s, g, d = t.shape
    r = CMP_LEN // CMP_STRIDE
    n_chunk = s // CMP_STRIDE
    n_cmp = n_chunk - r + 1
    c = t.reshape(b, n_chunk, CMP_STRIDE, g, d)
    blocks = jnp.concatenate([c[:, j:j + n_cmp] for j in range(r)], axis=2)
    blocks = blocks + pos_emb[None, None, :, None, :].astype(t.dtype)
    flat = blocks.transpose(0, 1, 3, 2, 4).reshape(b, n_cmp, g, CMP_LEN * d)
    return jax.nn.gelu(flat @ w1) @ w2


def nsa_attention(q, kc, vc, ks, vs, kw, vw, gates):
    b, s = q.shape[0], q.shape[1]
    n_cmp = kc.shape[1]
    n_sel_blocks = s // SEL_LEN
    n_top = min(SEL_BLOCKS, n_sel_blocks)
    scale = HEAD_DIM ** -0.5
    dt = q.dtype
    cmp_end = jnp.asarray(np.arange(n_cmp) * CMP_STRIDE + CMP_LEN - 1)
    ci = np.arange(n_cmp)[:, None] * CMP_STRIDE
    sj = np.arange(n_sel_blocks)[None, :] * SEL_LEN
    overlap = jnp.asarray(((ci < sj + SEL_LEN) & (ci + CMP_LEN > sj)).astype(np.float32))
    ks_blk = ks.reshape(b, n_sel_blocks, SEL_LEN, NSA_GROUPS, HEAD_DIM).transpose(0, 3, 1, 2, 4)
    vs_blk = vs.reshape(b, n_sel_blocks, SEL_LEN, NSA_GROUPS, HEAD_DIM).transpose(0, 3, 1, 2, 4)
    kw_pad = jnp.pad(kw, ((0, 0), (WINDOW, 0), (0, 0), (0, 0)))
    vw_pad = jnp.pad(vw, ((0, 0), (WINDOW, 0), (0, 0), (0, 0)))
    bix = jnp.arange(b)[:, None, None, None]
    gix = jnp.arange(NSA_GROUPS)[None, :, None, None]
    blk_ids = jnp.arange(n_sel_blocks)
    in_blk = jnp.arange(SEL_LEN)
    win_off = jnp.arange(WINDOW + Q_BLOCK) - WINDOW

    def block(bi):
        q0 = bi * Q_BLOCK
        t = q0 + jnp.arange(Q_BLOCK)
        qg = lax.dynamic_slice_in_dim(q, q0, Q_BLOCK, 1).reshape(b, Q_BLOCK, NSA_GROUPS, NSA_REP, HEAD_DIM)
        s_c = jnp.einsum('bqgrd,bngd->bgrqn', qg, kc).astype(jnp.float32) * scale
        p_c = masked_softmax(s_c, cmp_end[None, :] <= t[:, None])
        o_c = jnp.einsum('bgrqn,bngd->bqgrd', p_c.astype(dt), vc)
        imp = jnp.einsum('bgrqn,nj->bgqj', p_c, overlap)
        cur = (t // SEL_LEN)[:, None]
        forced = (blk_ids[None] == 0) | (blk_ids[None] == cur) | (blk_ids[None] == cur - 1)
        imp = jnp.where(forced, jnp.inf, imp)
        imp = jnp.where(blk_ids[None] * SEL_LEN <= t[:, None], imp, -jnp.inf)
        _, sel = lax.top_k(imp, n_top)
        k_g = ks_blk[bix, gix, sel].reshape(b, NSA_GROUPS, Q_BLOCK, n_top * SEL_LEN, HEAD_DIM)
        v_g = vs_blk[bix, gix, sel].reshape(b, NSA_GROUPS, Q_BLOCK, n_top * SEL_LEN, HEAD_DIM)
        key_pos = (sel[..., None] * SEL_LEN + in_blk).reshape(b, NSA_GROUPS, 1, Q_BLOCK, n_top * SEL_LEN)
        s_s = jnp.einsum('bqgrd,bgqmd->bgrqm', qg, k_g).astype(jnp.float32) * scale
        p_s = masked_softmax(s_s, key_pos <= t[:, None])
        o_s = jnp.einsum('bgrqm,bgqmd->bqgrd', p_s.astype(dt), v_g)
        k_w = lax.dynamic_slice_in_dim(kw_pad, q0, WINDOW + Q_BLOCK, 1)
        v_w = lax.dynamic_slice_in_dim(vw_pad, q0, WINDOW + Q_BLOCK, 1)
        kpos = q0 + win_off
        diff = t[:, None] - kpos[None, :]
        mask_w = (kpos[None, :] >= 0) & (diff >= 0) & (diff < WINDOW)
        s_w = jnp.einsum('bqgrd,bkgd->bgrqk', qg, k_w).astype(jnp.float32) * scale
        p_w = masked_softmax(s_w, mask_w)
        o_w = jnp.einsum('bgrqk,bkgd->bqgrd', p_w.astype(dt), v_w)
        g = lax.dynamic_slice_in_dim(gates, q0, Q_BLOCK, 1)[..., None]
        o = g[:, :, 0] * o_c + g[:, :, 1] * o_s + g[:, :, 2] * o_w
        return o.reshape(b, Q_BLOCK, MIX_A)

    out = lax.map(block, jnp.arange(s // Q_BLOCK))
    return out.transpose(1, 0, 2, 3).reshape(b, s, MIX_A)


def dsa_attention(q, k, v, iq, ik, iw):
    b, s = q.shape[0], q.shape[1]
    n_keep = min(DSA_TOPK, s // 4)
    scale = HEAD_DIM ** -0.5
    dt = q.dtype
    key_pos = jnp.arange(s)
    bix = jnp.arange(b)[:, None, None]

    def block(bi):
        q0 = bi * Q_BLOCK
        t = q0 + jnp.arange(Q_BLOCK)
        qb = lax.dynamic_slice_in_dim(q, q0, Q_BLOCK, 1)
        iqb = lax.dynamic_slice_in_dim(iq, q0, Q_BLOCK, 1)
        iwb = lax.dynamic_slice_in_dim(iw, q0, Q_BLOCK, 1).astype(jnp.float32) * IDX_HEADS ** -0.5
        logits = jnp.einsum('bqhd,bsd->bqhs', iqb, ik).astype(jnp.float32) * IDX_DIM ** -0.5
        score = jnp.einsum('bqhs,bqh->bqs', jax.nn.relu(logits), iwb)
        score = jnp.where(key_pos[None, None, :] <= t[None, :, None], score, -jnp.inf)
        _, sel = lax.top_k(score, n_keep)
        k_g = k[bix, sel]
        v_g = v[bix, sel]
        att = jnp.einsum('bqhd,bqkd->bhqk', qb, k_g).astype(jnp.float32) * scale
        p = masked_softmax(att, (sel <= t[None, :, None])[:, None])
        o = jnp.einsum('bhqk,bqkd->bqhd', p.astype(dt), v_g)
        return o.reshape(b, Q_BLOCK, MIX_B)

    out = lax.map(block, jnp.arange(s // Q_BLOCK))
    return out.transpose(1, 0, 2, 3).reshape(b, s, MIX_B)


def peer(h, wq, sub_keys, u, v):
    b, s, d = h.shape
    dt = h.dtype
    tok = h.reshape(b * s // PEER_TOKEN_BLOCK, PEER_TOKEN_BLOCK, d)

    def block(xb):
        q = (xb @ wq).reshape(PEER_TOKEN_BLOCK, PEER_HEADS, 2, PEER_QDIM // 2)
        s1 = jnp.einsum('thd,kd->thk', q[:, :, 0], sub_keys[0]).astype(jnp.float32)
        s2 = jnp.einsum('thd,kd->thk', q[:, :, 1], sub_keys[1]).astype(jnp.float32)
        v1, i1 = lax.top_k(s1, PEER_HALF_TOPK)
        v2, i2 = lax.top_k(s2, PEER_HALF_TOPK)
        cand = (v1[..., :, None] + v2[..., None, :]).reshape(PEER_TOKEN_BLOCK, PEER_HEADS, -1)
        cidx = (i1[..., :, None] * N_KEYS + i2[..., None, :]).reshape(PEER_TOKEN_BLOCK, PEER_HEADS, -1)
        top, pos = lax.top_k(cand, PEER_TOPK)
        eidx = jnp.take_along_axis(cidx, pos, axis=-1)
        gate = jax.nn.softmax(top, axis=-1)
        act = jax.nn.gelu(jnp.einsum('td,thkd->thk', xb, u[eidx]).astype(jnp.float32))
        return jnp.einsum('thk,thkd->td', (gate * act).astype(dt), v[eidx])

    return lax.map(block, tok).reshape(b, s, d)


def hybrid_layer(x, p_i, positions, attn_norm, w_in, nsa_qk_gain, cmp_pos, cmp_w1, cmp_w2,
                 dsa_qk_gain, w_branch_a, w_branch_b, w_out, ffn_norm, peer_wq, peer_sub_keys,
                 peer_u, peer_v, ple_w, ple_gate_w, ple_norm):
    b, s, d = x.shape
    t = b * s
    bf16 = jnp.bfloat16
    x2d = x.reshape(t, d)

    small_cols, mg_cols = _in_proj_column_order()
    w_small = jnp.pad(w_in[:, small_cols], ((0, 0), (0, PROJ_SMALL - small_cols.size))).astype(bf16)
    proj_small = norm_matmul(x2d, attn_norm, w_small, tn=PROJ_SMALL // 5)
    mg = norm_matmul(x2d, attn_norm, w_in[:, mg_cols].astype(bf16))
    qa, qb, iq, kvc, ks, vs, kw, vw, kbvb, misc = prep_pallas(
        proj_small, positions.reshape(t, 1), nsa_qk_gain, dsa_qk_gain)

    cmp = compress_pallas(kvc.reshape(b, s, -1), positions, cmp_pos, cmp_w1, cmp_w2, nsa_qk_gain[1])
    per_group = lambda a: a.reshape(b, -1, NSA_GROUPS, HEAD_DIM)
    gates = misc[:, IDX_DIM + IDX_HEADS:IDX_DIM + IDX_HEADS + 3 * NSA_HEADS]
    ya = nsa_attention_pallas(qa.reshape(b, s, MIX_A), per_group(cmp[0]), per_group(cmp[1]),
                              per_group(ks), per_group(vs), per_group(kw), per_group(vw),
                              gates.reshape(b, s, 3, NSA_GROUPS, NSA_REP))

    kbvb = kbvb.reshape(b, s, 2, HEAD_DIM)
    yb = dsa_attention_pallas(qb.reshape(b, s, MIX_B), kbvb[:, :, 0], kbvb[:, :, 1],
                              iq.reshape(b, s, -1), misc[:, :IDX_DIM].reshape(b, s, IDX_DIM),
                              misc[:, IDX_DIM:IDX_DIM + IDX_HEADS].reshape(b, s, IDX_HEADS))

    x2d = merge_pallas(x2d, ya.reshape(t, MIX_A), yb.reshape(t, MIX_B), mg,
                       w_branch_a, w_branch_b, w_out)
    x2d = peer_pallas(x2d.reshape(b, s, d), ffn_norm, peer_wq, peer_sub_keys, peer_u, peer_v)
    x2d = ple_pallas(x2d.reshape(t, d), p_i.reshape(t, PLE_DIM), ple_gate_w, ple_w, ple_norm)
    return x2d.reshape(b, s, d)


def kernel(x, p, positions, attn_norm, w_in, nsa_qk_gain, cmp_pos, cmp_w1, cmp_w2,
           dsa_qk_gain, w_branch_a, w_branch_b, w_out, ffn_norm, peer_wq, peer_sub_keys,
           peer_u, peer_v, ple_w, ple_gate_w, ple_norm):
    for i in range(DEPTH):
        x = hybrid_layer(x, p[i], positions, attn_norm[i], w_in[i], nsa_qk_gain[i], cmp_pos[i],
                         cmp_w1[i], cmp_w2[i], dsa_qk_gain[i], w_branch_a[i], w_branch_b[i],
                         w_out[i], ffn_norm[i], peer_wq[i], peer_sub_keys[i], peer_u[i],
                         peer_v[i], ple_w[i], ple_gate_w[i], ple_norm[i])
    return x
```

```python
import functools

import numpy as np
import jax
import jax.numpy as jnp
from jax import lax
from jax.experimental import pallas as pl
from jax.experimental.pallas import tpu as pltpu

D_MODEL = 1024
BATCH = 8
SEQ = 4096
DEPTH = 2

HEAD_DIM = 64
ROT_DIM = HEAD_DIM // 4
ROPE_THETA = 500000.0
Q_BLOCK = 128
NEG = -1e30
EPS = 1e-6

NSA_HEADS = 8
NSA_GROUPS = 2
NSA_REP = NSA_HEADS // NSA_GROUPS
CMP_LEN = 32
CMP_STRIDE = 16
CMP_HIDDEN = 256
SEL_LEN = 64
SEL_BLOCKS = 16
WINDOW = 512

DSA_HEADS = 8
IDX_HEADS = 8
IDX_DIM = 64
DSA_TOPK = 256

PEER_HEADS = 8
PEER_QDIM = 256
N_KEYS = 128
N_EXPERTS = N_KEYS * N_KEYS
PEER_HALF_TOPK = 16
PEER_TOPK = 16
PEER_TOKEN_BLOCK = 128

PLE_DIM = 256

MIX_A = NSA_HEADS * HEAD_DIM
MIX_B = DSA_HEADS * HEAD_DIM
IN_SIZES = (
    MIX_A,
    6 * NSA_GROUPS * HEAD_DIM,
    3 * NSA_HEADS,
    MIX_B,
    2 * HEAD_DIM,
    IDX_HEADS * IDX_DIM,
    IDX_DIM,
    IDX_HEADS,
    2 * D_MODEL,
)
IN_COLS = sum(IN_SIZES)

LANE = 128
SUBLANE = 8
VMEM_LIMIT = 48 * 1024 * 1024


def _norm_matmul_kernel(x_ref, g_ref, w_ref, o_ref):
    x = x_ref[...]
    h = x * lax.rsqrt(jnp.mean(x * x, axis=-1, keepdims=True) + EPS) * g_ref[...]
    o_ref[...] = jnp.dot(h.astype(jnp.bfloat16), w_ref[...],
                         preferred_element_type=jnp.float32)


def norm_matmul(x2d, g, w_bf16, *, tm=512, tn=512):
    m, k = x2d.shape
    n = w_bf16.shape[1]
    assert m % tm == 0 and n % tn == 0
    return pl.pallas_call(
        _norm_matmul_kernel,
        grid=(m // tm, n // tn),
        in_specs=[
            pl.BlockSpec((tm, k), lambda i, j: (i, 0)),
            pl.BlockSpec((1, k), lambda i, j: (0, 0)),
            pl.BlockSpec((k, tn), lambda i, j: (0, j)),
        ],
        out_specs=pl.BlockSpec((tm, tn), lambda i, j: (i, j)),
        out_shape=jax.ShapeDtypeStruct((m, n), jnp.float32),
        compiler_params=pltpu.CompilerParams(
            dimension_semantics=("parallel", "arbitrary"),
            vmem_limit_bytes=VMEM_LIMIT),
        name="norm_matmul",
    )(x2d, g.reshape(1, k), w_bf16)


SEG_NQ, SEG_DQ, SEG_IQ = 0, MIX_A, MIX_A + MIX_B
SEG_NKV = SEG_IQ + IDX_HEADS * IDX_DIM
SEG_DKV = SEG_NKV + 6 * NSA_GROUPS * HEAD_DIM
SEG_MISC = SEG_DKV + 2 * HEAD_DIM
PROJ_SMALL = SEG_MISC + LANE
PREP_TM = 256


def _in_proj_column_order():
    offs = np.concatenate([[0], np.cumsum(IN_SIZES)])
    seg = lambda i: np.arange(offs[i], offs[i + 1])
    nq, nkv, ngate, dq, dkv, iq, ik, iw, mg = (seg(i) for i in range(len(IN_SIZES)))
    small = np.concatenate([nq, dq, iq, nkv, dkv, ik, iw, ngate])
    return small, mg


def _rope_lane_constants():
    half = ROT_DIM // 2
    d = np.arange(LANE) % HEAD_DIM
    inv = np.where(d < ROT_DIM, ROPE_THETA ** (-(d % half) / half), 0.0)
    sign = np.where(d < half, -1.0, 1.0)
    return jnp.asarray(np.stack([inv, sign]), jnp.float32)


def _rope_tables(pos_col, rope_const):
    ang = pos_col.astype(jnp.float32) * rope_const[0:1, :]
    return jnp.cos(ang), jnp.sin(ang) * rope_const[1:2, :]


def _rope_apply(x, cos, sin):
    half = ROT_DIM // 2
    w = x.shape[-1]
    d = lax.broadcasted_iota(jnp.int32, (1, w), 1) % HEAD_DIM
    partner = jnp.where(d < half, pltpu.roll(x, w - half, 1), pltpu.roll(x, half, 1))
    return x * cos + partner * sin


def _head_sumsq(x, ones_bd):
    sq = x * x
    hi = sq.astype(jnp.bfloat16)
    lo = (sq - hi.astype(jnp.float32)).astype(jnp.bfloat16)
    w = x.shape[-1]
    bd = ones_bd[:w, :w]
    return (jnp.dot(hi, bd, preferred_element_type=jnp.float32)
            + jnp.dot(lo, bd, preferred_element_type=jnp.float32))


def _head_norm(x, gain, ones_bd):
    return x * lax.rsqrt(_head_sumsq(x, ones_bd) * (1.0 / HEAD_DIM) + EPS) * gain


def _prep_kernel(proj_ref, pos_ref, rc_ref, gq_ref, gdq_ref, gkv_ref, gkb_ref, bd_ref,
                 qa_ref, qb_ref, iq_ref, kvc_ref, ks_ref, vs_ref, kw_ref, vw_ref, kbvb_ref, misc_ref):
    bf16 = jnp.bfloat16
    bd = bd_ref[...]
    cos1, sin1 = _rope_tables(pos_ref[...], rc_ref[...])
    cos4, sin4 = jnp.tile(cos1, (1, 4)), jnp.tile(sin1, (1, 4))
    q_scale = HEAD_DIM ** -0.5

    nq = proj_ref[:, SEG_NQ:SEG_NQ + MIX_A]
    qa_ref[...] = (_rope_apply(_head_norm(nq, gq_ref[...], bd), cos4, sin4) * q_scale).astype(bf16)
    dq = proj_ref[:, SEG_DQ:SEG_DQ + MIX_B]
    qb_ref[...] = (_rope_apply(_head_norm(dq, gdq_ref[...], bd), cos4, sin4) * q_scale).astype(bf16)
    iq = proj_ref[:, SEG_IQ:SEG_IQ + IDX_HEADS * IDX_DIM]
    iq_ref[...] = (_rope_apply(iq, cos4, sin4) * IDX_DIM ** -0.5).astype(bf16)

    grp = NSA_GROUPS * HEAD_DIM
    kvc_ref[...] = proj_ref[:, SEG_NKV:SEG_NKV + 2 * grp]
    ks = proj_ref[:, SEG_NKV + 2 * grp:SEG_NKV + 3 * grp]
    ks_ref[...] = _rope_apply(_head_norm(ks, gkv_ref[0:1, :], bd), cos1, sin1).astype(bf16)
    vs_ref[...] = proj_ref[:, SEG_NKV + 3 * grp:SEG_NKV + 4 * grp].astype(bf16)
    kw = proj_ref[:, SEG_NKV + 4 * grp:SEG_NKV + 5 * grp]
    kw_ref[...] = _rope_apply(_head_norm(kw, gkv_ref[1:2, :], bd), cos1, sin1).astype(bf16)
    vw_ref[...] = proj_ref[:, SEG_NKV + 5 * grp:SEG_NKV + 6 * grp].astype(bf16)

    lane = lax.broadcasted_iota(jnp.int32, (1, LANE), 1)
    dkv = proj_ref[:, SEG_DKV:SEG_DKV + LANE]
    kb = _rope_apply(_head_norm(dkv, gkb_ref[...], bd), cos1, sin1)
    kbvb_ref[...] = jnp.where(lane < HEAD_DIM, kb, dkv).astype(bf16)

    misc = proj_ref[:, SEG_MISC:SEG_MISC + LANE]
    ik = _rope_apply(misc, cos1, sin1)
    misc_ref[...] = jnp.where(lane < IDX_DIM, ik,
                              jnp.where(lane < IDX_DIM + IDX_HEADS, misc * IDX_HEADS ** -0.5,
                                        jax.nn.sigmoid(misc)))


def prep_pallas(proj_small, pos_col, nsa_qk_gain, dsa_qk_gain):
    t = proj_small.shape[0]
    bf16 = jnp.bfloat16
    f32 = jnp.float32
    gq = jnp.tile(nsa_qk_gain[0], NSA_HEADS).reshape(1, MIX_A)
    gdq = jnp.tile(dsa_qk_gain[0], DSA_HEADS).reshape(1, MIX_B)
    gkv = jnp.stack([jnp.tile(nsa_qk_gain[2], NSA_GROUPS), jnp.tile(nsa_qk_gain[3], NSA_GROUPS)])
    gkb = jnp.tile(dsa_qk_gain[1], 2).reshape(1, LANE)
    head_of = np.arange(MIX_A) // HEAD_DIM
    bd = jnp.asarray(head_of[:, None] == head_of[None, :], bf16)
    rc = _rope_lane_constants()
    row = lambda w: pl.BlockSpec((PREP_TM, w), lambda i: (i, 0))
    full = lambda a: pl.BlockSpec(a.shape, lambda i: (0,) * a.ndim)
    widths = [MIX_A, MIX_B, IDX_HEADS * IDX_DIM, 2 * LANE, LANE, LANE, LANE, LANE, LANE, LANE]
    dtypes = [bf16, bf16, bf16, f32, bf16, bf16, bf16, bf16, bf16, f32]
    return pl.pallas_call(
        _prep_kernel,
        grid=(t // PREP_TM,),
        in_specs=[row(PROJ_SMALL), row(1), full(rc), full(gq), full(gdq), full(gkv), full(gkb),
                  full(bd)],
        out_specs=[row(w) for w in widths],
        out_shape=[jax.ShapeDtypeStruct((t, w), dt) for w, dt in zip(widths, dtypes)],
        compiler_params=pltpu.CompilerParams(
            dimension_semantics=("parallel",), vmem_limit_bytes=VMEM_LIMIT),
        name="proj_prep",
    )(proj_small, pos_col, rc, gq, gdq, gkv, gkb, bd)


def _compress_kernel(flat_ref, pe_ref, w1_ref, w2_ref, pos_ref, rc_ref, gain_ref, bd_ref, o_ref):
    bf16 = jnp.bfloat16
    f32 = jnp.float32
    outs = []
    for g in range(NSA_GROUPS):
        xin = (flat_ref[0, 0, g] + pe_ref[0]).astype(bf16)
        hid = jax.nn.gelu(jnp.dot(xin, w1_ref[0], preferred_element_type=f32))
        outs.append(jnp.dot(hid.astype(bf16), w2_ref[0], preferred_element_type=f32))
    out = jnp.concatenate(outs, axis=-1)

    @pl.when(pl.program_id(0) == 0)
    def _():
        cos, sin = _rope_tables(pos_ref[0], rc_ref[...])
        o_ref[0, 0] = _rope_apply(_head_norm(out, gain_ref[...], bd_ref[...]), cos, sin)

    @pl.when(pl.program_id(0) != 0)
    def _():
        o_ref[0, 0] = out


def compress_pallas(kvc, positions, cmp_pos, cmp_w1, cmp_w2, k_gain):
    b, s, _ = kvc.shape
    bf16 = jnp.bfloat16
    n_chunk = s // CMP_STRIDE
    n_cmp = n_chunk - CMP_LEN // CMP_STRIDE + 1
    n_pad = -(-n_cmp // LANE) * LANE
    c = kvc.reshape(b, n_chunk, CMP_STRIDE, 2, NSA_GROUPS, HEAD_DIM).transpose(3, 0, 4, 1, 2, 5)
    c = c.reshape(2, b, NSA_GROUPS, n_chunk, CMP_STRIDE * HEAD_DIM)
    flat = jnp.concatenate([c[..., j:j + n_cmp, :] for j in range(CMP_LEN // CMP_STRIDE)], axis=-1)
    flat = jnp.pad(flat, ((0, 0),) * 3 + ((0, n_pad - n_cmp), (0, 0)))
    pe = cmp_pos.reshape(2, 1, CMP_LEN * HEAD_DIM)
    pos_c = positions[:, CMP_LEN - 1::CMP_STRIDE][:, :n_cmp]
    pos_c = jnp.pad(pos_c, ((0, 0), (0, n_pad - n_cmp))).reshape(b, n_pad, 1)
    gain = jnp.tile(k_gain, NSA_GROUPS).reshape(1, LANE)
    head_of = np.arange(LANE) // HEAD_DIM
    bd = jnp.asarray(head_of[:, None] == head_of[None, :], bf16)
    kdim = CMP_LEN * HEAD_DIM
    return pl.pallas_call(
        _compress_kernel,
        grid=(2, b),
        in_specs=[
            pl.BlockSpec((1, 1, NSA_GROUPS, n_pad, kdim), lambda w, bi: (w, bi, 0, 0, 0)),
            pl.BlockSpec((1, 1, kdim), lambda w, bi: (w, 0, 0)),
            pl.BlockSpec((1, kdim, CMP_HIDDEN), lambda w, bi: (w, 0, 0)),
            pl.BlockSpec((1, CMP_HIDDEN, HEAD_DIM), lambda w, bi: (w, 0, 0)),
            pl.BlockSpec((1, n_pad, 1), lambda w, bi: (bi, 0, 0)),
            pl.BlockSpec((2, LANE), lambda w, bi: (0, 0)),
            pl.BlockSpec((1, LANE), lambda w, bi: (0, 0)),
            pl.BlockSpec((LANE, LANE), lambda w, bi: (0, 0)),
        ],
        out_specs=pl.BlockSpec((1, 1, n_pad, LANE), lambda w, bi: (w, bi, 0, 0)),
        out_shape=jax.ShapeDtypeStruct((2, b, n_pad, LANE), jnp.float32),
        compiler_params=pltpu.CompilerParams(
            dimension_semantics=("arbitrary", "arbitrary"), vmem_limit_bytes=VMEM_LIMIT),
        name="compress",
    )(flat, pe, cmp_w1.astype(bf16), cmp_w2.astype(bf16), pos_c, _rope_lane_constants(), gain, bd)


ROW_TM = 512


def _merge_kernel(x_ref, ya_ref, yb_ref, mg_ref, wa_ref, wb_ref, wo_ref, o_ref):
    f32 = jnp.float32
    a = jnp.dot(ya_ref[...], wa_ref[...], preferred_element_type=f32)
    b = jnp.dot(yb_ref[...], wb_ref[...], preferred_element_type=f32)
    g = jax.nn.sigmoid(mg_ref[...])
    mix = g[:, :D_MODEL] * a + g[:, D_MODEL:] * b
    o_ref[...] = x_ref[...] + jnp.dot(mix.astype(jnp.bfloat16), wo_ref[...],
                                      preferred_element_type=f32)


def merge_pallas(x2d, ya, yb, mg, wa, wb, wo):
    t, d = x2d.shape
    bf16 = jnp.bfloat16
    row = lambda w: pl.BlockSpec((ROW_TM, w), lambda i: (i, 0))
    full = lambda r, c: pl.BlockSpec((r, c), lambda i: (0, 0))
    return pl.pallas_call(
        _merge_kernel,
        grid=(t // ROW_TM,),
        in_specs=[row(d), row(MIX_A), row(MIX_B), row(2 * d),
                  full(MIX_A, d), full(MIX_B, d), full(d, d)],
        out_specs=row(d),
        out_shape=jax.ShapeDtypeStruct((t, d), jnp.float32),
        compiler_params=pltpu.CompilerParams(
            dimension_semantics=("parallel",), vmem_limit_bytes=VMEM_LIMIT),
        name="merge",
    )(x2d, ya, yb, mg, wa.astype(bf16), wb.astype(bf16), wo.astype(bf16))


def _ple_kernel(x_ref, p_ref, wg_ref, wp_ref, gn_ref, o_ref):
    f32 = jnp.float32
    bf16 = jnp.bfloat16
    x = x_ref[...]
    r = x * lax.rsqrt(jnp.mean(x * x, axis=-1, keepdims=True) + EPS)
    gate = jax.nn.sigmoid(jnp.dot(r.astype(bf16), wg_ref[...], preferred_element_type=f32))
    e = jnp.dot(p_ref[...].astype(bf16), wp_ref[...], preferred_element_type=f32)
    e = e * lax.rsqrt(jnp.mean(e * e, axis=-1, keepdims=True) + EPS) * gn_ref[...]
    o_ref[...] = x + gate * e


def ple_pallas(x2d, p2d, wg, wp, gn):
    t, d = x2d.shape
    bf16 = jnp.bfloat16
    row = lambda w: pl.BlockSpec((ROW_TM, w), lambda i: (i, 0))
    full = lambda r, c: pl.BlockSpec((r, c), lambda i: (0, 0))
    return pl.pallas_call(
        _ple_kernel,
        grid=(t // ROW_TM,),
        in_specs=[row(d), row(PLE_DIM), full(d, d), full(PLE_DIM, d), full(1, d)],
        out_specs=row(d),
        out_shape=jax.ShapeDtypeStruct((t, d), jnp.float32),
        compiler_params=pltpu.CompilerParams(
            dimension_semantics=("parallel",), vmem_limit_bytes=VMEM_LIMIT),
        name="ple",
    )(x2d, p2d, wg.astype(bf16), wp.astype(bf16), gn.reshape(1, d))


NSA_TQ = 128
NSA_TK = 512
NSA_TW = 128
NSA_WIN_TILES = (WINDOW + NSA_TQ) // NSA_TW
NSA_SUB = 128


def _flash_update_t(state, s, v_t, mask=None):
    m, l, acc = state
    m_new = jnp.maximum(m, jnp.max(s, axis=0, keepdims=True))
    alpha = jnp.exp(m - m_new)
    p = jnp.exp(s - m_new)
    if mask is not None:
        p = jnp.where(mask, p, 0.0)
    l = alpha * l + jnp.sum(p, axis=0, keepdims=True)
    acc = alpha * acc + jnp.dot(v_t, p.astype(jnp.bfloat16), preferred_element_type=jnp.float32)
    return m_new, l, acc


def _nsa_kernel(q_ref, kc_ref, vct_ref, ks_ref, vst_ref, kw_ref, vwt_ref, g_ref, ov_ref, exp_ref,
                o_ref, bias_ref, *, n_blk, n_top, n_ktiles):
    f32 = jnp.float32
    bf16 = jnp.bfloat16
    qi = pl.program_id(2)
    q0 = qi * NSA_TQ
    t_row = q0 + lax.broadcasted_iota(jnp.int32, (1, NSA_TQ), 1)

    n_cmp_pad = kc_ref.shape[-2]
    cmp_end = lax.broadcasted_iota(jnp.int32, (n_cmp_pad, 1), 0) * CMP_STRIDE + (CMP_LEN - 1)
    mask_c = cmp_end <= t_row
    kc = kc_ref[0, 0]
    vc_t = vct_ref[0, 0]
    p_sum = jnp.zeros((n_cmp_pad, NSA_TQ), f32)
    o_c = []
    for r in range(NSA_REP):
        s = jnp.dot(kc, q_ref[0, r], preferred_element_type=f32)
        s = jnp.where(mask_c, s, NEG)
        m = jnp.max(s, axis=0, keepdims=True)
        p = jnp.where(mask_c, jnp.exp(s - m), 0.0)
        l = jnp.sum(p, axis=0, keepdims=True)
        p = p / jnp.where(l > 0.0, l, 1.0)
        p_sum = p_sum + p
        o_c.append(jnp.dot(vc_t, p.astype(bf16), preferred_element_type=f32))

    p_hi = p_sum.astype(bf16)
    p_lo = (p_sum - p_hi.astype(f32)).astype(bf16)
    ov = ov_ref[...]
    imp_t = (jnp.dot(ov, p_hi, preferred_element_type=f32)
             + jnp.dot(ov, p_lo, preferred_element_type=f32))
    blk = lax.broadcasted_iota(jnp.int32, (n_blk, 1), 0)
    cur = t_row // SEL_LEN
    forced = (blk == 0) | (blk == cur) | (blk == cur - 1)
    admissible = blk * SEL_LEN <= t_row
    imp_t = jnp.where(forced, jnp.inf, imp_t)
    imp_t = jnp.where(admissible, imp_t, -jnp.inf)
    n_chunks = n_blk // SUBLANE
    chunks = [imp_t[c * SUBLANE:(c + 1) * SUBLANE, :] for c in range(n_chunks)]
    ranks = [jnp.zeros((SUBLANE, NSA_TQ), f32) for _ in range(n_chunks)]
    sub = lax.broadcasted_iota(jnp.int32, (SUBLANE, NSA_TQ), 0)
    for i in range(n_blk):
        ci, si = divmod(i, SUBLANE)
        row = jnp.broadcast_to(chunks[ci][si:si + 1, :], (SUBLANE, NSA_TQ))
        for c in range(n_chunks):
            if c > ci:
                beats = jnp.where(row >= chunks[c], 1.0, 0.0)
            elif c < ci:
                beats = jnp.where(row > chunks[c], 1.0, 0.0)
            else:
                tie = jnp.where(sub > si, 1.0, 0.0)
                beats = jnp.where(row > chunks[c], 1.0, jnp.where(row == chunks[c], tie, 0.0))
            ranks[c] = ranks[c] + beats
    rank = jnp.concatenate(ranks, axis=0)
    sel_t = jnp.where((rank < n_top) & admissible, 1.0, 0.0).astype(bf16)

    n_live = qi // (NSA_TK // NSA_TQ) + 1
    for j in range(n_ktiles):
        @pl.when(j < n_live)
        def _():
            hit = jnp.dot(exp_ref[j * NSA_TK:(j + 1) * NSA_TK, :], sel_t,
                          preferred_element_type=f32)
            kpos = j * NSA_TK + lax.broadcasted_iota(jnp.int32, (NSA_TK, 1), 0)
            bias_ref[j] = jnp.where((hit > 0.5) & (kpos <= t_row), 0.0, NEG)

    def init_state():
        return (jnp.full((1, NSA_TQ), NEG, f32), jnp.zeros((1, NSA_TQ), f32),
                jnp.zeros((HEAD_DIM, NSA_TQ), f32))

    def sel_body(j, states):
        states = list(states)
        for c in range(NSA_TK // NSA_SUB):
            rows = slice(c * NSA_SUB, (c + 1) * NSA_SUB)
            k = ks_ref[0, 0, j, rows, :]
            v_t = vst_ref[0, 0, j, :, rows]
            bias = bias_ref[j, rows, :]
            for r in range(NSA_REP):
                s = jnp.dot(k, q_ref[0, r], preferred_element_type=f32) + bias
                states[r] = _flash_update_t(states[r], s, v_t)
        return tuple(states)

    st_s = lax.fori_loop(0, n_live, sel_body, tuple(init_state() for _ in range(NSA_REP)))

    st_w = [init_state() for _ in range(NSA_REP)]
    for w in range(NSA_WIN_TILES):
        tile = qi - (NSA_WIN_TILES - 1) + w
        tix = jnp.maximum(tile, 0)
        k = kw_ref[0, 0, tix]
        v_t = vwt_ref[0, 0, tix]
        kpos = tile * NSA_TW + lax.broadcasted_iota(jnp.int32, (NSA_TW, 1), 0)
        diff = t_row - kpos
        mask = (kpos >= 0) & (diff >= 0) & (diff < WINDOW)
        for r in range(NSA_REP):
            s = jnp.where(mask, jnp.dot(k, q_ref[0, r], preferred_element_type=f32), NEG)
            st_w[r] = _flash_update_t(st_w[r], s, v_t, mask)

    g = g_ref[0, 0]
    for r in range(NSA_REP):
        o_s = st_s[r][2] / st_s[r][1]
        o_w = st_w[r][2] / st_w[r][1]
        o = (g[r:r + 1, :] * o_c[r] + g[NSA_REP + r:NSA_REP + r + 1, :] * o_s
             + g[2 * NSA_REP + r:2 * NSA_REP + r + 1, :] * o_w)
        o_ref[0, r] = o.astype(o_ref.dtype)


def nsa_attention_pallas(q, kc, vc, ks, vs, kw, vw, gates):
    b, s = q.shape[0], q.shape[1]
    bf16 = jnp.bfloat16
    n_cmp = s // CMP_STRIDE - CMP_LEN // CMP_STRIDE + 1
    n_cmp_pad = -(-n_cmp // LANE) * LANE
    n_blk = s // SEL_LEN
    n_top = min(SEL_BLOCKS, n_blk)
    n_kt = s // NSA_TK
    n_wt = s // NSA_TW
    assert s % NSA_TK == 0 and n_blk % SUBLANE == 0

    pad_c = ((0, 0), (0, n_cmp_pad - kc.shape[1]), (0, 0), (0, 0))
    kcp = jnp.pad(kc, pad_c).transpose(0, 2, 1, 3).astype(bf16)
    vct = jnp.pad(vc, pad_c).transpose(0, 2, 3, 1).astype(bf16)

    def key_tiles(k, tk):
        return k.reshape(b, s // tk, tk, NSA_GROUPS, HEAD_DIM).transpose(0, 3, 1, 2, 4).astype(bf16)

    def val_tiles(v, tk):
        return v.reshape(b, s // tk, tk, NSA_GROUPS, HEAD_DIM).transpose(0, 3, 1, 4, 2).astype(bf16)

    q_t = q.reshape(b, s, NSA_HEADS, HEAD_DIM).transpose(0, 2, 3, 1)
    gates_t = gates.transpose(0, 3, 2, 4, 1).reshape(b, NSA_GROUPS, 3 * NSA_REP, s)

    ci = np.arange(n_cmp_pad)[None, :] * CMP_STRIDE
    sj = np.arange(n_blk)[:, None] * SEL_LEN
    ov = (ci < sj + SEL_LEN) & (ci + CMP_LEN > sj) & (np.arange(n_cmp_pad)[None, :] < n_cmp)
    expand = (np.arange(s)[:, None] // SEL_LEN) == np.arange(n_blk)[None, :]

    per_bg = lambda bi, gi, i: (bi, gi, 0, 0)
    per_bg5 = lambda bi, gi, i: (bi, gi, 0, 0, 0)
    per_q = lambda bi, gi, i: (bi, gi, 0, i)
    out_t = pl.pallas_call(
        functools.partial(_nsa_kernel, n_blk=n_blk, n_top=n_top, n_ktiles=n_kt),
        grid=(b, NSA_GROUPS, s // NSA_TQ),
        in_specs=[
            pl.BlockSpec((1, NSA_REP, HEAD_DIM, NSA_TQ), per_q),
            pl.BlockSpec((1, 1, n_cmp_pad, HEAD_DIM), per_bg),
            pl.BlockSpec((1, 1, HEAD_DIM, n_cmp_pad), per_bg),
            pl.BlockSpec((1, 1, n_kt, NSA_TK, HEAD_DIM), per_bg5),
            pl.BlockSpec((1, 1, n_kt, HEAD_DIM, NSA_TK), per_bg5),
            pl.BlockSpec((1, 1, n_wt, NSA_TW, HEAD_DIM), per_bg5),
            pl.BlockSpec((1, 1, n_wt, HEAD_DIM, NSA_TW), per_bg5),
            pl.BlockSpec((1, 1, 3 * NSA_REP, NSA_TQ), per_q),
            pl.BlockSpec((n_blk, n_cmp_pad), lambda bi, gi, i: (0, 0)),
            pl.BlockSpec((s, n_blk), lambda bi, gi, i: (0, 0)),
        ],
        out_specs=pl.BlockSpec((1, NSA_REP, HEAD_DIM, NSA_TQ), per_q),
        out_shape=jax.ShapeDtypeStruct((b, NSA_HEADS, HEAD_DIM, s), bf16),
        scratch_shapes=[pltpu.VMEM((n_kt, NSA_TK, NSA_TQ), jnp.float32)],
        compiler_params=pltpu.CompilerParams(
            dimension_semantics=("parallel", "parallel", "arbitrary"),
            vmem_limit_bytes=VMEM_LIMIT),
        name="nsa_attention",
    )(q_t, kcp, vct, key_tiles(ks, NSA_TK), val_tiles(vs, NSA_TK), key_tiles(kw, NSA_TW),
      val_tiles(vw, NSA_TW), gates_t, jnp.asarray(ov, bf16), jnp.asarray(expand, bf16))
    return out_t.transpose(0, 3, 1, 2).reshape(b, s, MIX_A)


DSA_TQ = 128
DSA_TK = 512
DSA_SUB = 128
F32_ORDER_MASK = 0x7FFFFFFF
F32_NEG_INF_BITS = -8388608
F32_POS_INF_BITS = 0x7F800000


def _ordered_from_bits(bits):
    return jnp.where(bits >= 0, bits, bits ^ F32_ORDER_MASK)


def _sublane_group_sum(x):
    ways = 4
    g = x.reshape(ways, x.shape[0] // (SUBLANE * ways), SUBLANE, x.shape[1])
    return jnp.sum(jnp.sum(g, axis=1), axis=0)


def _dsa_kernel(iq_ref, w_ref, ik_ref, q_ref, k_ref, vt_ref, tril_ref, o_ref, sc_ref, *, n_keep):
    f32 = jnp.float32
    bf16 = jnp.bfloat16
    qi = pl.program_id(1)
    q0 = qi * DSA_TQ
    t_row = q0 + lax.broadcasted_iota(jnp.int32, (1, DSA_TQ), 1)
    n_live = qi // (DSA_TK // DSA_TQ) + 1
    key_off = lax.broadcasted_iota(jnp.int32, (DSA_TK, 1), 0)

    w = w_ref[0]

    def score_body(j, carry):
        ik = ik_ref[0, j]
        acc = jnp.zeros((DSA_TK, DSA_TQ), f32)
        for h in range(IDX_HEADS):
            logit = jnp.dot(ik, iq_ref[0, h], preferred_element_type=f32)
            acc = acc + jnp.maximum(logit, 0.0) * w[h:h + 1, :]
        causal = (j * DSA_TK + key_off) <= t_row
        sc_ref[j] = jnp.where(causal, acc + 0.0, -jnp.inf)
        return carry

    lax.fori_loop(0, n_live, score_body, 0)

    def count(pred):
        def body(j, acc):
            return acc + _sublane_group_sum(jnp.where(pred(sc_ref[j]), 1.0, 0.0))
        part = lax.fori_loop(0, n_live, body, jnp.zeros((SUBLANE, DSA_TQ), f32))
        return jnp.sum(part, axis=0, keepdims=True)

    def bisect_body(_, lohi):
        lo, hi = lohi
        mid = (lo >> 1) + (hi >> 1) + (lo & hi & 1)
        thr = lax.bitcast_convert_type(_ordered_from_bits(mid), f32)
        ok = count(lambda sc: sc >= thr) >= n_keep
        return jnp.where(ok, mid, lo), jnp.where(ok, hi, mid)

    lo0 = jnp.full((1, DSA_TQ), F32_NEG_INF_BITS ^ F32_ORDER_MASK, jnp.int32)
    hi0 = jnp.full((1, DSA_TQ), F32_POS_INF_BITS + 1, jnp.int32)
    lo, _ = lax.fori_loop(0, 32, bisect_body, (lo0, hi0))
    thr = lax.bitcast_convert_type(_ordered_from_bits(lo), f32)
    need = n_keep - count(lambda sc: sc > thr)

    tril = tril_ref[...]

    def mask_body(j, ties_before):
        sc = sc_ref[j]
        eq = sc == thr
        pref = ties_before + jnp.dot(tril, jnp.where(eq, 1.0, 0.0).astype(bf16),
                                     preferred_element_type=f32)
        keep_tie = jnp.where(pref <= need, 0.0, NEG)
        bias = jnp.where(sc > thr, 0.0, jnp.where(eq, keep_tie, NEG))
        causal = (j * DSA_TK + key_off) <= t_row
        sc_ref[j] = jnp.where(causal, bias, NEG)
        return pref[DSA_TK - 1:DSA_TK, :]

    lax.fori_loop(0, n_live, mask_body, jnp.zeros((1, DSA_TQ), f32))

    def att_body(j, states):
        states = list(states)
        for c in range(DSA_TK // DSA_SUB):
            rows = slice(c * DSA_SUB, (c + 1) * DSA_SUB)
            k = k_ref[0, j, rows, :]
            v_t = vt_ref[0, j, :, rows]
            bias = sc_ref[j, rows, :]
            for h in range(DSA_HEADS):
                s = jnp.dot(k, q_ref[0, h], preferred_element_type=f32) + bias
                states[h] = _flash_update_t(states[h], s, v_t)
        return tuple(states)

    init = tuple((jnp.full((1, DSA_TQ), NEG, f32), jnp.zeros((1, DSA_TQ), f32),
                  jnp.zeros((HEAD_DIM, DSA_TQ), f32)) for _ in range(DSA_HEADS))
    st = lax.fori_loop(0, n_live, att_body, init)
    for h in range(DSA_HEADS):
        o_ref[0, h] = (st[h][2] / st[h][1]).astype(o_ref.dtype)


def dsa_attention_pallas(q2, k, v, iq2, ik, w):
    b, s = q2.shape[0], q2.shape[1]
    bf16 = jnp.bfloat16
    n_keep = min(DSA_TOPK, s // 4)
    n_kt = s // DSA_TK
    assert s % DSA_TK == 0

    def heads_t(x, nh, dh):
        return x.reshape(b, s, nh, dh).transpose(0, 2, 3, 1)

    def key_tiles(x):
        return x.reshape(b, n_kt, DSA_TK, x.shape[-1]).astype(bf16)

    v_t = v.reshape(b, n_kt, DSA_TK, HEAD_DIM).transpose(0, 1, 3, 2).astype(bf16)
    tril = np.arange(DSA_TK)[:, None] >= np.arange(DSA_TK)[None, :]

    per_b = lambda bi, i: (bi, 0, 0, 0)
    per_q = lambda bi, i: (bi, 0, 0, i)
    out_t = pl.pallas_call(
        functools.partial(_dsa_kernel, n_keep=n_keep),
        grid=(b, s // DSA_TQ),
        in_specs=[
            pl.BlockSpec((1, IDX_HEADS, IDX_DIM, DSA_TQ), per_q),
            pl.BlockSpec((1, IDX_HEADS, DSA_TQ), lambda bi, i: (bi, 0, i)),
            pl.BlockSpec((1, n_kt, DSA_TK, IDX_DIM), per_b),
            pl.BlockSpec((1, DSA_HEADS, HEAD_DIM, DSA_TQ), per_q),
            pl.BlockSpec((1, n_kt, DSA_TK, HEAD_DIM), per_b),
            pl.BlockSpec((1, n_kt, HEAD_DIM, DSA_TK), per_b),
            pl.BlockSpec((DSA_TK, DSA_TK), lambda bi, i: (0, 0)),
        ],
        out_specs=pl.BlockSpec((1, DSA_HEADS, HEAD_DIM, DSA_TQ), per_q),
        out_shape=jax.ShapeDtypeStruct((b, DSA_HEADS, HEAD_DIM, s), bf16),
        scratch_shapes=[pltpu.VMEM((n_kt, DSA_TK, DSA_TQ), jnp.float32)],
        compiler_params=pltpu.CompilerParams(
            dimension_semantics=("parallel", "arbitrary"),
            vmem_limit_bytes=VMEM_LIMIT),
        name="dsa_attention",
    )(heads_t(iq2, IDX_HEADS, IDX_DIM), w.transpose(0, 2, 1), key_tiles(ik),
      heads_t(q2, DSA_HEADS, HEAD_DIM), key_tiles(k), v_t, jnp.asarray(tril, bf16))
    return out_t.transpose(0, 3, 1, 2).reshape(b, s, MIX_B)


PEER_TT = 128
PEER_SLOTS = PEER_HEADS * PEER_TOPK
PEER_CAND_A0 = PEER_HALF_TOPK
PEER_CAND_SQ = SUBLANE


def _peer_cand_flat_ids():
    ids = [0 * PEER_HALF_TOPK + bb for bb in range(PEER_CAND_A0)]
    for a in range(1, PEER_CAND_SQ):
        ids += [a * PEER_HALF_TOPK + bb for bb in range(PEER_CAND_SQ)]
    ids += [a * PEER_HALF_TOPK for a in range(PEER_CAND_SQ, PEER_HALF_TOPK)]
    return np.asarray(ids, np.int32)


def _extract_top(cur, ids, n):
    vals, picks = [], []
    for _ in range(n):
        m = jnp.max(cur, axis=0, keepdims=True)
        pick = jnp.min(jnp.where(cur == m, ids, jnp.int32(2 ** 30)), axis=0, keepdims=True)
        vals.append(m)
        picks.append(pick)
        cur = jnp.where(ids == pick, -jnp.inf, cur)
    return vals, picks


def _pair_grid(first, second, op):
    pieces = [op(first[0:1], second)]
    for a in range(1, PEER_CAND_SQ):
        pieces.append(op(first[a:a + 1], second[0:PEER_CAND_SQ]))
    pieces.append(op(first[PEER_CAND_SQ:], second[0:1]))
    return jnp.concatenate(pieces, axis=0)


def _peer_topk_kernel(x_ref, g_ref, wq_ref, sk_ref, fid_ref, h_ref, eidx_ref, gate_ref):
    f32 = jnp.float32
    bf16 = jnp.bfloat16
    x = x_ref[...]
    h = x * lax.rsqrt(jnp.mean(x * x, axis=-1, keepdims=True) + EPS) * g_ref[...]
    h_ref[...] = h
    q = jnp.dot(h.astype(bf16), wq_ref[...], preferred_element_type=f32).astype(bf16)
    key_ids = lax.broadcasted_iota(jnp.int32, (N_KEYS, PEER_TT), 0)
    fid = fid_ref[...]
    nt_dims = (((1,), (1,)), ((), ()))
    half = PEER_QDIM // 2
    for hd in range(PEER_HEADS):
        tops = []
        for c in range(2):
            col = (hd * 2 + c) * half
            s_t = lax.dot_general(sk_ref[c], q[:, col:col + half], nt_dims,
                                  preferred_element_type=f32)
            vals, picks = _extract_top(s_t, key_ids, PEER_HALF_TOPK)
            tops.append((jnp.concatenate(vals, axis=0), jnp.concatenate(picks, axis=0)))
        (v1, i1), (v2, i2) = tops
        cand = _pair_grid(v1, v2, lambda a, b: a + b)
        cexp = _pair_grid(i1, i2, lambda a, b: a * N_KEYS + b)
        vals, picks = _extract_top(cand, fid, PEER_TOPK)
        top = jnp.concatenate(vals, axis=0)
        eids = [jnp.max(jnp.where(fid == p, cexp, -1), axis=0, keepdims=True) for p in picks]
        ex = jnp.exp(top - top[0:1])
        gate = ex / jnp.sum(ex, axis=0, keepdims=True)
        eidx_ref[0, hd * PEER_TOPK:(hd + 1) * PEER_TOPK, :] = jnp.concatenate(eids, axis=0)
        gate_ref[0, hd * PEER_TOPK:(hd + 1) * PEER_TOPK, :] = gate


def peer_topk_pallas(x2d, ffn_norm, wq, sub_keys):
    t, d = x2d.shape
    n_tiles = t // PEER_TT
    fid = np.broadcast_to(_peer_cand_flat_ids()[:, None], (_peer_cand_flat_ids().shape[0], PEER_TT))
    n_cand = fid.shape[0]
    return pl.pallas_call(
        _peer_topk_kernel,
        grid=(n_tiles,),
        in_specs=[
            pl.BlockSpec((PEER_TT, d), lambda i: (i, 0)),
            pl.BlockSpec((1, d), lambda i: (0, 0)),
            pl.BlockSpec((d, PEER_HEADS * PEER_QDIM), lambda i: (0, 0)),
            pl.BlockSpec((2, N_KEYS, PEER_QDIM // 2), lambda i: (0, 0, 0)),
            pl.BlockSpec((n_cand, PEER_TT), lambda i: (0, 0)),
        ],
        out_specs=[
            pl.BlockSpec((PEER_TT, d), lambda i: (i, 0)),
            pl.BlockSpec((1, PEER_SLOTS, PEER_TT), lambda i: (i, 0, 0)),
            pl.BlockSpec((1, PEER_SLOTS, PEER_TT), lambda i: (i, 0, 0)),
        ],
        out_shape=[
            jax.ShapeDtypeStruct((t, d), jnp.float32),
            jax.ShapeDtypeStruct((n_tiles, PEER_SLOTS, PEER_TT), jnp.int32),
            jax.ShapeDtypeStruct((n_tiles, PEER_SLOTS, PEER_TT), jnp.float32),
        ],
        compiler_params=pltpu.CompilerParams(
            dimension_semantics=("parallel",), vmem_limit_bytes=VMEM_LIMIT),
        name="peer_topk",
    )(x2d, ffn_norm.reshape(1, d), wq.astype(jnp.bfloat16), sub_keys.astype(jnp.bfloat16),
      jnp.asarray(fid))


PEER_GT = 8
PEER_ROWS = PEER_GT * PEER_SLOTS
PEER_NSLOT = 2
PEER_DMA_QUEUES = 2


def _peer_eval_kernel(idx_ref, h_ref, gate_ref, x_ref, uv_hbm, o_ref, buf, sem):
    j = pl.program_id(0)
    n_blocks = pl.num_programs(0) - 1
    lane = lax.broadcasted_iota(jnp.int32, (PEER_SLOTS, PEER_TT), 1)
    lanes_per_block = PEER_NSLOT * PEER_GT
    lane0 = ((j - 1) % (PEER_TT // lanes_per_block)) * lanes_per_block

    def evaluate(slot, tok):
        row0 = slot * PEER_GT + tok
        rows = buf[slot, tok * PEER_SLOTS:(tok + 1) * PEER_SLOTS, :]
        act = jnp.sum(rows[:, :D_MODEL] * h_ref[row0:row0 + 1, :], axis=-1, keepdims=True)
        gate = jnp.sum(jnp.where(lane == lane0 + row0, gate_ref[0], 0.0), axis=-1, keepdims=True)
        wgt = gate * jax.nn.gelu(act)
        o_ref[row0:row0 + 1, :] = (x_ref[row0:row0 + 1, :]
                                   + jnp.sum(wgt * rows[:, D_MODEL:], axis=0, keepdims=True))

    def issue(slot, tok):
        for r in range(tok * PEER_SLOTS, (tok + 1) * PEER_SLOTS):
            e = idx_ref[0, 0, slot * PEER_ROWS + r]
            pltpu.async_copy(uv_hbm.at[e], buf.at[slot, pl.ds(r, 1)], sem.at[slot],
                             priority=r % PEER_DMA_QUEUES)

    def step(do_evaluate, do_issue):
        for slot in range(PEER_NSLOT):
            if do_evaluate:
                pltpu.make_async_copy(uv_hbm.at[pl.ds(0, PEER_ROWS), 0], buf.at[slot],
                                      sem.at[slot]).wait()
            for tok in range(PEER_GT):
                if do_evaluate:
                    evaluate(slot, tok)
                if do_issue:
                    issue(slot, tok)

    @pl.when(j == 0)
    def _():
        step(False, True)

    @pl.when((j > 0) & (j < n_blocks))
    def _():
        step(True, True)

    @pl.when(j == n_blocks)
    def _():
        step(True, False)


def peer_eval_pallas(x2d, h2d, eidx_t, gate_t, u, v):
    t, d = x2d.shape
    blk = PEER_NSLOT * PEER_GT
    n_blocks = t // blk
    uv = jnp.concatenate([u, v], axis=1).reshape(u.shape[0], 1, 2 * d)
    idx = eidx_t.transpose(0, 2, 1).reshape(n_blocks, 1, PEER_NSLOT * PEER_ROWS)
    per_tile = PEER_TT // blk
    prev = lambda j: jnp.maximum(j - 1, 0)
    return pl.pallas_call(
        _peer_eval_kernel,
        grid=(n_blocks + 1,),
        in_specs=[
            pl.BlockSpec((1, 1, PEER_NSLOT * PEER_ROWS),
                         lambda j: (jnp.minimum(j, n_blocks - 1), 0, 0), memory_space=pltpu.SMEM),
            pl.BlockSpec((blk, d), lambda j: (prev(j), 0)),
            pl.BlockSpec((1, PEER_SLOTS, PEER_TT), lambda j: (prev(j) // per_tile, 0, 0)),
            pl.BlockSpec((blk, d), lambda j: (prev(j), 0)),
            pl.BlockSpec(memory_space=pl.ANY),
        ],
        out_specs=pl.BlockSpec((blk, d), lambda j: (prev(j), 0)),
        out_shape=jax.ShapeDtypeStruct((t, d), jnp.float32),
        scratch_shapes=[pltpu.VMEM((PEER_NSLOT, PEER_ROWS, 2 * d), jnp.float32),
                        pltpu.SemaphoreType.DMA((PEER_NSLOT,))],
        compiler_params=pltpu.CompilerParams(
            dimension_semantics=("arbitrary",), vmem_limit_bytes=VMEM_LIMIT),
        name="peer_eval",
    )(idx, h2d, gate_t, x2d, uv)


def peer_pallas(x, ffn_norm, wq, sub_keys, u, v):
    b, s, d = x.shape
    x2d = x.reshape(b * s, d)
    h2d, eidx_t, gate_t = peer_topk_pallas(x2d, ffn_norm, wq, sub_keys)
    return peer_eval_pallas(x2d, h2d, eidx_t, gate_t, u, v).reshape(b, s, d)


def _rms(x):
    xf = x.astype(jnp.float32)
    return xf * lax.rsqrt(jnp.mean(xf * xf, axis=-1, keepdims=True) + EPS)


def rms_norm(x, g):
    return (_rms(x) * g.astype(jnp.float32)).astype(x.dtype)


def rope(x, pos):
    half = ROT_DIM // 2
    inv = ROPE_THETA ** (-jnp.arange(half, dtype=jnp.float32) / half)
    ang = pos.astype(jnp.float32)[..., None] * inv
    cos = jnp.cos(ang)[:, :, None, :]
    sin = jnp.sin(ang)[:, :, None, :]
    xf = x.astype(jnp.float32)
    x1, x2, rest = xf[..., :half], xf[..., half:ROT_DIM], xf[..., ROT_DIM:]
    return jnp.concatenate([x1 * cos - x2 * sin, x2 * cos + x1 * sin, rest], axis=-1).astype(x.dtype)


def masked_softmax(s, mask):
    p = jax.nn.softmax(jnp.where(mask, s, NEG), axis=-1)
    return jnp.where(mask, p, 0.0)


def compress(t, pos_emb, w1, w2):
    b, s, g, d = t.shape
    r = CMP_LEN // CMP_STRIDE
    n_chunk = s // CMP_STRIDE
    n_cmp = n_chunk - r + 1
    c = t.reshape(b, n_chunk, CMP_STRIDE, g, d)
    blocks = jnp.concatenate([c[:, j:j + n_cmp] for j in range(r)], axis=2)
    blocks = blocks + pos_emb[None, None, :, None, :].astype(t.dtype)
    flat = blocks.transpose(0, 1, 3, 2, 4).reshape(b, n_cmp, g, CMP_LEN * d)
    return jax.nn.gelu(flat @ w1) @ w2


def nsa_attention(q, kc, vc, ks, vs, kw, vw, gates):
    b, s = q.shape[0], q.shape[1]
    n_cmp = kc.shape[1]
    n_sel_blocks = s // SEL_LEN
    n_top = min(SEL_BLOCKS, n_sel_blocks)
    scale = HEAD_DIM ** -0.5
    dt = q.dtype
    cmp_end = jnp.asarray(np.arange(n_cmp) * CMP_STRIDE + CMP_LEN - 1)
    ci = np.arange(n_cmp)[:, None] * CMP_STRIDE
    sj = np.arange(n_sel_blocks)[None, :] * SEL_LEN
    overlap = jnp.asarray(((ci < sj + SEL_LEN) & (ci + CMP_LEN > sj)).astype(np.float32))
    ks_blk = ks.reshape(b, n_sel_blocks, SEL_LEN, NSA_GROUPS, HEAD_DIM).transpose(0, 3, 1, 2, 4)
    vs_blk = vs.reshape(b, n_sel_blocks, SEL_LEN, NSA_GROUPS, HEAD_DIM).transpose(0, 3, 1, 2, 4)
    kw_pad = jnp.pad(kw, ((0, 0), (WINDOW, 0), (0, 0), (0, 0)))
    vw_pad = jnp.pad(vw, ((0, 0), (WINDOW, 0), (0, 0), (0, 0)))
    bix = jnp.arange(b)[:, None, None, None]
    gix = jnp.arange(NSA_GROUPS)[None, :, None, None]
    blk_ids = jnp.arange(n_sel_blocks)
    in_blk = jnp.arange(SEL_LEN)
    win_off = jnp.arange(WINDOW + Q_BLOCK) - WINDOW

    def block(bi):
        q0 = bi * Q_BLOCK
        t = q0 + jnp.arange(Q_BLOCK)
        qg = lax.dynamic_slice_in_dim(q, q0, Q_BLOCK, 1).reshape(b, Q_BLOCK, NSA_GROUPS, NSA_REP, HEAD_DIM)
        s_c = jnp.einsum('bqgrd,bngd->bgrqn', qg, kc).astype(jnp.float32) * scale
        p_c = masked_softmax(s_c, cmp_end[None, :] <= t[:, None])
        o_c = jnp.einsum('bgrqn,bngd->bqgrd', p_c.astype(dt), vc)
        imp = jnp.einsum('bgrqn,nj->bgqj', p_c, overlap)
        cur = (t // SEL_LEN)[:, None]
        forced = (blk_ids[None] == 0) | (blk_ids[None] == cur) | (blk_ids[None] == cur - 1)
        imp = jnp.where(forced, jnp.inf, imp)
        imp = jnp.where(blk_ids[None] * SEL_LEN <= t[:, None], imp, -jnp.inf)
        _, sel = lax.top_k(imp, n_top)
        k_g = ks_blk[bix, gix, sel].reshape(b, NSA_GROUPS, Q_BLOCK, n_top * SEL_LEN, HEAD_DIM)
        v_g = vs_blk[bix, gix, sel].reshape(b, NSA_GROUPS, Q_BLOCK, n_top * SEL_LEN, HEAD_DIM)
        key_pos = (sel[..., None] * SEL_LEN + in_blk).reshape(b, NSA_GROUPS, 1, Q_BLOCK, n_top * SEL_LEN)
        s_s = jnp.einsum('bqgrd,bgqmd->bgrqm', qg, k_g).astype(jnp.float32) * scale
        p_s = masked_softmax(s_s, key_pos <= t[:, None])
        o_s = jnp.einsum('bgrqm,bgqmd->bqgrd', p_s.astype(dt), v_g)
        k_w = lax.dynamic_slice_in_dim(kw_pad, q0, WINDOW + Q_BLOCK, 1)
        v_w = lax.dynamic_slice_in_dim(vw_pad, q0, WINDOW + Q_BLOCK, 1)
        kpos = q0 + win_off
        diff = t[:, None] - kpos[None, :]
        mask_w = (kpos[None, :] >= 0) & (diff >= 0) & (diff < WINDOW)
        s_w = jnp.einsum('bqgrd,bkgd->bgrqk', qg, k_w).astype(jnp.float32) * scale
        p_w = masked_softmax(s_w, mask_w)
        o_w = jnp.einsum('bgrqk,bkgd->bqgrd', p_w.astype(dt), v_w)
        g = lax.dynamic_slice_in_dim(gates, q0, Q_BLOCK, 1)[..., None]
        o = g[:, :, 0] * o_c + g[:, :, 1] * o_s + g[:, :, 2] * o_w
        return o.reshape(b, Q_BLOCK, MIX_A)

    out = lax.map(block, jnp.arange(s // Q_BLOCK))
    return out.transpose(1, 0, 2, 3).reshape(b, s, MIX_A)


def dsa_attention(q, k, v, iq, ik, iw):
    b, s = q.shape[0], q.shape[1]
    n_keep = min(DSA_TOPK, s // 4)
    scale = HEAD_DIM ** -0.5
    dt = q.dtype
    key_pos = jnp.arange(s)
    bix = jnp.arange(b)[:, None, None]

    def block(bi):
        q0 = bi * Q_BLOCK
        t = q0 + jnp.arange(Q_BLOCK)
        qb = lax.dynamic_slice_in_dim(q, q0, Q_BLOCK, 1)
        iqb = lax.dynamic_slice_in_dim(iq, q0, Q_BLOCK, 1)
        iwb = lax.dynamic_slice_in_dim(iw, q0, Q_BLOCK, 1).astype(jnp.float32) * IDX_HEADS ** -0.5
        logits = jnp.einsum('bqhd,bsd->bqhs', iqb, ik).astype(jnp.float32) * IDX_DIM ** -0.5
        score = jnp.einsum('bqhs,bqh->bqs', jax.nn.relu(logits), iwb)
        score = jnp.where(key_pos[None, None, :] <= t[None, :, None], score, -jnp.inf)
        _, sel = lax.top_k(score, n_keep)
        k_g = k[bix, sel]
        v_g = v[bix, sel]
        att = jnp.einsum('bqhd,bqkd->bhqk', qb, k_g).astype(jnp.float32) * scale
        p = masked_softmax(att, (sel <= t[None, :, None])[:, None])
        o = jnp.einsum('bhqk,bqkd->bqhd', p.astype(dt), v_g)
        return o.reshape(b, Q_BLOCK, MIX_B)

    out = lax.map(block, jnp.arange(s // Q_BLOCK))
    return out.transpose(1, 0, 2, 3).reshape(b, s, MIX_B)


def peer(h, wq, sub_keys, u, v):
    b, s, d = h.shape
    dt = h.dtype
    tok = h.reshape(b * s // PEER_TOKEN_BLOCK, PEER_TOKEN_BLOCK, d)

    def block(xb):
        q = (xb @ wq).reshape(PEER_TOKEN_BLOCK, PEER_HEADS, 2, PEER_QDIM // 2)
        s1 = jnp.einsum('thd,kd->thk', q[:, :, 0], sub_keys[0]).astype(jnp.float32)
        s2 = jnp.einsum('thd,kd->thk', q[:, :, 1], sub_keys[1]).astype(jnp.float32)
        v1, i1 = lax.top_k(s1, PEER_HALF_TOPK)
        v2, i2 = lax.top_k(s2, PEER_HALF_TOPK)
        cand = (v1[..., :, None] + v2[..., None, :]).reshape(PEER_TOKEN_BLOCK, PEER_HEADS, -1)
        cidx = (i1[..., :, None] * N_KEYS + i2[..., None, :]).reshape(PEER_TOKEN_BLOCK, PEER_HEADS, -1)
        top, pos = lax.top_k(cand, PEER_TOPK)
        eidx = jnp.take_along_axis(cidx, pos, axis=-1)
        gate = jax.nn.softmax(top, axis=-1)
        act = jax.nn.gelu(jnp.einsum('td,thkd->thk', xb, u[eidx]).astype(jnp.float32))
        return jnp.einsum('thk,thkd->td', (gate * act).astype(dt), v[eidx])

    return lax.map(block, tok).reshape(b, s, d)


def hybrid_layer(x, p_i, positions, attn_norm, w_in, nsa_qk_gain, cmp_pos, cmp_w1, cmp_w2,
                 dsa_qk_gain, w_branch_a, w_branch_b, w_out, ffn_norm, peer_wq, peer_sub_keys,
                 peer_u, peer_v, ple_w, ple_gate_w, ple_norm):
    b, s, d = x.shape
    t = b * s
    bf16 = jnp.bfloat16
    x2d = x.reshape(t, d)

    small_cols, mg_cols = _in_proj_column_order()
    w_small = jnp.pad(w_in[:, small_cols], ((0, 0), (0, PROJ_SMALL - small_cols.size))).astype(bf16)
    proj_small = norm_matmul(x2d, attn_norm, w_small, tn=PROJ_SMALL // 5)
    mg = norm_matmul(x2d, attn_norm, w_in[:, mg_cols].astype(bf16))
    qa, qb, iq, kvc, ks, vs, kw, vw, kbvb, misc = prep_pallas(
        proj_small, positions.reshape(t, 1), nsa_qk_gain, dsa_qk_gain)

    cmp = compress_pallas(kvc.reshape(b, s, -1), positions, cmp_pos, cmp_w1, cmp_w2, nsa_qk_gain[1])
    per_group = lambda a: a.reshape(b, -1, NSA_GROUPS, HEAD_DIM)
    gates = misc[:, IDX_DIM + IDX_HEADS:IDX_DIM + IDX_HEADS + 3 * NSA_HEADS]
    ya = nsa_attention_pallas(qa.reshape(b, s, MIX_A), per_group(cmp[0]), per_group(cmp[1]),
                              per_group(ks), per_group(vs), per_group(kw), per_group(vw),
                              gates.reshape(b, s, 3, NSA_GROUPS, NSA_REP))

    kbvb = kbvb.reshape(b, s, 2, HEAD_DIM)
    yb = dsa_attention_pallas(qb.reshape(b, s, MIX_B), kbvb[:, :, 0], kbvb[:, :, 1],
                              iq.reshape(b, s, -1), misc[:, :IDX_DIM].reshape(b, s, IDX_DIM),
                              misc[:, IDX_DIM:IDX_DIM + IDX_HEADS].reshape(b, s, IDX_HEADS))

    x2d = merge_pallas(x2d, ya.reshape(t, MIX_A), yb.reshape(t, MIX_B), mg,
                       w_branch_a, w_branch_b, w_out)
    x2d = peer_pallas(x2d.reshape(b, s, d), ffn_norm, peer_wq, peer_sub_keys, peer_u, peer_v)
    x2d = ple_pallas(x2d.reshape(t, d), p_i.reshape(t, PLE_DIM), ple_gate_w, ple_w, ple_norm)
    return x2d.reshape(b, s, d)


def kernel(x, p, positions, attn_norm, w_in, nsa_qk_gain, cmp_pos, cmp_w1, cmp_w2,
           dsa_qk_gain, w_branch_a, w_branch_b, w_out, ffn_norm, peer_wq, peer_sub_keys,
           peer_u, peer_v, ple_w, ple_gate_w, ple_norm):
    for i in range(DEPTH):
        x = hybrid_layer(x, p[i], positions, attn_norm[i], w_in[i], nsa_qk_gain[i], cmp_pos[i],
                         cmp_w1[i], cmp_w2[i], dsa_qk_gain[i], w_branch_a[i], w_branch_b[i],
                         w_out[i], ffn_norm[i], peer_wq[i], peer_sub_keys[i], peer_u[i],
                         peer_v[i], ple_w[i], ple_gate_w[i], ple_norm[i])
    return x
```

```python
import functools

import numpy as np
import jax
import jax.numpy as jnp
from jax import lax
from jax.experimental import pallas as pl
from jax.experimental.pallas import tpu as pltpu

D_MODEL = 1024
DEPTH = 2

HEAD_DIM = 64
ROT_DIM = HEAD_DIM // 4
ROPE_THETA = 500000.0
NEG = -1e30
EPS = 1e-6

NSA_HEADS = 8
NSA_GROUPS = 2
NSA_REP = NSA_HEADS // NSA_GROUPS
CMP_LEN = 32
CMP_STRIDE = 16
CMP_HIDDEN = 256
SEL_LEN = 64
SEL_BLOCKS = 16
WINDOW = 512

DSA_HEADS = 8
IDX_HEADS = 8
IDX_DIM = 64
DSA_TOPK = 256

PEER_HEADS = 8
PEER_QDIM = 256
N_KEYS = 128
PEER_HALF_TOPK = 16
PEER_TOPK = 16

PLE_DIM = 256

MIX_A = NSA_HEADS * HEAD_DIM
MIX_B = DSA_HEADS * HEAD_DIM
IN_SIZES = (
    MIX_A,
    6 * NSA_GROUPS * HEAD_DIM,
    3 * NSA_HEADS,
    MIX_B,
    2 * HEAD_DIM,
    IDX_HEADS * IDX_DIM,
    IDX_DIM,
    IDX_HEADS,
    2 * D_MODEL,
)

LANE = 128
SUBLANE = 8
VMEM_LIMIT = 48 * 1024 * 1024


def _norm_matmul_kernel(x_ref, g_ref, w_ref, o_ref, h_ref):
    @pl.when(pl.program_id(1) == 0)
    def _():
        x = x_ref[...]
        h = x * lax.rsqrt(jnp.mean(x * x, axis=-1, keepdims=True) + EPS) * g_ref[...]
        h_ref[...] = h.astype(jnp.bfloat16)

    o_ref[...] = jnp.dot(h_ref[...], w_ref[...], preferred_element_type=jnp.float32)


def norm_matmul(x2d, g, w_bf16, *, tm=512, tn=512):
    m, k = x2d.shape
    n = w_bf16.shape[1]
    assert m % tm == 0 and n % tn == 0
    return pl.pallas_call(
        _norm_matmul_kernel,
        grid=(m // tm, n // tn),
        in_specs=[
            pl.BlockSpec((tm, k), lambda i, j: (i, 0)),
            pl.BlockSpec((1, k), lambda i, j: (0, 0)),
            pl.BlockSpec((k, tn), lambda i, j: (0, j)),
        ],
        out_specs=pl.BlockSpec((tm, tn), lambda i, j: (i, j)),
        out_shape=jax.ShapeDtypeStruct((m, n), jnp.float32),
        scratch_shapes=[pltpu.VMEM((tm, k), jnp.bfloat16)],
        compiler_params=pltpu.CompilerParams(
            dimension_semantics=("parallel", "arbitrary"),
            vmem_limit_bytes=VMEM_LIMIT),
        name="norm_matmul",
    )(x2d, g.reshape(1, k), w_bf16)


SEG_NQ, SEG_DQ, SEG_IQ = 0, MIX_A, MIX_A + MIX_B
SEG_NKV = SEG_IQ + IDX_HEADS * IDX_DIM
SEG_DKV = SEG_NKV + 6 * NSA_GROUPS * HEAD_DIM
SEG_MISC = SEG_DKV + 2 * HEAD_DIM
PROJ_SMALL = SEG_MISC + LANE
PREP_TM = 256


def _in_proj_column_order():
    offs = np.concatenate([[0], np.cumsum(IN_SIZES)])
    seg = lambda i: np.arange(offs[i], offs[i + 1])
    nq, nkv, ngate, dq, dkv, iq, ik, iw, mg = (seg(i) for i in range(len(IN_SIZES)))
    small = np.concatenate([nq, dq, iq, nkv, dkv, ik, iw, ngate])
    return small, mg


def _rope_lane_constants():
    half = ROT_DIM // 2
    d = np.arange(LANE) % HEAD_DIM
    inv = np.where(d < ROT_DIM, ROPE_THETA ** (-(d % half) / half), 0.0)
    sign = np.where(d < half, -1.0, 1.0)
    return jnp.asarray(np.stack([inv, sign]), jnp.float32)


def _rope_tables(pos_col, rope_const):
    ang = pos_col.astype(jnp.float32) * rope_const[0:1, :]
    return jnp.cos(ang), jnp.sin(ang) * rope_const[1:2, :]


def _rope_apply(x, cos, sin):
    half = ROT_DIM // 2
    w = x.shape[-1]
    d = lax.broadcasted_iota(jnp.int32, (1, w), 1) % HEAD_DIM
    partner = jnp.where(d < half, pltpu.roll(x, w - half, 1), pltpu.roll(x, half, 1))
    return x * cos + partner * sin


def _head_sumsq(x, ones_bd):
    sq = x * x
    hi = sq.astype(jnp.bfloat16)
    lo = (sq - hi.astype(jnp.float32)).astype(jnp.bfloat16)
    w = x.shape[-1]
    bd = ones_bd[:w, :w]
    return (jnp.dot(hi, bd, preferred_element_type=jnp.float32)
            + jnp.dot(lo, bd, preferred_element_type=jnp.float32))


def _head_norm(x, gain, ones_bd):
    return x * lax.rsqrt(_head_sumsq(x, ones_bd) * (1.0 / HEAD_DIM) + EPS) * gain


def _prep_kernel(proj_ref, pos_ref, rc_ref, gq_ref, gdq_ref, gkv_ref, gkb_ref, bd_ref,
                 qa_ref, qb_ref, iq_ref, kvc_ref, ks_ref, vs_ref, kw_ref, vw_ref, kbvb_ref, misc_ref):
    bf16 = jnp.bfloat16
    bd = bd_ref[...]
    cos1, sin1 = _rope_tables(pos_ref[...], rc_ref[...])
    cos4, sin4 = jnp.tile(cos1, (1, 4)), jnp.tile(sin1, (1, 4))
    q_scale = HEAD_DIM ** -0.5

    nq = proj_ref[:, SEG_NQ:SEG_NQ + MIX_A]
    qa_ref[...] = (_rope_apply(_head_norm(nq, gq_ref[...], bd), cos4, sin4) * q_scale).astype(bf16)
    dq = proj_ref[:, SEG_DQ:SEG_DQ + MIX_B]
    qb_ref[...] = (_rope_apply(_head_norm(dq, gdq_ref[...], bd), cos4, sin4) * q_scale).astype(bf16)
    iq = proj_ref[:, SEG_IQ:SEG_IQ + IDX_HEADS * IDX_DIM]
    iq_ref[...] = (_rope_apply(iq, cos4, sin4) * IDX_DIM ** -0.5).astype(bf16)

    grp = NSA_GROUPS * HEAD_DIM
    kvc_ref[...] = proj_ref[:, SEG_NKV:SEG_NKV + 2 * grp]
    ks = proj_ref[:, SEG_NKV + 2 * grp:SEG_NKV + 3 * grp]
    ks_ref[...] = _rope_apply(_head_norm(ks, gkv_ref[0:1, :], bd), cos1, sin1).astype(bf16)
    vs_ref[...] = proj_ref[:, SEG_NKV + 3 * grp:SEG_NKV + 4 * grp].astype(bf16)
    kw = proj_ref[:, SEG_NKV + 4 * grp:SEG_NKV + 5 * grp]
    kw_ref[...] = _rope_apply(_head_norm(kw, gkv_ref[1:2, :], bd), cos1, sin1).astype(bf16)
    vw_ref[...] = proj_ref[:, SEG_NKV + 5 * grp:SEG_NKV + 6 * grp].astype(bf16)

    lane = lax.broadcasted_iota(jnp.int32, (1, LANE), 1)
    dkv = proj_ref[:, SEG_DKV:SEG_DKV + LANE]
    kb = _rope_apply(_head_norm(dkv, gkb_ref[...], bd), cos1, sin1)
    kbvb_ref[...] = jnp.where(lane < HEAD_DIM, kb, dkv).astype(bf16)

    misc = proj_ref[:, SEG_MISC:SEG_MISC + LANE]
    ik = _rope_apply(misc, cos1, sin1)
    misc_ref[...] = jnp.where(lane < IDX_DIM, ik,
                              jnp.where(lane < IDX_DIM + IDX_HEADS, misc * IDX_HEADS ** -0.5,
                                        jax.nn.sigmoid(misc)))


def prep_pallas(proj_small, pos_col, nsa_qk_gain, dsa_qk_gain):
    t = proj_small.shape[0]
    bf16 = jnp.bfloat16
    f32 = jnp.float32
    gq = jnp.tile(nsa_qk_gain[0], NSA_HEADS).reshape(1, MIX_A)
    gdq = jnp.tile(dsa_qk_gain[0], DSA_HEADS).reshape(1, MIX_B)
    gkv = jnp.stack([jnp.tile(nsa_qk_gain[2], NSA_GROUPS), jnp.tile(nsa_qk_gain[3], NSA_GROUPS)])
    gkb = jnp.tile(dsa_qk_gain[1], 2).reshape(1, LANE)
    head_of = np.arange(MIX_A) // HEAD_DIM
    bd = jnp.asarray(head_of[:, None] == head_of[None, :], bf16)
    rc = _rope_lane_constants()
    row = lambda w: pl.BlockSpec((PREP_TM, w), lambda i: (i, 0))
    full = lambda a: pl.BlockSpec(a.shape, lambda i: (0,) * a.ndim)
    widths = [MIX_A, MIX_B, IDX_HEADS * IDX_DIM, 2 * LANE, LANE, LANE, LANE, LANE, LANE, LANE]
    dtypes = [bf16, bf16, bf16, f32, bf16, bf16, bf16, bf16, bf16, f32]
    return pl.pallas_call(
        _prep_kernel,
        grid=(t // PREP_TM,),
        in_specs=[row(PROJ_SMALL), row(1), full(rc), full(gq), full(gdq), full(gkv), full(gkb),
                  full(bd)],
        out_specs=[row(w) for w in widths],
        out_shape=[jax.ShapeDtypeStruct((t, w), dt) for w, dt in zip(widths, dtypes)],
        compiler_params=pltpu.CompilerParams(
            dimension_semantics=("parallel",), vmem_limit_bytes=VMEM_LIMIT),
        name="proj_prep",
    )(proj_small, pos_col, rc, gq, gdq, gkv, gkb, bd)


def _compress_kernel(flat_ref, pe_ref, w1_ref, w2_ref, pos_ref, rc_ref, gain_ref, bd_ref, o_ref):
    bf16 = jnp.bfloat16
    f32 = jnp.float32
    outs = []
    for g in range(NSA_GROUPS):
        xin = (flat_ref[0, 0, g] + pe_ref[0]).astype(bf16)
        hid = jax.nn.gelu(jnp.dot(xin, w1_ref[0], preferred_element_type=f32))
        outs.append(jnp.dot(hid.astype(bf16), w2_ref[0], preferred_element_type=f32))
    out = jnp.concatenate(outs, axis=-1)

    @pl.when(pl.program_id(0) == 0)
    def _():
        cos, sin = _rope_tables(pos_ref[0], rc_ref[...])
        o_ref[0, 0] = _rope_apply(_head_norm(out, gain_ref[...], bd_ref[...]), cos, sin)

    @pl.when(pl.program_id(0) != 0)
    def _():
        o_ref[0, 0] = out


def compress_pallas(kvc, positions, cmp_pos, cmp_w1, cmp_w2, k_gain):
    b, s, _ = kvc.shape
    bf16 = jnp.bfloat16
    n_chunk = s // CMP_STRIDE
    n_cmp = n_chunk - CMP_LEN // CMP_STRIDE + 1
    n_pad = -(-n_cmp // LANE) * LANE
    c = kvc.reshape(b, n_chunk, CMP_STRIDE, 2, NSA_GROUPS, HEAD_DIM).transpose(3, 0, 4, 1, 2, 5)
    c = c.reshape(2, b, NSA_GROUPS, n_chunk, CMP_STRIDE * HEAD_DIM)
    flat = jnp.concatenate([c[..., j:j + n_cmp, :] for j in range(CMP_LEN // CMP_STRIDE)], axis=-1)
    flat = jnp.pad(flat, ((0, 0),) * 3 + ((0, n_pad - n_cmp), (0, 0)))
    pe = cmp_pos.reshape(2, 1, CMP_LEN * HEAD_DIM)
    pos_c = positions[:, CMP_LEN - 1::CMP_STRIDE][:, :n_cmp]
    pos_c = jnp.pad(pos_c, ((0, 0), (0, n_pad - n_cmp))).reshape(b, n_pad, 1)
    gain = jnp.tile(k_gain, NSA_GROUPS).reshape(1, LANE)
    head_of = np.arange(LANE) // HEAD_DIM
    bd = jnp.asarray(head_of[:, None] == head_of[None, :], bf16)
    kdim = CMP_LEN * HEAD_DIM
    return pl.pallas_call(
        _compress_kernel,
        grid=(2, b),
        in_specs=[
            pl.BlockSpec((1, 1, NSA_GROUPS, n_pad, kdim), lambda w, bi: (w, bi, 0, 0, 0)),
            pl.BlockSpec((1, 1, kdim), lambda w, bi: (w, 0, 0)),
            pl.BlockSpec((1, kdim, CMP_HIDDEN), lambda w, bi: (w, 0, 0)),
            pl.BlockSpec((1, CMP_HIDDEN, HEAD_DIM), lambda w, bi: (w, 0, 0)),
            pl.BlockSpec((1, n_pad, 1), lambda w, bi: (bi, 0, 0)),
            pl.BlockSpec((2, LANE), lambda w, bi: (0, 0)),
            pl.BlockSpec((1, LANE), lambda w, bi: (0, 0)),
            pl.BlockSpec((LANE, LANE), lambda w, bi: (0, 0)),
        ],
        out_specs=pl.BlockSpec((1, 1, n_pad, LANE), lambda w, bi: (w, bi, 0, 0)),
        out_shape=jax.ShapeDtypeStruct((2, b, n_pad, LANE), jnp.float32),
        compiler_params=pltpu.CompilerParams(
            dimension_semantics=("arbitrary", "arbitrary"), vmem_limit_bytes=VMEM_LIMIT),
        name="compress",
    )(flat, pe, cmp_w1.astype(bf16), cmp_w2.astype(bf16), pos_c, _rope_lane_constants(), gain, bd)


ROW_TM = 512


def _merge_kernel(x_ref, ya_ref, yb_ref, mg_ref, wa_ref, wb_ref, wo_ref, o_ref):
    f32 = jnp.float32
    a = jnp.dot(ya_ref[...], wa_ref[...], preferred_element_type=f32)
    b = jnp.dot(yb_ref[...], wb_ref[...], preferred_element_type=f32)
    g = jax.nn.sigmoid(mg_ref[...])
    mix = g[:, :D_MODEL] * a + g[:, D_MODEL:] * b
    o_ref[...] = x_ref[...] + jnp.dot(mix.astype(jnp.bfloat16), wo_ref[...],
                                      preferred_element_type=f32)


def merge_pallas(x2d, ya, yb, mg, wa, wb, wo):
    t, d = x2d.shape
    bf16 = jnp.bfloat16
    row = lambda w: pl.BlockSpec((ROW_TM, w), lambda i: (i, 0))
    full = lambda r, c: pl.BlockSpec((r, c), lambda i: (0, 0))
    return pl.pallas_call(
        _merge_kernel,
        grid=(t // ROW_TM,),
        in_specs=[row(d), row(MIX_A), row(MIX_B), row(2 * d),
                  full(MIX_A, d), full(MIX_B, d), full(d, d)],
        out_specs=row(d),
        out_shape=jax.ShapeDtypeStruct((t, d), jnp.float32),
        compiler_params=pltpu.CompilerParams(
            dimension_semantics=("parallel",), vmem_limit_bytes=VMEM_LIMIT),
        name="merge",
    )(x2d, ya, yb, mg, wa.astype(bf16), wb.astype(bf16), wo.astype(bf16))


def _ple_kernel(x_ref, p_ref, wg_ref, wp_ref, gn_ref, o_ref):
    f32 = jnp.float32
    bf16 = jnp.bfloat16
    x = x_ref[...]
    r = x * lax.rsqrt(jnp.mean(x * x, axis=-1, keepdims=True) + EPS)
    gate = jax.nn.sigmoid(jnp.dot(r.astype(bf16), wg_ref[...], preferred_element_type=f32))
    e = jnp.dot(p_ref[...].astype(bf16), wp_ref[...], preferred_element_type=f32)
    e = e * lax.rsqrt(jnp.mean(e * e, axis=-1, keepdims=True) + EPS) * gn_ref[...]
    o_ref[...] = x + gate * e


def ple_pallas(x2d, p2d, wg, wp, gn):
    t, d = x2d.shape
    bf16 = jnp.bfloat16
    row = lambda w: pl.BlockSpec((ROW_TM, w), lambda i: (i, 0))
    full = lambda r, c: pl.BlockSpec((r, c), lambda i: (0, 0))
    return pl.pallas_call(
        _ple_kernel,
        grid=(t // ROW_TM,),
        in_specs=[row(d), row(PLE_DIM), full(d, d), full(PLE_DIM, d), full(1, d)],
        out_specs=row(d),
        out_shape=jax.ShapeDtypeStruct((t, d), jnp.float32),
        compiler_params=pltpu.CompilerParams(
            dimension_semantics=("parallel",), vmem_limit_bytes=VMEM_LIMIT),
        name="ple",
    )(x2d, p2d, wg.astype(bf16), wp.astype(bf16), gn.reshape(1, d))


NSA_TQ = 128
NSA_TK = 512
NSA_TW = 128
NSA_WIN_TILES = (WINDOW + NSA_TQ) // NSA_TW
NSA_SUB = 128


def _flash_update_t(state, s, v_t, mask=None):
    m, l, acc = state
    m_new = jnp.maximum(m, jnp.max(s, axis=0, keepdims=True))
    alpha = jnp.exp(m - m_new)
    p = jnp.exp(s - m_new)
    if mask is not None:
        p = jnp.where(mask, p, 0.0)
    l = alpha * l + jnp.sum(p, axis=0, keepdims=True)
    acc = alpha * acc + jnp.dot(v_t, p.astype(jnp.bfloat16), preferred_element_type=jnp.float32)
    return m_new, l, acc


def _nsa_kernel(q_ref, kc_ref, vct_ref, ks_ref, vst_ref, kw_ref, vwt_ref, g_ref, ov_ref, exp_ref,
                o_ref, bias_ref, *, n_blk, n_top, n_ktiles):
    f32 = jnp.float32
    bf16 = jnp.bfloat16
    qi = pl.program_id(2)
    q0 = qi * NSA_TQ
    t_row = q0 + lax.broadcasted_iota(jnp.int32, (1, NSA_TQ), 1)

    n_cmp_pad = kc_ref.shape[-2]
    cmp_end = lax.broadcasted_iota(jnp.int32, (n_cmp_pad, 1), 0) * CMP_STRIDE + (CMP_LEN - 1)
    mask_c = cmp_end <= t_row
    kc = kc_ref[0, 0]
    vc_t = vct_ref[0, 0]
    p_sum = jnp.zeros((n_cmp_pad, NSA_TQ), f32)
    o_c = []
    for r in range(NSA_REP):
        s = jnp.dot(kc, q_ref[0, r], preferred_element_type=f32)
        s = jnp.where(mask_c, s, NEG)
        m = jnp.max(s, axis=0, keepdims=True)
        p = jnp.where(mask_c, jnp.exp(s - m), 0.0)
        l = jnp.sum(p, axis=0, keepdims=True)
        p = p / jnp.where(l > 0.0, l, 1.0)
        p_sum = p_sum + p
        o_c.append(jnp.dot(vc_t, p.astype(bf16), preferred_element_type=f32))

    p_hi = p_sum.astype(bf16)
    p_lo = (p_sum - p_hi.astype(f32)).astype(bf16)
    ov = ov_ref[...]
    imp_t = (jnp.dot(ov, p_hi, preferred_element_type=f32)
             + jnp.dot(ov, p_lo, preferred_element_type=f32))
    blk = lax.broadcasted_iota(jnp.int32, (n_blk, 1), 0)
    cur = t_row // SEL_LEN
    forced = (blk == 0) | (blk == cur) | (blk == cur - 1)
    admissible = blk * SEL_LEN <= t_row
    imp_t = jnp.where(forced, jnp.inf, imp_t)
    imp_t = jnp.where(admissible, imp_t, -jnp.inf)
    n_chunks = n_blk // SUBLANE
    chunks = [imp_t[c * SUBLANE:(c + 1) * SUBLANE, :] for c in range(n_chunks)]
    ranks = [jnp.zeros((SUBLANE, NSA_TQ), f32) for _ in range(n_chunks)]
    sub = lax.broadcasted_iota(jnp.int32, (SUBLANE, NSA_TQ), 0)
    for i in range(n_blk):
        ci, si = divmod(i, SUBLANE)
        row = jnp.broadcast_to(chunks[ci][si:si + 1, :], (SUBLANE, NSA_TQ))
        for c in range(n_chunks):
            if c > ci:
                beats = jnp.where(row >= chunks[c], 1.0, 0.0)
            elif c < ci:
                beats = jnp.where(row > chunks[c], 1.0, 0.0)
            else:
                tie = jnp.where(sub > si, 1.0, 0.0)
                beats = jnp.where(row > chunks[c], 1.0, jnp.where(row == chunks[c], tie, 0.0))
            ranks[c] = ranks[c] + beats
    rank = jnp.concatenate(ranks, axis=0)
    sel_t = jnp.where((rank < n_top) & admissible, 1.0, 0.0).astype(bf16)

    n_live = qi // (NSA_TK // NSA_TQ) + 1
    for j in range(n_ktiles):
        @pl.when(j < n_live)
        def _():
            hit = jnp.dot(exp_ref[j * NSA_TK:(j + 1) * NSA_TK, :], sel_t,
                          preferred_element_type=f32)
            kpos = j * NSA_TK + lax.broadcasted_iota(jnp.int32, (NSA_TK, 1), 0)
            bias_ref[j] = jnp.where((hit > 0.5) & (kpos <= t_row), 0.0, NEG)

    def init_state():
        return (jnp.full((1, NSA_TQ), NEG, f32), jnp.zeros((1, NSA_TQ), f32),
                jnp.zeros((HEAD_DIM, NSA_TQ), f32))

    def sel_body(j, states):
        states = list(states)
        for c in range(NSA_TK // NSA_SUB):
            rows = slice(c * NSA_SUB, (c + 1) * NSA_SUB)
            k = ks_ref[0, 0, j, rows, :]
            v_t = vst_ref[0, 0, j, :, rows]
            bias = bias_ref[j, rows, :]
            for r in range(NSA_REP):
                s = jnp.dot(k, q_ref[0, r], preferred_element_type=f32) + bias
                states[r] = _flash_update_t(states[r], s, v_t)
        return tuple(states)

    st_s = lax.fori_loop(0, n_live, sel_body, tuple(init_state() for _ in range(NSA_REP)))

    st_w = [init_state() for _ in range(NSA_REP)]
    for w in range(NSA_WIN_TILES):
        tile = qi - (NSA_WIN_TILES - 1) + w
        tix = jnp.maximum(tile, 0)
        k = kw_ref[0, 0, tix]
        v_t = vwt_ref[0, 0, tix]
        kpos = tile * NSA_TW + lax.broadcasted_iota(jnp.int32, (NSA_TW, 1), 0)
        diff = t_row - kpos
        mask = (kpos >= 0) & (diff >= 0) & (diff < WINDOW)
        for r in range(NSA_REP):
            s = jnp.where(mask, jnp.dot(k, q_ref[0, r], preferred_element_type=f32), NEG)
            st_w[r] = _flash_update_t(st_w[r], s, v_t, mask)

    g = g_ref[0, 0]
    outs = []
    for r in range(NSA_REP):
        o_s = st_s[r][2] / st_s[r][1]
        o_w = st_w[r][2] / st_w[r][1]
        outs.append(g[r:r + 1, :] * o_c[r] + g[NSA_REP + r:NSA_REP + r + 1, :] * o_s
                    + g[2 * NSA_REP + r:2 * NSA_REP + r + 1, :] * o_w)
    o_ref[0] = jnp.concatenate(outs, axis=0).T.astype(o_ref.dtype)


def nsa_attention_pallas(q, kc, vc, ks, vs, kw, vw, gates):
    b, s = q.shape[0], q.shape[1]
    bf16 = jnp.bfloat16
    n_cmp = s // CMP_STRIDE - CMP_LEN // CMP_STRIDE + 1
    n_cmp_pad = -(-n_cmp // LANE) * LANE
    n_blk = s // SEL_LEN
    n_top = min(SEL_BLOCKS, n_blk)
    n_kt = s // NSA_TK
    n_wt = s // NSA_TW
    assert s % NSA_TK == 0 and n_blk % SUBLANE == 0

    pad_c = ((0, 0), (0, n_cmp_pad - kc.shape[1]), (0, 0), (0, 0))
    kcp = jnp.pad(kc, pad_c).transpose(0, 2, 1, 3).astype(bf16)
    vct = jnp.pad(vc, pad_c).transpose(0, 2, 3, 1).astype(bf16)

    def key_tiles(k, tk):
        return k.reshape(b, s // tk, tk, NSA_GROUPS, HEAD_DIM).transpose(0, 3, 1, 2, 4).astype(bf16)

    def val_tiles(v, tk):
        return v.reshape(b, s // tk, tk, NSA_GROUPS, HEAD_DIM).transpose(0, 3, 1, 4, 2).astype(bf16)

    gates_t = gates.transpose(0, 3, 2, 4, 1).reshape(b, NSA_GROUPS, 3 * NSA_REP, s)
    q_t = q.reshape(b, s, NSA_HEADS, HEAD_DIM).transpose(0, 2, 3, 1)
    grp = NSA_REP * HEAD_DIM

    ci = np.arange(n_cmp_pad)[None, :] * CMP_STRIDE
    sj = np.arange(n_blk)[:, None] * SEL_LEN
    ov = (ci < sj + SEL_LEN) & (ci + CMP_LEN > sj) & (np.arange(n_cmp_pad)[None, :] < n_cmp)
    expand = (np.arange(s)[:, None] // SEL_LEN) == np.arange(n_blk)[None, :]

    per_bg = lambda bi, gi, i: (bi, gi, 0, 0)
    per_bg5 = lambda bi, gi, i: (bi, gi, 0, 0, 0)
    per_q = lambda bi, gi, i: (bi, gi, 0, i)
    return pl.pallas_call(
        functools.partial(_nsa_kernel, n_blk=n_blk, n_top=n_top, n_ktiles=n_kt),
        grid=(b, NSA_GROUPS, s // NSA_TQ),
        in_specs=[
            pl.BlockSpec((1, NSA_REP, HEAD_DIM, NSA_TQ), per_q),
            pl.BlockSpec((1, 1, n_cmp_pad, HEAD_DIM), per_bg),
            pl.BlockSpec((1, 1, HEAD_DIM, n_cmp_pad), per_bg),
            pl.BlockSpec((1, 1, n_kt, NSA_TK, HEAD_DIM), per_bg5),
            pl.BlockSpec((1, 1, n_kt, HEAD_DIM, NSA_TK), per_bg5),
            pl.BlockSpec((1, 1, n_wt, NSA_TW, HEAD_DIM), per_bg5),
            pl.BlockSpec((1, 1, n_wt, HEAD_DIM, NSA_TW), per_bg5),
            pl.BlockSpec((1, 1, 3 * NSA_REP, NSA_TQ), per_q),
            pl.BlockSpec((n_blk, n_cmp_pad), lambda bi, gi, i: (0, 0)),
            pl.BlockSpec((s, n_blk), lambda bi, gi, i: (0, 0)),
        ],
        out_specs=pl.BlockSpec((1, NSA_TQ, grp), lambda bi, gi, i: (bi, i, gi)),
        out_shape=jax.ShapeDtypeStruct((b, s, MIX_A), bf16),
        scratch_shapes=[pltpu.VMEM((n_kt, NSA_TK, NSA_TQ), jnp.float32)],
        compiler_params=pltpu.CompilerParams(
            dimension_semantics=("parallel", "parallel", "arbitrary"),
            vmem_limit_bytes=VMEM_LIMIT),
        name="nsa_attention",
    )(q_t, kcp, vct, key_tiles(ks, NSA_TK), val_tiles(vs, NSA_TK), key_tiles(kw, NSA_TW),
      val_tiles(vw, NSA_TW), gates_t, jnp.asarray(ov, bf16), jnp.asarray(expand, bf16))


DSA_TQ = 128
DSA_TK = 512
DSA_SUB = 128
F32_ORDER_MASK = 0x7FFFFFFF
F32_NEG_INF_BITS = -8388608
F32_POS_INF_BITS = 0x7F800000


def _ordered_from_bits(bits):
    return jnp.where(bits >= 0, bits, bits ^ F32_ORDER_MASK)


def _sublane_group_sum(x):
    ways = 4
    g = x.reshape(ways, x.shape[0] // (SUBLANE * ways), SUBLANE, x.shape[1])
    return jnp.sum(jnp.sum(g, axis=1), axis=0)


def _dsa_kernel(iq_ref, w_ref, ik_ref, q_ref, k_ref, vt_ref, tril_ref, o_ref, sc_ref, *, n_keep):
    f32 = jnp.float32
    bf16 = jnp.bfloat16
    qi = pl.program_id(1)
    q0 = qi * DSA_TQ
    t_row = q0 + lax.broadcasted_iota(jnp.int32, (1, DSA_TQ), 1)
    n_live = qi // (DSA_TK // DSA_TQ) + 1
    key_off = lax.broadcasted_iota(jnp.int32, (DSA_TK, 1), 0)

    w = w_ref[0]

    def score_body(j, carry):
        ik = ik_ref[0, j]
        acc = jnp.zeros((DSA_TK, DSA_TQ), f32)
        for h in range(IDX_HEADS):
            logit = jnp.dot(ik, iq_ref[0, h], preferred_element_type=f32)
            acc = acc + jnp.maximum(logit, 0.0) * w[h:h + 1, :]
        causal = (j * DSA_TK + key_off) <= t_row
        sc_ref[j] = jnp.where(causal, acc + 0.0, -jnp.inf)
        return carry

    lax.fori_loop(0, n_live, score_body, 0)

    def count(pred):
        def body(j, acc):
            return acc + _sublane_group_sum(jnp.where(pred(sc_ref[j]), 1.0, 0.0))
        part = lax.fori_loop(0, n_live, body, jnp.zeros((SUBLANE, DSA_TQ), f32))
        return jnp.sum(part, axis=0, keepdims=True)

    def bisect_body(_, lohi):
        lo, hi = lohi
        mid = (lo >> 1) + (hi >> 1) + (lo & hi & 1)
        thr = lax.bitcast_convert_type(_ordered_from_bits(mid), f32)
        ok = count(lambda sc: sc >= thr) >= n_keep
        return jnp.where(ok, mid, lo), jnp.where(ok, hi, mid)

    lo0 = jnp.full((1, DSA_TQ), F32_NEG_INF_BITS ^ F32_ORDER_MASK, jnp.int32)
    hi0 = jnp.full((1, DSA_TQ), F32_POS_INF_BITS + 1, jnp.int32)
    lo, _ = lax.fori_loop(0, 32, bisect_body, (lo0, hi0))
    thr = lax.bitcast_convert_type(_ordered_from_bits(lo), f32)
    need = n_keep - count(lambda sc: sc > thr)

    tril = tril_ref[...]

    def mask_body(j, ties_before):
        sc = sc_ref[j]
        eq = sc == thr
        pref = ties_before + jnp.dot(tril, jnp.where(eq, 1.0, 0.0).astype(bf16),
                                     preferred_element_type=f32)
        keep_tie = jnp.where(pref <= need, 0.0, NEG)
        bias = jnp.where(sc > thr, 0.0, jnp.where(eq, keep_tie, NEG))
        causal = (j * DSA_TK + key_off) <= t_row
        sc_ref[j] = jnp.where(causal, bias, NEG)
        return pref[DSA_TK - 1:DSA_TK, :]

    lax.fori_loop(0, n_live, mask_body, jnp.zeros((1, DSA_TQ), f32))

    def att_body(j, states):
        states = list(states)
        for c in range(DSA_TK // DSA_SUB):
            rows = slice(c * DSA_SUB, (c + 1) * DSA_SUB)
            k = k_ref[0, j, rows, :]
            v_t = vt_ref[0, j, :, rows]
            bias = sc_ref[j, rows, :]
            for h in range(DSA_HEADS):
                s = jnp.dot(k, q_ref[0, h], preferred_element_type=f32) + bias
                states[h] = _flash_update_t(states[h], s, v_t)
        return tuple(states)

    init = tuple((jnp.full((1, DSA_TQ), NEG, f32), jnp.zeros((1, DSA_TQ), f32),
                  jnp.zeros((HEAD_DIM, DSA_TQ), f32)) for _ in range(DSA_HEADS))
    st = lax.fori_loop(0, n_live, att_body, init)
    out_t = jnp.concatenate([st[h][2] / st[h][1] for h in range(DSA_HEADS)], axis=0)
    o_ref[0] = out_t.T.astype(o_ref.dtype)


def dsa_attention_pallas(q2, k, v, iq2, ik, w):
    b, s = q2.shape[0], q2.shape[1]
    bf16 = jnp.bfloat16
    n_keep = min(DSA_TOPK, s // 4)
    n_kt = s // DSA_TK
    assert s % DSA_TK == 0

    def key_tiles(x):
        return x.reshape(b, n_kt, DSA_TK, x.shape[-1]).astype(bf16)

    v_t = v.reshape(b, n_kt, DSA_TK, HEAD_DIM).transpose(0, 1, 3, 2).astype(bf16)
    tril = np.arange(DSA_TK)[:, None] >= np.arange(DSA_TK)[None, :]

    def heads_t(x, nh, dh):
        return x.reshape(b, s, nh, dh).transpose(0, 2, 3, 1)

    per_b = lambda bi, i: (bi, 0, 0, 0)
    per_q = lambda bi, i: (bi, 0, 0, i)
    q_rows = lambda width: pl.BlockSpec((1, DSA_TQ, width), lambda bi, i: (bi, i, 0))
    return pl.pallas_call(
        functools.partial(_dsa_kernel, n_keep=n_keep),
        grid=(b, s // DSA_TQ),
        in_specs=[
            pl.BlockSpec((1, IDX_HEADS, IDX_DIM, DSA_TQ), per_q),
            pl.BlockSpec((1, IDX_HEADS, DSA_TQ), lambda bi, i: (bi, 0, i)),
            pl.BlockSpec((1, n_kt, DSA_TK, IDX_DIM), per_b),
            pl.BlockSpec((1, DSA_HEADS, HEAD_DIM, DSA_TQ), per_q),
            pl.BlockSpec((1, n_kt, DSA_TK, HEAD_DIM), per_b),
            pl.BlockSpec((1, n_kt, HEAD_DIM, DSA_TK), per_b),
            pl.BlockSpec((DSA_TK, DSA_TK), lambda bi, i: (0, 0)),
        ],
        out_specs=q_rows(MIX_B),
        out_shape=jax.ShapeDtypeStruct((b, s, MIX_B), bf16),
        scratch_shapes=[pltpu.VMEM((n_kt, DSA_TK, DSA_TQ), jnp.float32)],
        compiler_params=pltpu.CompilerParams(
            dimension_semantics=("parallel", "arbitrary"),
            vmem_limit_bytes=VMEM_LIMIT),
        name="dsa_attention",
    )(heads_t(iq2, IDX_HEADS, IDX_DIM), w.transpose(0, 2, 1), key_tiles(ik),
      heads_t(q2, DSA_HEADS, HEAD_DIM), key_tiles(k), v_t, jnp.asarray(tril, bf16))


PEER_TT = 128
PEER_SLOTS = PEER_HEADS * PEER_TOPK
PEER_CAND_A0 = PEER_HALF_TOPK
PEER_CAND_SQ = SUBLANE


def _peer_cand_flat_ids():
    ids = [0 * PEER_HALF_TOPK + bb for bb in range(PEER_CAND_A0)]
    for a in range(1, PEER_CAND_SQ):
        ids += [a * PEER_HALF_TOPK + bb for bb in range(PEER_CAND_SQ)]
    ids += [a * PEER_HALF_TOPK for a in range(PEER_CAND_SQ, PEER_HALF_TOPK)]
    return np.asarray(ids, np.int32)


def _extract_top(cur, ids, n):
    vals, picks = [], []
    for _ in range(n):
        m = jnp.max(cur, axis=0, keepdims=True)
        pick = jnp.min(jnp.where(cur == m, ids, jnp.int32(2 ** 30)), axis=0, keepdims=True)
        vals.append(m)
        picks.append(pick)
        cur = jnp.where(ids == pick, -jnp.inf, cur)
    return vals, picks


def _pair_grid(first, second, op):
    pieces = [op(first[0:1], second)]
    for a in range(1, PEER_CAND_SQ):
        pieces.append(op(first[a:a + 1], second[0:PEER_CAND_SQ]))
    pieces.append(op(first[PEER_CAND_SQ:], second[0:1]))
    return jnp.concatenate(pieces, axis=0)


def _peer_topk_kernel(x_ref, g_ref, wq_ref, sk_ref, fid_ref, h_ref, eidx_ref, gate_ref):
    f32 = jnp.float32
    bf16 = jnp.bfloat16
    x = x_ref[...]
    h = x * lax.rsqrt(jnp.mean(x * x, axis=-1, keepdims=True) + EPS) * g_ref[...]
    h_ref[...] = h
    q = jnp.dot(h.astype(bf16), wq_ref[...], preferred_element_type=f32).astype(bf16)
    key_ids = lax.broadcasted_iota(jnp.int32, (N_KEYS, PEER_TT), 0)
    fid = fid_ref[...]
    nt_dims = (((1,), (1,)), ((), ()))
    half = PEER_QDIM // 2
    for hd in range(PEER_HEADS):
        tops = []
        for c in range(2):
            col = (hd * 2 + c) * half
            s_t = lax.dot_general(sk_ref[c], q[:, col:col + half], nt_dims,
                                  preferred_element_type=f32)
            vals, picks = _extract_top(s_t, key_ids, PEER_HALF_TOPK)
            tops.append((jnp.concatenate(vals, axis=0), jnp.concatenate(picks, axis=0)))
        (v1, i1), (v2, i2) = tops
        cand = _pair_grid(v1, v2, lambda a, b: a + b)
        cexp = _pair_grid(i1, i2, lambda a, b: a * N_KEYS + b)
        vals, picks = _extract_top(cand, fid, PEER_TOPK)
        top = jnp.concatenate(vals, axis=0)
        eids = [jnp.max(jnp.where(fid == p, cexp, -1), axis=0, keepdims=True) for p in picks]
        ex = jnp.exp(top - top[0:1])
        gate = ex / jnp.sum(ex, axis=0, keepdims=True)
        eidx_ref[0, hd * PEER_TOPK:(hd + 1) * PEER_TOPK, :] = jnp.concatenate(eids, axis=0)
        gate_ref[0, hd * PEER_TOPK:(hd + 1) * PEER_TOPK, :] = gate


def peer_topk_pallas(x2d, ffn_norm, wq, sub_keys):
    t, d = x2d.shape
    n_tiles = t // PEER_TT
    fid = np.broadcast_to(_peer_cand_flat_ids()[:, None], (_peer_cand_flat_ids().shape[0], PEER_TT))
    n_cand = fid.shape[0]
    return pl.pallas_call(
        _peer_topk_kernel,
        grid=(n_tiles,),
        in_specs=[
            pl.BlockSpec((PEER_TT, d), lambda i: (i, 0)),
            pl.BlockSpec((1, d), lambda i: (0, 0)),
            pl.BlockSpec((d, PEER_HEADS * PEER_QDIM), lambda i: (0, 0)),
            pl.BlockSpec((2, N_KEYS, PEER_QDIM // 2), lambda i: (0, 0, 0)),
            pl.BlockSpec((n_cand, PEER_TT), lambda i: (0, 0)),
        ],
        out_specs=[
            pl.BlockSpec((PEER_TT, d), lambda i: (i, 0)),
            pl.BlockSpec((1, PEER_SLOTS, PEER_TT), lambda i: (i, 0, 0)),
            pl.BlockSpec((1, PEER_SLOTS, PEER_TT), lambda i: (i, 0, 0)),
        ],
        out_shape=[
            jax.ShapeDtypeStruct((t, d), jnp.float32),
            jax.ShapeDtypeStruct((n_tiles, PEER_SLOTS, PEER_TT), jnp.int32),
            jax.ShapeDtypeStruct((n_tiles, PEER_SLOTS, PEER_TT), jnp.float32),
        ],
        compiler_params=pltpu.CompilerParams(
            dimension_semantics=("parallel",), vmem_limit_bytes=VMEM_LIMIT),
        name="peer_topk",
    )(x2d, ffn_norm.reshape(1, d), wq.astype(jnp.bfloat16), sub_keys.astype(jnp.bfloat16),
      jnp.asarray(fid))


PEER_GT = 8
PEER_ROWS = PEER_GT * PEER_SLOTS
PEER_NSLOT = 2
PEER_DMA_QUEUES = 2


def _peer_eval_kernel(idx_ref, h_ref, gate_ref, x_ref, uv_hbm, o_ref, buf, sem):
    j = pl.program_id(0)
    n_blocks = pl.num_programs(0) - 1
    lane = lax.broadcasted_iota(jnp.int32, (PEER_SLOTS, PEER_TT), 1)
    lanes_per_block = PEER_NSLOT * PEER_GT
    lane0 = ((j - 1) % (PEER_TT // lanes_per_block)) * lanes_per_block

    def evaluate(slot, tok):
        row0 = slot * PEER_GT + tok
        rows = buf[slot, tok * PEER_SLOTS:(tok + 1) * PEER_SLOTS, :]
        act = jnp.sum(rows[:, :D_MODEL] * h_ref[row0:row0 + 1, :], axis=-1, keepdims=True)
        gate = jnp.sum(jnp.where(lane == lane0 + row0, gate_ref[0], 0.0), axis=-1, keepdims=True)
        wgt = gate * jax.nn.gelu(act)
        o_ref[row0:row0 + 1, :] = (x_ref[row0:row0 + 1, :]
                                   + jnp.sum(wgt * rows[:, D_MODEL:], axis=0, keepdims=True))

    def issue(slot, tok):
        for r in range(tok * PEER_SLOTS, (tok + 1) * PEER_SLOTS):
            e = idx_ref[0, 0, slot * PEER_ROWS + r]
            pltpu.async_copy(uv_hbm.at[e], buf.at[slot, pl.ds(r, 1)], sem.at[slot],
                             priority=r % PEER_DMA_QUEUES)

    def step(do_evaluate, do_issue):
        for slot in range(PEER_NSLOT):
            if do_evaluate:
                pltpu.make_async_copy(uv_hbm.at[pl.ds(0, PEER_ROWS), 0], buf.at[slot],
                                      sem.at[slot]).wait()
            for tok in range(PEER_GT):
                if do_evaluate:
                    evaluate(slot, tok)
                if do_issue:
                    issue(slot, tok)

    @pl.when(j == 0)
    def _():
        step(False, True)

    @pl.when((j > 0) & (j < n_blocks))
    def _():
        step(True, True)

    @pl.when(j == n_blocks)
    def _():
        step(True, False)


def peer_eval_pallas(x2d, h2d, eidx_t, gate_t, u, v):
    t, d = x2d.shape
    blk = PEER_NSLOT * PEER_GT
    n_blocks = t // blk
    uv = jnp.concatenate([u, v], axis=1).reshape(u.shape[0], 1, 2 * d)
    idx = eidx_t.transpose(0, 2, 1).reshape(n_blocks, 1, PEER_NSLOT * PEER_ROWS)
    per_tile = PEER_TT // blk
    prev = lambda j: jnp.maximum(j - 1, 0)
    return pl.pallas_call(
        _peer_eval_kernel,
        grid=(n_blocks + 1,),
        in_specs=[
            pl.BlockSpec((1, 1, PEER_NSLOT * PEER_ROWS),
                         lambda j: (jnp.minimum(j, n_blocks - 1), 0, 0), memory_space=pltpu.SMEM),
            pl.BlockSpec((blk, d), lambda j: (prev(j), 0)),
            pl.BlockSpec((1, PEER_SLOTS, PEER_TT), lambda j: (prev(j) // per_tile, 0, 0)),
            pl.BlockSpec((blk, d), lambda j: (prev(j), 0)),
            pl.BlockSpec(memory_space=pl.ANY),
        ],
        out_specs=pl.BlockSpec((blk, d), lambda j: (prev(j), 0)),
        out_shape=jax.ShapeDtypeStruct((t, d), jnp.float32),
        scratch_shapes=[pltpu.VMEM((PEER_NSLOT, PEER_ROWS, 2 * d), jnp.float32),
                        pltpu.SemaphoreType.DMA((PEER_NSLOT,))],
        compiler_params=pltpu.CompilerParams(
            dimension_semantics=("arbitrary",), vmem_limit_bytes=VMEM_LIMIT),
        name="peer_eval",
    )(idx, h2d, gate_t, x2d, uv)


def peer_pallas(x, ffn_norm, wq, sub_keys, u, v):
    b, s, d = x.shape
    x2d = x.reshape(b * s, d)
    h2d, eidx_t, gate_t = peer_topk_pallas(x2d, ffn_norm, wq, sub_keys)
    return peer_eval_pallas(x2d, h2d, eidx_t, gate_t, u, v).reshape(b, s, d)


def hybrid_layer(x, p_i, positions, attn_norm, w_in, nsa_qk_gain, cmp_pos, cmp_w1, cmp_w2,
                 dsa_qk_gain, w_branch_a, w_branch_b, w_out, ffn_norm, peer_wq, peer_sub_keys,
                 peer_u, peer_v, ple_w, ple_gate_w, ple_norm):
    b, s, d = x.shape
    t = b * s
    bf16 = jnp.bfloat16
    x2d = x.reshape(t, d)

    small_cols, mg_cols = _in_proj_column_order()
    w_small = jnp.pad(w_in[:, small_cols], ((0, 0), (0, PROJ_SMALL - small_cols.size))).astype(bf16)
    proj_small = norm_matmul(x2d, attn_norm, w_small, tn=PROJ_SMALL // 5)
    mg = norm_matmul(x2d, attn_norm, w_in[:, mg_cols].astype(bf16))
    qa, qb, iq, kvc, ks, vs, kw, vw, kbvb, misc = prep_pallas(
        proj_small, positions.reshape(t, 1), nsa_qk_gain, dsa_qk_gain)

    cmp = compress_pallas(kvc.reshape(b, s, -1), positions, cmp_pos, cmp_w1, cmp_w2, nsa_qk_gain[1])
    per_group = lambda a: a.reshape(b, -1, NSA_GROUPS, HEAD_DIM)
    gates = misc[:, IDX_DIM + IDX_HEADS:IDX_DIM + IDX_HEADS + 3 * NSA_HEADS]
    ya = nsa_attention_pallas(qa.reshape(b, s, MIX_A), per_group(cmp[0]), per_group(cmp[1]),
                              per_group(ks), per_group(vs), per_group(kw), per_group(vw),
                              gates.reshape(b, s, 3, NSA_GROUPS, NSA_REP))

    kbvb = kbvb.reshape(b, s, 2, HEAD_DIM)
    yb = dsa_attention_pallas(qb.reshape(b, s, MIX_B), kbvb[:, :, 0], kbvb[:, :, 1],
                              iq.reshape(b, s, -1), misc[:, :IDX_DIM].reshape(b, s, IDX_DIM),
                              misc[:, IDX_DIM:IDX_DIM + IDX_HEADS].reshape(b, s, IDX_HEADS))

    x2d = merge_pallas(x2d, ya.reshape(t, MIX_A), yb.reshape(t, MIX_B), mg,
                       w_branch_a, w_branch_b, w_out)
    x2d = peer_pallas(x2d.reshape(b, s, d), ffn_norm, peer_wq, peer_sub_keys, peer_u, peer_v)
    x2d = ple_pallas(x2d.reshape(t, d), p_i.reshape(t, PLE_DIM), ple_gate_w, ple_w, ple_norm)
    return x2d.reshape(b, s, d)


def kernel(x, p, positions, attn_norm, w_in, nsa_qk_gain, cmp_pos, cmp_w1, cmp_w2,
           dsa_qk_gain, w_branch_a, w_branch_b, w_out, ffn_norm, peer_wq, peer_sub_keys,
           peer_u, peer_v, ple_w, ple_gate_w, ple_norm):
    for i in range(DEPTH):
        x = hybrid_layer(x, p[i], positions, attn_norm[i], w_in[i], nsa_qk_gain[i], cmp_pos[i],
                         cmp_w1[i], cmp_w2[i], dsa_qk_gain[i], w_branch_a[i], w_branch_b[i],
                         w_out[i], ffn_norm[i], peer_wq[i], peer_sub_keys[i], peer_u[i],
                         peer_v[i], ple_w[i], ple_gate_w[i], ple_norm[i])
    return x
```

```python
import functools

import numpy as np
import jax
import jax.numpy as jnp
from jax import lax
from jax.experimental import pallas as pl
from jax.experimental.pallas import tpu as pltpu

D_MODEL = 1024
DEPTH = 2

HEAD_DIM = 64
ROT_DIM = HEAD_DIM // 4
ROPE_THETA = 500000.0
NEG = -1e30
EPS = 1e-6

NSA_HEADS = 8
NSA_GROUPS = 2
NSA_REP = NSA_HEADS // NSA_GROUPS
CMP_LEN = 32
CMP_STRIDE = 16
CMP_HIDDEN = 256
SEL_LEN = 64
SEL_BLOCKS = 16
WINDOW = 512

DSA_HEADS = 8
IDX_HEADS = 8
IDX_DIM = 64
DSA_TOPK = 256

PEER_HEADS = 8
PEER_QDIM = 256
N_KEYS = 128
PEER_HALF_TOPK = 16
PEER_TOPK = 16

PLE_DIM = 256

MIX_A = NSA_HEADS * HEAD_DIM
MIX_B = DSA_HEADS * HEAD_DIM
IN_SIZES = (
    MIX_A,
    6 * NSA_GROUPS * HEAD_DIM,
    3 * NSA_HEADS,
    MIX_B,
    2 * HEAD_DIM,
    IDX_HEADS * IDX_DIM,
    IDX_DIM,
    IDX_HEADS,
    2 * D_MODEL,
)

LANE = 128
SUBLANE = 8
VMEM_LIMIT = 48 * 1024 * 1024


def _norm_matmul_kernel(x_ref, g_ref, w_ref, o_ref, h_ref):
    @pl.when(pl.program_id(1) == 0)
    def _():
        x = x_ref[...]
        h = x * lax.rsqrt(jnp.mean(x * x, axis=-1, keepdims=True) + EPS) * g_ref[...]
        h_ref[...] = h.astype(jnp.bfloat16)

    o_ref[...] = jnp.dot(h_ref[...], w_ref[...], preferred_element_type=jnp.float32)


def norm_matmul(x2d, g, w_bf16, *, tm=512, tn=512):
    m, k = x2d.shape
    n = w_bf16.shape[1]
    assert m % tm == 0 and n % tn == 0
    return pl.pallas_call(
        _norm_matmul_kernel,
        grid=(m // tm, n // tn),
        in_specs=[
            pl.BlockSpec((tm, k), lambda i, j: (i, 0)),
            pl.BlockSpec((1, k), lambda i, j: (0, 0)),
            pl.BlockSpec((k, tn), lambda i, j: (0, j)),
        ],
        out_specs=pl.BlockSpec((tm, tn), lambda i, j: (i, j)),
        out_shape=jax.ShapeDtypeStruct((m, n), jnp.float32),
        scratch_shapes=[pltpu.VMEM((tm, k), jnp.bfloat16)],
        compiler_params=pltpu.CompilerParams(
            dimension_semantics=("parallel", "arbitrary"),
            vmem_limit_bytes=VMEM_LIMIT),
        name="norm_matmul",
    )(x2d, g.reshape(1, k), w_bf16)


SEG_NQ, SEG_DQ, SEG_IQ = 0, MIX_A, MIX_A + MIX_B
SEG_NKV = SEG_IQ + IDX_HEADS * IDX_DIM
SEG_DKV = SEG_NKV + 6 * NSA_GROUPS * HEAD_DIM
SEG_MISC = SEG_DKV + 2 * HEAD_DIM
PROJ_SMALL = SEG_MISC + LANE
PREP_TM = 256


def _in_proj_column_order():
    offs = np.concatenate([[0], np.cumsum(IN_SIZES)])
    seg = lambda i: np.arange(offs[i], offs[i + 1])
    nq, nkv, ngate, dq, dkv, iq, ik, iw, mg = (seg(i) for i in range(len(IN_SIZES)))
    small = np.concatenate([nq, dq, iq, nkv, dkv, ik, iw, ngate])
    return small, mg


def _rope_lane_constants():
    half = ROT_DIM // 2
    d = np.arange(LANE) % HEAD_DIM
    inv = np.where(d < ROT_DIM, ROPE_THETA ** (-(d % half) / half), 0.0)
    sign = np.where(d < half, -1.0, 1.0)
    return jnp.asarray(np.stack([inv, sign]), jnp.float32)


def _rope_tables(pos_col, rope_const):
    ang = pos_col.astype(jnp.float32) * rope_const[0:1, :]
    return jnp.cos(ang), jnp.sin(ang) * rope_const[1:2, :]


def _rope_apply(x, cos, sin):
    half = ROT_DIM // 2
    w = x.shape[-1]
    d = lax.broadcasted_iota(jnp.int32, (1, w), 1) % HEAD_DIM
    partner = jnp.where(d < half, pltpu.roll(x, w - half, 1), pltpu.roll(x, half, 1))
    return x * cos + partner * sin


def _head_sumsq(x, ones_bd):
    sq = x * x
    hi = sq.astype(jnp.bfloat16)
    lo = (sq - hi.astype(jnp.float32)).astype(jnp.bfloat16)
    w = x.shape[-1]
    bd = ones_bd[:w, :w]
    return (jnp.dot(hi, bd, preferred_element_type=jnp.float32)
            + jnp.dot(lo, bd, preferred_element_type=jnp.float32))


def _head_norm(x, gain, ones_bd):
    return x * lax.rsqrt(_head_sumsq(x, ones_bd) * (1.0 / HEAD_DIM) + EPS) * gain


def _prep_kernel(proj_ref, pos_ref, rc_ref, gq_ref, gdq_ref, gkv_ref, gkb_ref, bd_ref,
                 qa_ref, qb_ref, iq_ref, kvc_ref, ks_ref, vs_ref, kw_ref, vw_ref, kbvb_ref, misc_ref):
    bf16 = jnp.bfloat16
    bd = bd_ref[...]
    cos1, sin1 = _rope_tables(pos_ref[...], rc_ref[...])
    cos4, sin4 = jnp.tile(cos1, (1, 4)), jnp.tile(sin1, (1, 4))
    q_scale = HEAD_DIM ** -0.5

    nq = proj_ref[:, SEG_NQ:SEG_NQ + MIX_A]
    qa_ref[...] = (_rope_apply(_head_norm(nq, gq_ref[...], bd), cos4, sin4) * q_scale).astype(bf16)
    dq = proj_ref[:, SEG_DQ:SEG_DQ + MIX_B]
    qb_ref[...] = (_rope_apply(_head_norm(dq, gdq_ref[...], bd), cos4, sin4) * q_scale).astype(bf16)
    iq = proj_ref[:, SEG_IQ:SEG_IQ + IDX_HEADS * IDX_DIM]
    iq_ref[...] = (_rope_apply(iq, cos4, sin4) * IDX_DIM ** -0.5).astype(bf16)

    grp = NSA_GROUPS * HEAD_DIM
    kvc_ref[...] = proj_ref[:, SEG_NKV:SEG_NKV + 2 * grp]
    ks = proj_ref[:, SEG_NKV + 2 * grp:SEG_NKV + 3 * grp]
    ks_ref[...] = _rope_apply(_head_norm(ks, gkv_ref[0:1, :], bd), cos1, sin1).astype(bf16)
    vs_ref[...] = proj_ref[:, SEG_NKV + 3 * grp:SEG_NKV + 4 * grp].astype(bf16)
    kw = proj_ref[:, SEG_NKV + 4 * grp:SEG_NKV + 5 * grp]
    kw_ref[...] = _rope_apply(_head_norm(kw, gkv_ref[1:2, :], bd), cos1, sin1).astype(bf16)
    vw_ref[...] = proj_ref[:, SEG_NKV + 5 * grp:SEG_NKV + 6 * grp].astype(bf16)

    lane = lax.broadcasted_iota(jnp.int32, (1, LANE), 1)
    dkv = proj_ref[:, SEG_DKV:SEG_DKV + LANE]
    kb = _rope_apply(_head_norm(dkv, gkb_ref[...], bd), cos1, sin1)
    kbvb_ref[...] = jnp.where(lane < HEAD_DIM, kb, dkv).astype(bf16)

    misc = proj_ref[:, SEG_MISC:SEG_MISC + LANE]
    ik = _rope_apply(misc, cos1, sin1)
    misc_ref[...] = jnp.where(lane < IDX_DIM, ik,
                              jnp.where(lane < IDX_DIM + IDX_HEADS, misc * IDX_HEADS ** -0.5,
                                        jax.nn.sigmoid(misc)))


def prep_pallas(proj_small, pos_col, nsa_qk_gain, dsa_qk_gain):
    t = proj_small.shape[0]
    bf16 = jnp.bfloat16
    f32 = jnp.float32
    gq = jnp.tile(nsa_qk_gain[0], NSA_HEADS).reshape(1, MIX_A)
    gdq = jnp.tile(dsa_qk_gain[0], DSA_HEADS).reshape(1, MIX_B)
    gkv = jnp.stack([jnp.tile(nsa_qk_gain[2], NSA_GROUPS), jnp.tile(nsa_qk_gain[3], NSA_GROUPS)])
    gkb = jnp.tile(dsa_qk_gain[1], 2).reshape(1, LANE)
    head_of = np.arange(MIX_A) // HEAD_DIM
    bd = jnp.asarray(head_of[:, None] == head_of[None, :], bf16)
    rc = _rope_lane_constants()
    row = lambda w: pl.BlockSpec((PREP_TM, w), lambda i: (i, 0))
    full = lambda a: pl.BlockSpec(a.shape, lambda i: (0,) * a.ndim)
    widths = [MIX_A, MIX_B, IDX_HEADS * IDX_DIM, 2 * LANE, LANE, LANE, LANE, LANE, LANE, LANE]
    dtypes = [bf16, bf16, bf16, f32, bf16, bf16, bf16, bf16, bf16, f32]
    return pl.pallas_call(
        _prep_kernel,
        grid=(t // PREP_TM,),
        in_specs=[row(PROJ_SMALL), row(1), full(rc), full(gq), full(gdq), full(gkv), full(gkb),
                  full(bd)],
        out_specs=[row(w) for w in widths],
        out_shape=[jax.ShapeDtypeStruct((t, w), dt) for w, dt in zip(widths, dtypes)],
        compiler_params=pltpu.CompilerParams(
            dimension_semantics=("parallel",), vmem_limit_bytes=VMEM_LIMIT),
        name="proj_prep",
    )(proj_small, pos_col, rc, gq, gdq, gkv, gkb, bd)


def _compress_kernel(flat_ref, pe_ref, w1_ref, w2_ref, pos_ref, rc_ref, gain_ref, bd_ref, o_ref):
    bf16 = jnp.bfloat16
    f32 = jnp.float32
    outs = []
    for g in range(NSA_GROUPS):
        xin = (flat_ref[0, 0, g] + pe_ref[0]).astype(bf16)
        hid = jax.nn.gelu(jnp.dot(xin, w1_ref[0], preferred_element_type=f32))
        outs.append(jnp.dot(hid.astype(bf16), w2_ref[0], preferred_element_type=f32))
    out = jnp.concatenate(outs, axis=-1)

    @pl.when(pl.program_id(0) == 0)
    def _():
        cos, sin = _rope_tables(pos_ref[0], rc_ref[...])
        o_ref[0, 0] = _rope_apply(_head_norm(out, gain_ref[...], bd_ref[...]), cos, sin)

    @pl.when(pl.program_id(0) != 0)
    def _():
        o_ref[0, 0] = out


def compress_pallas(kvc, positions, cmp_pos, cmp_w1, cmp_w2, k_gain):
    b, s, _ = kvc.shape
    bf16 = jnp.bfloat16
    n_chunk = s // CMP_STRIDE
    n_cmp = n_chunk - CMP_LEN // CMP_STRIDE + 1
    n_pad = -(-n_cmp // LANE) * LANE
    c = kvc.reshape(b, n_chunk, CMP_STRIDE, 2, NSA_GROUPS, HEAD_DIM).transpose(3, 0, 4, 1, 2, 5)
    c = c.reshape(2, b, NSA_GROUPS, n_chunk, CMP_STRIDE * HEAD_DIM)
    flat = jnp.concatenate([c[..., j:j + n_cmp, :] for j in range(CMP_LEN // CMP_STRIDE)], axis=-1)
    flat = jnp.pad(flat, ((0, 0),) * 3 + ((0, n_pad - n_cmp), (0, 0)))
    pe = cmp_pos.reshape(2, 1, CMP_LEN * HEAD_DIM)
    pos_c = positions[:, CMP_LEN - 1::CMP_STRIDE][:, :n_cmp]
    pos_c = jnp.pad(pos_c, ((0, 0), (0, n_pad - n_cmp))).reshape(b, n_pad, 1)
    gain = jnp.tile(k_gain, NSA_GROUPS).reshape(1, LANE)
    head_of = np.arange(LANE) // HEAD_DIM
    bd = jnp.asarray(head_of[:, None] == head_of[None, :], bf16)
    kdim = CMP_LEN * HEAD_DIM
    return pl.pallas_call(
        _compress_kernel,
        grid=(2, b),
        in_specs=[
            pl.BlockSpec((1, 1, NSA_GROUPS, n_pad, kdim), lambda w, bi: (w, bi, 0, 0, 0)),
            pl.BlockSpec((1, 1, kdim), lambda w, bi: (w, 0, 0)),
            pl.BlockSpec((1, kdim, CMP_HIDDEN), lambda w, bi: (w, 0, 0)),
            pl.BlockSpec((1, CMP_HIDDEN, HEAD_DIM), lambda w, bi: (w, 0, 0)),
            pl.BlockSpec((1, n_pad, 1), lambda w, bi: (bi, 0, 0)),
            pl.BlockSpec((2, LANE), lambda w, bi: (0, 0)),
            pl.BlockSpec((1, LANE), lambda w, bi: (0, 0)),
            pl.BlockSpec((LANE, LANE), lambda w, bi: (0, 0)),
        ],
        out_specs=pl.BlockSpec((1, 1, n_pad, LANE), lambda w, bi: (w, bi, 0, 0)),
        out_shape=jax.ShapeDtypeStruct((2, b, n_pad, LANE), jnp.float32),
        compiler_params=pltpu.CompilerParams(
            dimension_semantics=("arbitrary", "arbitrary"), vmem_limit_bytes=VMEM_LIMIT),
        name="compress",
    )(flat, pe, cmp_w1.astype(bf16), cmp_w2.astype(bf16), pos_c, _rope_lane_constants(), gain, bd)


ROW_TM = 512


def _merge_kernel(x_ref, ya_ref, yb_ref, mg_ref, wa_ref, wb_ref, wo_ref, o_ref):
    f32 = jnp.float32
    a = jnp.dot(ya_ref[...], wa_ref[...], preferred_element_type=f32)
    b = jnp.dot(yb_ref[...], wb_ref[...], preferred_element_type=f32)
    g = jax.nn.sigmoid(mg_ref[...])
    mix = g[:, :D_MODEL] * a + g[:, D_MODEL:] * b
    o_ref[...] = x_ref[...] + jnp.dot(mix.astype(jnp.bfloat16), wo_ref[...],
                                      preferred_element_type=f32)


def merge_pallas(x2d, ya, yb, mg, wa, wb, wo):
    t, d = x2d.shape
    bf16 = jnp.bfloat16
    row = lambda w: pl.BlockSpec((ROW_TM, w), lambda i: (i, 0))
    full = lambda r, c: pl.BlockSpec((r, c), lambda i: (0, 0))
    return pl.pallas_call(
        _merge_kernel,
        grid=(t // ROW_TM,),
        in_specs=[row(d), row(MIX_A), row(MIX_B), row(2 * d),
                  full(MIX_A, d), full(MIX_B, d), full(d, d)],
        out_specs=row(d),
        out_shape=jax.ShapeDtypeStruct((t, d), jnp.float32),
        compiler_params=pltpu.CompilerParams(
            dimension_semantics=("parallel",), vmem_limit_bytes=VMEM_LIMIT),
        name="merge",
    )(x2d, ya, yb, mg, wa.astype(bf16), wb.astype(bf16), wo.astype(bf16))


def _ple_kernel(x_ref, p_ref, wg_ref, wp_ref, gn_ref, o_ref):
    f32 = jnp.float32
    bf16 = jnp.bfloat16
    x = x_ref[...]
    r = x * lax.rsqrt(jnp.mean(x * x, axis=-1, keepdims=True) + EPS)
    gate = jax.nn.sigmoid(jnp.dot(r.astype(bf16), wg_ref[...], preferred_element_type=f32))
    e = jnp.dot(p_ref[...].astype(bf16), wp_ref[...], preferred_element_type=f32)
    e = e * lax.rsqrt(jnp.mean(e * e, axis=-1, keepdims=True) + EPS) * gn_ref[...]
    o_ref[...] = x + gate * e


def ple_pallas(x2d, p2d, wg, wp, gn):
    t, d = x2d.shape
    bf16 = jnp.bfloat16
    row = lambda w: pl.BlockSpec((ROW_TM, w), lambda i: (i, 0))
    full = lambda r, c: pl.BlockSpec((r, c), lambda i: (0, 0))
    return pl.pallas_call(
        _ple_kernel,
        grid=(t // ROW_TM,),
        in_specs=[row(d), row(PLE_DIM), full(d, d), full(PLE_DIM, d), full(1, d)],
        out_specs=row(d),
        out_shape=jax.ShapeDtypeStruct((t, d), jnp.float32),
        compiler_params=pltpu.CompilerParams(
            dimension_semantics=("parallel",), vmem_limit_bytes=VMEM_LIMIT),
        name="ple",
    )(x2d, p2d, wg.astype(bf16), wp.astype(bf16), gn.reshape(1, d))


NSA_TQ = 128
NSA_TK = 512
NSA_TW = 128
NSA_WIN_TILES = (WINDOW + NSA_TQ) // NSA_TW
NSA_SUB = 128


def _flash_update_t(state, s, v_t, mask=None):
    m, l, acc = state
    m_new = jnp.maximum(m, jnp.max(s, axis=0, keepdims=True))
    alpha = jnp.exp(m - m_new)
    p = jnp.exp(s - m_new)
    if mask is not None:
        p = jnp.where(mask, p, 0.0)
    l = alpha * l + jnp.sum(p, axis=0, keepdims=True)
    acc = alpha * acc + jnp.dot(v_t, p.astype(jnp.bfloat16), preferred_element_type=jnp.float32)
    return m_new, l, acc


def _nsa_kernel(q_ref, kc_ref, vct_ref, ks_ref, vst_ref, kw_ref, vwt_ref, g_ref, ov_ref, exp_ref,
                o_ref, bias_ref, *, n_blk, n_top, n_ktiles):
    f32 = jnp.float32
    bf16 = jnp.bfloat16
    qi = pl.program_id(2)
    q0 = qi * NSA_TQ
    t_row = q0 + lax.broadcasted_iota(jnp.int32, (1, NSA_TQ), 1)

    n_cmp_pad = kc_ref.shape[-2]
    cmp_end = lax.broadcasted_iota(jnp.int32, (n_cmp_pad, 1), 0) * CMP_STRIDE + (CMP_LEN - 1)
    mask_c = cmp_end <= t_row
    kc = kc_ref[0, 0]
    vc_t = vct_ref[0, 0]
    p_sum = jnp.zeros((n_cmp_pad, NSA_TQ), f32)
    o_c = []
    for r in range(NSA_REP):
        s = jnp.dot(kc, q_ref[0, r], preferred_element_type=f32)
        s = jnp.where(mask_c, s, NEG)
        m = jnp.max(s, axis=0, keepdims=True)
        p = jnp.where(mask_c, jnp.exp(s - m), 0.0)
        l = jnp.sum(p, axis=0, keepdims=True)
        p = p / jnp.where(l > 0.0, l, 1.0)
        p_sum = p_sum + p
        o_c.append(jnp.dot(vc_t, p.astype(bf16), preferred_element_type=f32))

    p_hi = p_sum.astype(bf16)
    p_lo = (p_sum - p_hi.astype(f32)).astype(bf16)
    ov = ov_ref[...]
    imp_t = (jnp.dot(ov, p_hi, preferred_element_type=f32)
             + jnp.dot(ov, p_lo, preferred_element_type=f32))
    blk = lax.broadcasted_iota(jnp.int32, (n_blk, 1), 0)
    cur = t_row // SEL_LEN
    forced = (blk == 0) | (blk == cur) | (blk == cur - 1)
    admissible = blk * SEL_LEN <= t_row
    imp_t = jnp.where(forced, jnp.inf, imp_t)
    imp_t = jnp.where(admissible, imp_t, -jnp.inf)
    n_chunks = n_blk // SUBLANE
    chunks = [imp_t[c * SUBLANE:(c + 1) * SUBLANE, :] for c in range(n_chunks)]
    ranks = [jnp.zeros((SUBLANE, NSA_TQ), f32) for _ in range(n_chunks)]
    sub = lax.broadcasted_iota(jnp.int32, (SUBLANE, NSA_TQ), 0)
    for i in range(n_blk):
        ci, si = divmod(i, SUBLANE)
        row = jnp.broadcast_to(chunks[ci][si:si + 1, :], (SUBLANE, NSA_TQ))
        for c in range(n_chunks):
            if c > ci:
                beats = jnp.where(row >= chunks[c], 1.0, 0.0)
            elif c < ci:
                beats = jnp.where(row > chunks[c], 1.0, 0.0)
            else:
                tie = jnp.where(sub > si, 1.0, 0.0)
                beats = jnp.where(row > chunks[c], 1.0, jnp.where(row == chunks[c], tie, 0.0))
            ranks[c] = ranks[c] + beats
    rank = jnp.concatenate(ranks, axis=0)
    sel_t = jnp.where((rank < n_top) & admissible, 1.0, 0.0).astype(bf16)

    n_live = qi // (NSA_TK // NSA_TQ) + 1
    for j in range(n_ktiles):
        @pl.when(j < n_live)
        def _():
            hit = jnp.dot(exp_ref[j * NSA_TK:(j + 1) * NSA_TK, :], sel_t,
                          preferred_element_type=f32)
            kpos = j * NSA_TK + lax.broadcasted_iota(jnp.int32, (NSA_TK, 1), 0)
            bias_ref[j] = jnp.where((hit > 0.5) & (kpos <= t_row), 0.0, NEG)

    def init_state():
        return (jnp.full((1, NSA_TQ), NEG, f32), jnp.zeros((1, NSA_TQ), f32),
                jnp.zeros((HEAD_DIM, NSA_TQ), f32))

    def sel_body(j, states):
        states = list(states)
        for c in range(NSA_TK // NSA_SUB):
            rows = slice(c * NSA_SUB, (c + 1) * NSA_SUB)
            k = ks_ref[0, 0, j, rows, :]
            v_t = vst_ref[0, 0, j, :, rows]
            bias = bias_ref[j, rows, :]
            for r in range(NSA_REP):
                s = jnp.dot(k, q_ref[0, r], preferred_element_type=f32) + bias
                states[r] = _flash_update_t(states[r], s, v_t)
        return tuple(states)

    st_s = lax.fori_loop(0, n_live, sel_body, tuple(init_state() for _ in range(NSA_REP)))

    st_w = [init_state() for _ in range(NSA_REP)]
    for w in range(NSA_WIN_TILES):
        tile = qi - (NSA_WIN_TILES - 1) + w
        tix = jnp.maximum(tile, 0)
        k = kw_ref[0, 0, tix]
        v_t = vwt_ref[0, 0, tix]
        kpos = tile * NSA_TW + lax.broadcasted_iota(jnp.int32, (NSA_TW, 1), 0)
        diff = t_row - kpos
        mask = (kpos >= 0) & (diff >= 0) & (diff < WINDOW)
        for r in range(NSA_REP):
            s = jnp.where(mask, jnp.dot(k, q_ref[0, r], preferred_element_type=f32), NEG)
            st_w[r] = _flash_update_t(st_w[r], s, v_t, mask)

    g = g_ref[0, 0]
    outs = []
    for r in range(NSA_REP):
        o_s = st_s[r][2] / st_s[r][1]
        o_w = st_w[r][2] / st_w[r][1]
        outs.append(g[r:r + 1, :] * o_c[r] + g[NSA_REP + r:NSA_REP + r + 1, :] * o_s
                    + g[2 * NSA_REP + r:2 * NSA_REP + r + 1, :] * o_w)
    o_ref[0] = jnp.concatenate(outs, axis=0).T.astype(o_ref.dtype)


def nsa_attention_pallas(q, kc, vc, ks, vs, kw, vw, gates):
    b, s = q.shape[0], q.shape[1]
    bf16 = jnp.bfloat16
    n_cmp = s // CMP_STRIDE - CMP_LEN // CMP_STRIDE + 1
    n_cmp_pad = -(-n_cmp // LANE) * LANE
    n_blk = s // SEL_LEN
    n_top = min(SEL_BLOCKS, n_blk)
    n_kt = s // NSA_TK
    n_wt = s // NSA_TW
    assert s % NSA_TK == 0 and n_blk % SUBLANE == 0

    pad_c = ((0, 0), (0, n_cmp_pad - kc.shape[1]), (0, 0), (0, 0))
    kcp = jnp.pad(kc, pad_c).transpose(0, 2, 1, 3).astype(bf16)
    vct = jnp.pad(vc, pad_c).transpose(0, 2, 3, 1).astype(bf16)

    def key_tiles(k, tk):
        return k.reshape(b, s // tk, tk, NSA_GROUPS, HEAD_DIM).transpose(0, 3, 1, 2, 4).astype(bf16)

    def val_tiles(v, tk):
        return v.reshape(b, s // tk, tk, NSA_GROUPS, HEAD_DIM).transpose(0, 3, 1, 4, 2).astype(bf16)

    gates_t = gates.transpose(0, 3, 2, 4, 1).reshape(b, NSA_GROUPS, 3 * NSA_REP, s)
    q_t = q.reshape(b, s, NSA_HEADS, HEAD_DIM).transpose(0, 2, 3, 1)
    grp = NSA_REP * HEAD_DIM

    ci = np.arange(n_cmp_pad)[None, :] * CMP_STRIDE
    sj = np.arange(n_blk)[:, None] * SEL_LEN
    ov = (ci < sj + SEL_LEN) & (ci + CMP_LEN > sj) & (np.arange(n_cmp_pad)[None, :] < n_cmp)
    expand = (np.arange(s)[:, None] // SEL_LEN) == np.arange(n_blk)[None, :]

    per_bg = lambda bi, gi, i: (bi, gi, 0, 0)
    per_bg5 = lambda bi, gi, i: (bi, gi, 0, 0, 0)
    per_q = lambda bi, gi, i: (bi, gi, 0, i)
    return pl.pallas_call(
        functools.partial(_nsa_kernel, n_blk=n_blk, n_top=n_top, n_ktiles=n_kt),
        grid=(b, NSA_GROUPS, s // NSA_TQ),
        in_specs=[
            pl.BlockSpec((1, NSA_REP, HEAD_DIM, NSA_TQ), per_q),
            pl.BlockSpec((1, 1, n_cmp_pad, HEAD_DIM), per_bg),
            pl.BlockSpec((1, 1, HEAD_DIM, n_cmp_pad), per_bg),
            pl.BlockSpec((1, 1, n_kt, NSA_TK, HEAD_DIM), per_bg5),
            pl.BlockSpec((1, 1, n_kt, HEAD_DIM, NSA_TK), per_bg5),
            pl.BlockSpec((1, 1, n_wt, NSA_TW, HEAD_DIM), per_bg5),
            pl.BlockSpec((1, 1, n_wt, HEAD_DIM, NSA_TW), per_bg5),
            pl.BlockSpec((1, 1, 3 * NSA_REP, NSA_TQ), per_q),
            pl.BlockSpec((n_blk, n_cmp_pad), lambda bi, gi, i: (0, 0)),
            pl.BlockSpec((s, n_blk), lambda bi, gi, i: (0, 0)),
        ],
        out_specs=pl.BlockSpec((1, NSA_TQ, grp), lambda bi, gi, i: (bi, i, gi)),
        out_shape=jax.ShapeDtypeStruct((b, s, MIX_A), bf16),
        scratch_shapes=[pltpu.VMEM((n_kt, NSA_TK, NSA_TQ), jnp.float32)],
        compiler_params=pltpu.CompilerParams(
            dimension_semantics=("parallel", "parallel", "arbitrary"),
            vmem_limit_bytes=VMEM_LIMIT),
        name="nsa_attention",
    )(q_t, kcp, vct, key_tiles(ks, NSA_TK), val_tiles(vs, NSA_TK), key_tiles(kw, NSA_TW),
      val_tiles(vw, NSA_TW), gates_t, jnp.asarray(ov, bf16), jnp.asarray(expand, bf16))


DSA_TQ = 128
DSA_TK = 512
DSA_SUB = 128
F32_ORDER_MASK = 0x7FFFFFFF
F32_NEG_INF_BITS = -8388608
F32_POS_INF_BITS = 0x7F800000


def _ordered_from_bits(bits):
    return jnp.where(bits >= 0, bits, bits ^ F32_ORDER_MASK)


def _sublane_group_sum(x):
    ways = 4
    g = x.reshape(ways, x.shape[0] // (SUBLANE * ways), SUBLANE, x.shape[1])
    return jnp.sum(jnp.sum(g, axis=1), axis=0)


def _dsa_kernel(iq_ref, w_ref, ik_ref, q_ref, k_ref, vt_ref, tril_ref, o_ref, sc_ref, *, n_keep):
    f32 = jnp.float32
    bf16 = jnp.bfloat16
    qi = pl.program_id(1)
    q0 = qi * DSA_TQ
    t_row = q0 + lax.broadcasted_iota(jnp.int32, (1, DSA_TQ), 1)
    n_live = qi // (DSA_TK // DSA_TQ) + 1
    key_off = lax.broadcasted_iota(jnp.int32, (DSA_TK, 1), 0)

    w = w_ref[0]

    def score_body(j, carry):
        ik = ik_ref[0, j]
        acc = jnp.zeros((DSA_TK, DSA_TQ), f32)
        for h in range(IDX_HEADS):
            logit = jnp.dot(ik, iq_ref[0, h], preferred_element_type=f32)
            acc = acc + jnp.maximum(logit, 0.0) * w[h:h + 1, :]
        causal = (j * DSA_TK + key_off) <= t_row
        sc_ref[j] = jnp.where(causal, acc + 0.0, -jnp.inf)
        return carry

    lax.fori_loop(0, n_live, score_body, 0)

    def count(pred):
        def body(j, acc):
            return acc + _sublane_group_sum(jnp.where(pred(sc_ref[j]), 1.0, 0.0))
        part = lax.fori_loop(0, n_live, body, jnp.zeros((SUBLANE, DSA_TQ), f32))
        return jnp.sum(part, axis=0, keepdims=True)

    def bisect_body(_, lohi):
        lo, hi = lohi
        mid = (lo >> 1) + (hi >> 1) + (lo & hi & 1)
        thr = lax.bitcast_convert_type(_ordered_from_bits(mid), f32)
        ok = count(lambda sc: sc >= thr) >= n_keep
        return jnp.where(ok, mid, lo), jnp.where(ok, hi, mid)

    lo0 = jnp.full((1, DSA_TQ), F32_NEG_INF_BITS ^ F32_ORDER_MASK, jnp.int32)
    hi0 = jnp.full((1, DSA_TQ), F32_POS_INF_BITS + 1, jnp.int32)
    lo, _ = lax.fori_loop(0, 32, bisect_body, (lo0, hi0))
    thr = lax.bitcast_convert_type(_ordered_from_bits(lo), f32)
    need = n_keep - count(lambda sc: sc > thr)

    tril = tril_ref[...]

    def mask_body(j, ties_before):
        sc = sc_ref[j]
        eq = sc == thr
        pref = ties_before + jnp.dot(tril, jnp.where(eq, 1.0, 0.0).astype(bf16),
                                     preferred_element_type=f32)
        keep_tie = jnp.where(pref <= need, 0.0, NEG)
        bias = jnp.where(sc > thr, 0.0, jnp.where(eq, keep_tie, NEG))
        causal = (j * DSA_TK + key_off) <= t_row
        sc_ref[j] = jnp.where(causal, bias, NEG)
        return pref[DSA_TK - 1:DSA_TK, :]

    lax.fori_loop(0, n_live, mask_body, jnp.zeros((1, DSA_TQ), f32))

    def att_body(j, states):
        states = list(states)
        for c in range(DSA_TK // DSA_SUB):
            rows = slice(c * DSA_SUB, (c + 1) * DSA_SUB)
            k = k_ref[0, j, rows, :]
            v_t = vt_ref[0, j, :, rows]
            bias = sc_ref[j, rows, :]
            for h in range(DSA_HEADS):
                s = jnp.dot(k, q_ref[0, h], preferred_element_type=f32) + bias
                states[h] = _flash_update_t(states[h], s, v_t)
        return tuple(states)

    init = tuple((jnp.full((1, DSA_TQ), NEG, f32), jnp.zeros((1, DSA_TQ), f32),
                  jnp.zeros((HEAD_DIM, DSA_TQ), f32)) for _ in range(DSA_HEADS))
    st = lax.fori_loop(0, n_live, att_body, init)
    out_t = jnp.concatenate([st[h][2] / st[h][1] for h in range(DSA_HEADS)], axis=0)
    o_ref[0] = out_t.T.astype(o_ref.dtype)


def dsa_attention_pallas(q2, k, v, iq2, ik, w):
    b, s = q2.shape[0], q2.shape[1]
    bf16 = jnp.bfloat16
    n_keep = min(DSA_TOPK, s // 4)
    n_kt = s // DSA_TK
    assert s % DSA_TK == 0

    def key_tiles(x):
        return x.reshape(b, n_kt, DSA_TK, x.shape[-1]).astype(bf16)

    v_t = v.reshape(b, n_kt, DSA_TK, HEAD_DIM).transpose(0, 1, 3, 2).astype(bf16)
    tril = np.arange(DSA_TK)[:, None] >= np.arange(DSA_TK)[None, :]

    def heads_t(x, nh, dh):
        return x.reshape(b, s, nh, dh).transpose(0, 2, 3, 1)

    per_b = lambda bi, i: (bi, 0, 0, 0)
    per_q = lambda bi, i: (bi, 0, 0, i)
    q_rows = lambda width: pl.BlockSpec((1, DSA_TQ, width), lambda bi, i: (bi, i, 0))
    return pl.pallas_call(
        functools.partial(_dsa_kernel, n_keep=n_keep),
        grid=(b, s // DSA_TQ),
        in_specs=[
            pl.BlockSpec((1, IDX_HEADS, IDX_DIM, DSA_TQ), per_q),
            pl.BlockSpec((1, IDX_HEADS, DSA_TQ), lambda bi, i: (bi, 0, i)),
            pl.BlockSpec((1, n_kt, DSA_TK, IDX_DIM), per_b),
            pl.BlockSpec((1, DSA_HEADS, HEAD_DIM, DSA_TQ), per_q),
            pl.BlockSpec((1, n_kt, DSA_TK, HEAD_DIM), per_b),
            pl.BlockSpec((1, n_kt, HEAD_DIM, DSA_TK), per_b),
            pl.BlockSpec((DSA_TK, DSA_TK), lambda bi, i: (0, 0)),
        ],
        out_specs=q_rows(MIX_B),
        out_shape=jax.ShapeDtypeStruct((b, s, MIX_B), bf16),
        scratch_shapes=[pltpu.VMEM((n_kt, DSA_TK, DSA_TQ), jnp.float32)],
        compiler_params=pltpu.CompilerParams(
            dimension_semantics=("parallel", "arbitrary"),
            vmem_limit_bytes=VMEM_LIMIT),
        name="dsa_attention",
    )(heads_t(iq2, IDX_HEADS, IDX_DIM), w.transpose(0, 2, 1), key_tiles(ik),
      heads_t(q2, DSA_HEADS, HEAD_DIM), key_tiles(k), v_t, jnp.asarray(tril, bf16))


PEER_TT = 128
PEER_SLOTS = PEER_HEADS * PEER_TOPK
PEER_CAND_A0 = PEER_HALF_TOPK
PEER_CAND_SQ = SUBLANE


def _peer_cand_flat_ids():
    ids = [0 * PEER_HALF_TOPK + bb for bb in range(PEER_CAND_A0)]
    for a in range(1, PEER_CAND_SQ):
        ids += [a * PEER_HALF_TOPK + bb for bb in range(PEER_CAND_SQ)]
    ids += [a * PEER_HALF_TOPK for a in range(PEER_CAND_SQ, PEER_HALF_TOPK)]
    return np.asarray(ids, np.int32)


def _extract_top(cur, ids, n):
    vals, picks = [], []
    for _ in range(n):
        m = jnp.max(cur, axis=0, keepdims=True)
        pick = jnp.min(jnp.where(cur == m, ids, jnp.int32(2 ** 30)), axis=0, keepdims=True)
        vals.append(m)
        picks.append(pick)
        cur = jnp.where(ids == pick, -jnp.inf, cur)
    return vals, picks


def _pair_grid(first, second, op):
    pieces = [op(first[0:1], second)]
    for a in range(1, PEER_CAND_SQ):
        pieces.append(op(first[a:a + 1], second[0:PEER_CAND_SQ]))
    pieces.append(op(first[PEER_CAND_SQ:], second[0:1]))
    return jnp.concatenate(pieces, axis=0)


def _peer_topk_kernel(x_ref, g_ref, wq_ref, sk_ref, fid_ref, h_ref, eidx_ref, gate_ref):
    f32 = jnp.float32
    bf16 = jnp.bfloat16
    x = x_ref[...]
    h = x * lax.rsqrt(jnp.mean(x * x, axis=-1, keepdims=True) + EPS) * g_ref[...]
    h_ref[...] = h
    q = jnp.dot(h.astype(bf16), wq_ref[...], preferred_element_type=f32).astype(bf16)
    key_ids = lax.broadcasted_iota(jnp.int32, (N_KEYS, PEER_TT), 0)
    fid = fid_ref[...]
    nt_dims = (((1,), (1,)), ((), ()))
    half = PEER_QDIM // 2
    for hd in range(PEER_HEADS):
        tops = []
        for c in range(2):
            col = (hd * 2 + c) * half
            s_t = lax.dot_general(sk_ref[c], q[:, col:col + half], nt_dims,
                                  preferred_element_type=f32)
            vals, picks = _extract_top(s_t, key_ids, PEER_HALF_TOPK)
            tops.append((jnp.concatenate(vals, axis=0), jnp.concatenate(picks, axis=0)))
        (v1, i1), (v2, i2) = tops
        cand = _pair_grid(v1, v2, lambda a, b: a + b)
        cexp = _pair_grid(i1, i2, lambda a, b: a * N_KEYS + b)
        vals, picks = _extract_top(cand, fid, PEER_TOPK)
        top = jnp.concatenate(vals, axis=0)
        eids = [jnp.max(jnp.where(fid == p, cexp, -1), axis=0, keepdims=True) for p in picks]
        ex = jnp.exp(top - top[0:1])
        gate = ex / jnp.sum(ex, axis=0, keepdims=True)
        eidx_ref[0, hd * PEER_TOPK:(hd + 1) * PEER_TOPK, :] = jnp.concatenate(eids, axis=0)
        gate_ref[0, hd * PEER_TOPK:(hd + 1) * PEER_TOPK, :] = gate


def peer_topk_pallas(x2d, ffn_norm, wq, sub_keys):
    t, d = x2d.shape
    n_tiles = t // PEER_TT
    fid = np.broadcast_to(_peer_cand_flat_ids()[:, None], (_peer_cand_flat_ids().shape[0], PEER_TT))
    n_cand = fid.shape[0]
    return pl.pallas_call(
        _peer_topk_kernel,
        grid=(n_tiles,),
        in_specs=[
            pl.BlockSpec((PEER_TT, d), lambda i: (i, 0)),
            pl.BlockSpec((1, d), lambda i: (0, 0)),
            pl.BlockSpec((d, PEER_HEADS * PEER_QDIM), lambda i: (0, 0)),
            pl.BlockSpec((2, N_KEYS, PEER_QDIM // 2), lambda i: (0, 0, 0)),
            pl.BlockSpec((n_cand, PEER_TT), lambda i: (0, 0)),
        ],
        out_specs=[
            pl.BlockSpec((PEER_TT, d), lambda i: (i, 0)),
            pl.BlockSpec((1, PEER_SLOTS, PEER_TT), lambda i: (i, 0, 0)),
            pl.BlockSpec((1, PEER_SLOTS, PEER_TT), lambda i: (i, 0, 0)),
        ],
        out_shape=[
            jax.ShapeDtypeStruct((t, d), jnp.float32),
            jax.ShapeDtypeStruct((n_tiles, PEER_SLOTS, PEER_TT), jnp.int32),
            jax.ShapeDtypeStruct((n_tiles, PEER_SLOTS, PEER_TT), jnp.float32),
        ],
        compiler_params=pltpu.CompilerParams(
            dimension_semantics=("parallel",), vmem_limit_bytes=VMEM_LIMIT),
        name="peer_topk",
    )(x2d, ffn_norm.reshape(1, d), wq.astype(jnp.bfloat16), sub_keys.astype(jnp.bfloat16),
      jnp.asarray(fid))


PEER_GT = 8
PEER_ROWS = PEER_GT * PEER_SLOTS
PEER_NSLOT = 2
PEER_DMA_PATTERN = (0, 0, 1, 0, 1, 0, 0, 1)


def _peer_eval_kernel(idx_ref, h_ref, gate_ref, x_ref, uv_hbm, o_ref, buf, sem):
    j = pl.program_id(0)
    n_blocks = pl.num_programs(0) - 1
    lane = lax.broadcasted_iota(jnp.int32, (PEER_SLOTS, PEER_TT), 1)
    lanes_per_block = PEER_NSLOT * PEER_GT
    lane0 = ((j - 1) % (PEER_TT // lanes_per_block)) * lanes_per_block

    def evaluate(slot, tok):
        row0 = slot * PEER_GT + tok
        rows = buf[slot, tok * PEER_SLOTS:(tok + 1) * PEER_SLOTS, :]
        act = jnp.sum(rows[:, :D_MODEL] * h_ref[row0:row0 + 1, :], axis=-1, keepdims=True)
        gate = jnp.sum(jnp.where(lane == lane0 + row0, gate_ref[0], 0.0), axis=-1, keepdims=True)
        wgt = gate * jax.nn.gelu(act)
        o_ref[row0:row0 + 1, :] = (x_ref[row0:row0 + 1, :]
                                   + jnp.sum(wgt * rows[:, D_MODEL:], axis=0, keepdims=True))

    def issue(slot, tok):
        for r in range(tok * PEER_SLOTS, (tok + 1) * PEER_SLOTS):
            e = idx_ref[0, 0, slot * PEER_ROWS + r]
            pltpu.async_copy(uv_hbm.at[e], buf.at[slot, pl.ds(r, 1)], sem.at[slot],
                             priority=PEER_DMA_PATTERN[r % len(PEER_DMA_PATTERN)])

    def step(do_evaluate, do_issue):
        for slot in range(PEER_NSLOT):
            if do_evaluate:
                pltpu.make_async_copy(uv_hbm.at[pl.ds(0, PEER_ROWS), 0], buf.at[slot],
                                      sem.at[slot]).wait()
            for tok in range(PEER_GT):
                if do_evaluate:
                    evaluate(slot, tok)
                if do_issue:
                    issue(slot, tok)

    @pl.when(j == 0)
    def _():
        step(False, True)

    @pl.when((j > 0) & (j < n_blocks))
    def _():
        step(True, True)

    @pl.when(j == n_blocks)
    def _():
        step(True, False)


def peer_eval_pallas(x2d, h2d, eidx_t, gate_t, u, v):
    t, d = x2d.shape
    blk = PEER_NSLOT * PEER_GT
    n_blocks = t // blk
    uv = jnp.concatenate([u, v], axis=1).reshape(u.shape[0], 1, 2 * d)
    idx = eidx_t.transpose(0, 2, 1).reshape(n_blocks, 1, PEER_NSLOT * PEER_ROWS)
    per_tile = PEER_TT // blk
    prev = lambda j: jnp.maximum(j - 1, 0)
    return pl.pallas_call(
        _peer_eval_kernel,
        grid=(n_blocks + 1,),
        in_specs=[
            pl.BlockSpec((1, 1, PEER_NSLOT * PEER_ROWS),
                         lambda j: (jnp.minimum(j, n_blocks - 1), 0, 0), memory_space=pltpu.SMEM),
            pl.BlockSpec((blk, d), lambda j: (prev(j), 0)),
            pl.BlockSpec((1, PEER_SLOTS, PEER_TT), lambda j: (prev(j) // per_tile, 0, 0)),
            pl.BlockSpec((blk, d), lambda j: (prev(j), 0)),
            pl.BlockSpec(memory_space=pl.ANY),
        ],
        out_specs=pl.BlockSpec((blk, d), lambda j: (prev(j), 0)),
        out_shape=jax.ShapeDtypeStruct((t, d), jnp.float32),
        scratch_shapes=[pltpu.VMEM((PEER_NSLOT, PEER_ROWS, 2 * d), jnp.float32),
                        pltpu.SemaphoreType.DMA((PEER_NSLOT,))],
        compiler_params=pltpu.CompilerParams(
            dimension_semantics=("arbitrary",), vmem_limit_bytes=VMEM_LIMIT),
        name="peer_eval",
    )(idx, h2d, gate_t, x2d, uv)


def peer_pallas(x, ffn_norm, wq, sub_keys, u, v):
    b, s, d = x.shape
    x2d = x.reshape(b * s, d)
    h2d, eidx_t, gate_t = peer_topk_pallas(x2d, ffn_norm, wq, sub_keys)
    return peer_eval_pallas(x2d, h2d, eidx_t, gate_t, u, v).reshape(b, s, d)


def hybrid_layer(x, p_i, positions, attn_norm, w_in, nsa_qk_gain, cmp_pos, cmp_w1, cmp_w2,
                 dsa_qk_gain, w_branch_a, w_branch_b, w_out, ffn_norm, peer_wq, peer_sub_keys,
                 peer_u, peer_v, ple_w, ple_gate_w, ple_norm):
    b, s, d = x.shape
    t = b * s
    bf16 = jnp.bfloat16
    x2d = x.reshape(t, d)

    small_cols, mg_cols = _in_proj_column_order()
    w_small = jnp.pad(w_in[:, small_cols], ((0, 0), (0, PROJ_SMALL - small_cols.size))).astype(bf16)
    proj_small = norm_matmul(x2d, attn_norm, w_small, tn=PROJ_SMALL // 5)
    mg = norm_matmul(x2d, attn_norm, w_in[:, mg_cols].astype(bf16))
    qa, qb, iq, kvc, ks, vs, kw, vw, kbvb, misc = prep_pallas(
        proj_small, positions.reshape(t, 1), nsa_qk_gain, dsa_qk_gain)

    cmp = compress_pallas(kvc.reshape(b, s, -1), positions, cmp_pos, cmp_w1, cmp_w2, nsa_qk_gain[1])
    per_group = lambda a: a.reshape(b, -1, NSA_GROUPS, HEAD_DIM)
    gates = misc[:, IDX_DIM + IDX_HEADS:IDX_DIM + IDX_HEADS + 3 * NSA_HEADS]
    ya = nsa_attention_pallas(qa.reshape(b, s, MIX_A), per_group(cmp[0]), per_group(cmp[1]),
                              per_group(ks), per_group(vs), per_group(kw), per_group(vw),
                              gates.reshape(b, s, 3, NSA_GROUPS, NSA_REP))

    kbvb = kbvb.reshape(b, s, 2, HEAD_DIM)
    yb = dsa_attention_pallas(qb.reshape(b, s, MIX_B), kbvb[:, :, 0], kbvb[:, :, 1],
                              iq.reshape(b, s, -1), misc[:, :IDX_DIM].reshape(b, s, IDX_DIM),
                              misc[:, IDX_DIM:IDX_DIM + IDX_HEADS].reshape(b, s, IDX_HEADS))

    x2d = merge_pallas(x2d, ya.reshape(t, MIX_A), yb.reshape(t, MIX_B), mg,
                       w_branch_a, w_branch_b, w_out)
    x2d = peer_pallas(x2d.reshape(b, s, d), ffn_norm, peer_wq, peer_sub_keys, peer_u, peer_v)
    x2d = ple_pallas(x2d.reshape(t, d), p_i.reshape(t, PLE_DIM), ple_gate_w, ple_w, ple_norm)
    return x2d.reshape(b, s, d)


def kernel(x, p, positions, attn_norm, w_in, nsa_qk_gain, cmp_pos, cmp_w1, cmp_w2,
           dsa_qk_gain, w_branch_a, w_branch_b, w_out, ffn_norm, peer_wq, peer_sub_keys,
           peer_u, peer_v, ple_w, ple_gate_w, ple_norm):
    for i in range(DEPTH):
        x = hybrid_layer(x, p[i], positions, attn_norm[i], w_in[i], nsa_qk_gain[i], cmp_pos[i],
                         cmp_w1[i], cmp_w2[i], dsa_qk_gain[i], w_branch_a[i], w_branch_b[i],
                         w_out[i], ffn_norm[i], peer_wq[i], peer_sub_keys[i], peer_u[i],
                         peer_v[i], ple_w[i], ple_gate_w[i], ple_norm[i])
    return x
```
